```python
import functools
import jax, jax.numpy as jnp
from jax import lax
import numpy as np

D_MODEL = 1024
BATCH = 8
SEQ = 4096
DEPTH = 1

GRID_W = 64
CTX_LEN = 256
EPS = 1e-6
N_MOD = 6
D_RNN = 1024
RG_BLOCKS = 8
RG_BLOCK = D_RNN // RG_BLOCKS
CONV_W = 4
CONV_PAD_L = 2
RG_C = 8.0
MLA_HEADS = 8
QK_NOPE = 64
QK_ROPE = 32
V_HEAD = 64
Q_LORA = 256
KV_LORA = 128
ROPE_BASE = 10000.0
ATTN_SCALE = (QK_NOPE + QK_ROPE) ** -0.5
Q_BLOCK = 128
N_BRANCH = 2
N_EXPERTS = 16
CAPACITY_FACTOR = 2
D_EXPERT = 1024
SPLIT_POINTS = (D_RNN, 2 * D_RNN, 2 * D_RNN + Q_LORA, 2 * D_RNN + Q_LORA + KV_LORA, 2 * D_RNN + Q_LORA + KV_LORA + QK_ROPE)
D_IN = 2 * D_RNN + Q_LORA + KV_LORA + QK_ROPE + N_BRANCH * D_MODEL

kernel_name = "hybrid_rglru_mla_ecmoe_diffusion_layer"


def rmsnorm(x, g):
    xf = x.astype(jnp.float32)
    y = xf * lax.rsqrt(jnp.mean(xf * xf, axis=-1, keepdims=True) + EPS)
    return (y * g.astype(jnp.float32)).astype(x.dtype)


def modulate(x, shift, scale):
    return x * (1 + scale) + shift


def axial_rope_tables(n_tokens, dtype):
    rows = n_tokens // GRID_W
    row = jnp.repeat(jnp.arange(rows, dtype=jnp.float32), GRID_W)
    col = jnp.tile(jnp.arange(GRID_W, dtype=jnp.float32), rows)
    n_pairs = QK_ROPE // 4
    inv_freq = ROPE_BASE ** (-jnp.arange(n_pairs, dtype=jnp.float32) / n_pairs)
    ang = jnp.concatenate([row[:, None] * inv_freq, col[:, None] * inv_freq], axis=-1)
    return jnp.cos(ang).astype(dtype), jnp.sin(ang).astype(dtype)


def apply_rope(x, cos, sin):
    x1, x2 = jnp.split(x, 2, axis=-1)
    return jnp.concatenate([x1 * cos - x2 * sin, x1 * sin + x2 * cos], axis=-1)


def centred_depthwise_conv(x, w, b):
    L = x.shape[1]
    xp = jnp.pad(x, ((0, 0), (CONV_PAD_L, CONV_W - 1 - CONV_PAD_L), (0, 0)))
    return sum(xp[:, k:k + L] * w[k] for k in range(CONV_W)) + b


def block_diag_linear(x, w, b):
    xb = x.reshape(*x.shape[:-1], RG_BLOCKS, RG_BLOCK)
    return jnp.einsum('blni,nij->blnj', xb, w).reshape(x.shape) + b


def rglru_coeffs(x, w_a, b_a, w_x, b_x, lam):
    r = jax.nn.sigmoid(block_diag_linear(x, w_a, b_a))
    i = jax.nn.sigmoid(block_diag_linear(x, w_x, b_x))
    log_a = -RG_C * r * jax.nn.softplus(-lam)
    a = jnp.exp(log_a)
    return a, jnp.sqrt(-jnp.expm1(2.0 * log_a)) * (i * x)


def linear_scan(a, b, h0, reverse):
    def comb(l, r):
        return (l[0] * r[0], r[0] * l[1] + r[1])
    a_cum, b_cum = lax.associative_scan(comb, (a, b), axis=1, reverse=reverse)
    return a_cum * h0[:, None, :] + b_cum


def rglru_bidir(xr_c, xr_l, p):
    B = xr_l.shape[0]
    hs_c, hs_l = [], []
    for d, rev in enumerate((False, True)):
        coeff = functools.partial(rglru_coeffs, w_a=p['rg_wa'][d], b_a=p['rg_ba'][d],
                                  w_x=p['rg_wx'][d], b_x=p['rg_bx'][d], lam=p['rg_lambda'][d])
        a_c, b_c = coeff(xr_c)
        h_c = linear_scan(a_c, b_c, jnp.zeros((B, D_RNN), jnp.float32), rev)
        h_last = h_c[:, 0] if rev else h_c[:, -1]
        a_l, b_l = coeff(xr_l)
        hs_c.append(h_c)
        hs_l.append(linear_scan(a_l, b_l, h_last, rev))
    return hs_c[0] + hs_c[1], hs_l[0] + hs_l[1]


def mla_queries(zq, p, cos, sin):
    B, L, _ = zq.shape
    q = (rmsnorm(zq, p['g_q_lora']) @ p['w_uq']).reshape(B, L, MLA_HEADS, QK_NOPE + QK_ROPE)
    q_nope = rmsnorm(q[..., :QK_NOPE], p['g_q_nope'])
    q_rope = rmsnorm(q[..., QK_NOPE:], p['g_q_rope'])
    if cos is not None:
        q_rope = apply_rope(q_rope, cos[:, None, :], sin[:, None, :])
    return q_nope, q_rope


def mla_keys(zkv, zkr, p, cos, sin):
    B, L, _ = zkv.shape
    ckv = rmsnorm(zkv, p['g_kv_lora'])
    k_nope = rmsnorm((ckv @ p['w_uk']).reshape(B, L, MLA_HEADS, QK_NOPE), p['g_k_nope'])
    v = (ckv @ p['w_uv']).reshape(B, L, MLA_HEADS, V_HEAD)
    k_rope = rmsnorm(zkr, p['g_k_rope'])
    if cos is not None:
        k_rope = apply_rope(k_rope, cos, sin)
    return k_nope, k_rope, v


def attend(q_nope, q_rope, k_nope, k_rope, v):
    s = (jnp.einsum('bqhd,bkhd->bhqk', q_nope, k_nope)
         + jnp.einsum('bqhr,bkr->bhqk', q_rope, k_rope)).astype(jnp.float32) * ATTN_SCALE
    prob = jax.nn.softmax(s, axis=-1).astype(v.dtype)
    return jnp.einsum('bhqk,bkhd->bqhd', prob, v)


def blocked_attention(q_nope, q_rope, k_nope, k_rope, v):
    B, L = q_nope.shape[:2]
    nb = L // Q_BLOCK

    def to_blocks(t):
        return jnp.moveaxis(t.reshape(B, nb, Q_BLOCK, *t.shape[2:]), 1, 0)

    out = lax.map(lambda qs: attend(qs[0], qs[1], k_nope, k_rope, v), (to_blocks(q_nope), to_blocks(q_rope)))
    return jnp.moveaxis(out, 0, 1).reshape(B, L, MLA_HEADS * V_HEAD)


def merge_branches(rnn, attn, zg, p):
    B, L, _ = zg.shape
    g = jax.nn.sigmoid(zg).reshape(B, L, N_BRANCH, D_MODEL)
    merged = g[:, :, 0] * (rnn @ p['w_rnn_out']) + g[:, :, 1] * (attn @ p['w_mla_out'])
    return merged @ p['w_o']


def ec_moe(h, p):
    B, L, _ = h.shape
    cap = CAPACITY_FACTOR * L // N_EXPERTS
    aff = jax.nn.softmax((h @ p['w_router']).astype(jnp.float32), axis=-1)
    gate, idx = lax.top_k(jnp.swapaxes(aff, 1, 2), cap)
    bidx = jnp.arange(B)[:, None, None]
    xg = h[bidx, idx]
    hid = jax.nn.silu(jnp.einsum('becd,edf->becf', xg, p['w_e_gate'])) * jnp.einsum('becd,edf->becf', xg, p['w_e_up'])
    y = jnp.einsum('becf,efd->becd', hid, p['w_e_down']) * gate[..., None].astype(h.dtype)
    return jnp.zeros_like(h).at[bidx, idx].add(y)


def layer(x, ctx, c, c_ctx, p, update_ctx):
    B, S, _ = x.shape
    Lc = ctx.shape[1]
    mod_l = jnp.split((jax.nn.silu(c) @ p['w_ada'] + p['b_ada'])[:, None, :], N_MOD, axis=-1)
    mod_c = jnp.split(jax.nn.silu(c_ctx) @ p['w_ada'] + p['b_ada'], N_MOD, axis=-1)
    cos, sin = axial_rope_tables(S, x.dtype)

    z_l = jnp.split(modulate(rmsnorm(x, p['g_norm1']), mod_l[0], mod_l[1]) @ p['w_in'], SPLIT_POINTS, axis=-1)
    z_c = jnp.split(modulate(rmsnorm(ctx, p['g_norm1']), mod_c[0], mod_c[1]) @ p['w_in'], SPLIT_POINTS, axis=-1)

    xr_l = centred_depthwise_conv(z_l[0], p['conv_w'], p['conv_b']).astype(jnp.float32)
    xr_c = centred_depthwise_conv(z_c[0], p['conv_w'], p['conv_b']).astype(jnp.float32)
    h_c, h_l = rglru_bidir(xr_c, xr_l, p)
    rnn_l = (h_l * jax.nn.gelu(z_l[1].astype(jnp.float32))).astype(x.dtype)

    qn_l, qr_l = mla_queries(z_l[2], p, cos, sin)
    kn_l, kr_l, v_l = mla_keys(z_l[3], z_l[4], p, cos, sin)
    kn_c, kr_c, v_c = mla_keys(z_c[3], z_c[4], p, None, None)
    attn_l = blocked_attention(qn_l, qr_l, jnp.concatenate([kn_c, kn_l], axis=1),
                               jnp.concatenate([kr_c, kr_l], axis=1), jnp.concatenate([v_c, v_l], axis=1))

    x_new = x + mod_l[2] * merge_branches(rnn_l, attn_l, z_l[5], p)
    x_new = x_new + mod_l[5] * ec_moe(modulate(rmsnorm(x_new, p['g_norm2']), mod_l[3], mod_l[4]), p)

    if update_ctx:
        rnn_c = (h_c * jax.nn.gelu(z_c[1].astype(jnp.float32))).astype(ctx.dtype)
        qn_c, qr_c = mla_queries(z_c[2], p, None, None)
        attn_c = attend(qn_c, qr_c, kn_c, kr_c, v_c).reshape(B, Lc, MLA_HEADS * V_HEAD)
        ctx = ctx + mod_c[2] * merge_branches(rnn_c, attn_c, z_c[5], p)
        ctx = ctx + mod_c[5] * ec_moe(modulate(rmsnorm(ctx, p['g_norm2']), mod_c[3], mod_c[4]), p)
    return x_new, ctx


def setup_inputs(seed: int = 0) -> dict:
    key = jax.random.key(seed)
    ks = iter(jax.random.split(key, 40))

    def nrm(shape, fan_in, scale=1.0):
        return scale * fan_in ** -0.5 * jax.random.normal(next(ks), shape, jnp.float32)

    def gain(shape):
        return 1.0 + 0.05 * jax.random.normal(next(ks), shape, jnp.float32)

    def small(shape):
        return 0.02 * jax.random.normal(next(ks), shape, jnp.float32)

    a0 = jax.random.uniform(next(ks), (DEPTH, 2, D_RNN), jnp.float32, minval=0.9, maxval=0.999)
    return {
        "x": jax.random.normal(next(ks), (BATCH, SEQ, D_MODEL), jnp.float32),
        "c": jax.random.normal(next(ks), (BATCH, D_MODEL), jnp.float32),
        "ctx": jax.random.normal(next(ks), (BATCH, CTX_LEN, D_MODEL), jnp.float32),
        "c_ctx": jax.random.normal(next(ks), (D_MODEL,), jnp.float32),
        "w_ada": nrm((DEPTH, D_MODEL, N_MOD * D_MODEL), D_MODEL, 0.5),
        "b_ada": small((DEPTH, N_MOD * D_MODEL)),
        "g_norm1": gain((DEPTH, D_MODEL)),
        "g_norm2": gain((DEPTH, D_MODEL)),
        "w_in": nrm((DEPTH, D_MODEL, D_IN), D_MODEL),
        "conv_w": nrm((DEPTH, CONV_W, D_RNN), CONV_W),
        "conv_b": small((DEPTH, D_RNN)),
        "rg_wa": nrm((DEPTH, 2, RG_BLOCKS, RG_BLOCK, RG_BLOCK), RG_BLOCK),
        "rg_ba": small((DEPTH, 2, D_RNN)),
        "rg_wx": nrm((DEPTH, 2, RG_BLOCKS, RG_BLOCK, RG_BLOCK), RG_BLOCK),
        "rg_bx": small((DEPTH, 2, D_RNN)),
        "rg_lambda": jnp.log(a0) - jnp.log1p(-a0),
        "w_rnn_out": nrm((DEPTH, D_RNN, D_MODEL), D_RNN),
        "g_q_lora": gain((DEPTH, Q_LORA)),
        "w_uq": nrm((DEPTH, Q_LORA, MLA_HEADS * (QK_NOPE + QK_ROPE)), Q_LORA),
        "g_kv_lora": gain((DEPTH, KV_LORA)),
        "w_uk": nrm((DEPTH, KV_LORA, MLA_HEADS * QK_NOPE), KV_LORA),
        "w_uv": nrm((DEPTH, KV_LORA, MLA_HEADS * V_HEAD), KV_LORA),
        "g_q_nope": gain((DEPTH, QK_NOPE)),
        "g_q_rope": gain((DEPTH, QK_ROPE)),
        "g_k_nope": gain((DEPTH, QK_NOPE)),
        "g_k_rope": gain((DEPTH, QK_ROPE)),
        "w_mla_out": nrm((DEPTH, MLA_HEADS * V_HEAD, D_MODEL), MLA_HEADS * V_HEAD),
        "w_o": nrm((DEPTH, D_MODEL, D_MODEL), D_MODEL),
        "w_router": nrm((DEPTH, D_MODEL, N_EXPERTS), D_MODEL),
        "w_e_gate": nrm((DEPTH, N_EXPERTS, D_MODEL, D_EXPERT), D_MODEL),
        "w_e_up": nrm((DEPTH, N_EXPERTS, D_MODEL, D_EXPERT), D_MODEL),
        "w_e_down": nrm((DEPTH, N_EXPERTS, D_EXPERT, D_MODEL), D_EXPERT),
    }


def reference(x, c, ctx, c_ctx, w_ada, b_ada, g_norm1, g_norm2, w_in, conv_w, conv_b, rg_wa, rg_ba, rg_wx, rg_bx,
              rg_lambda, w_rnn_out, g_q_lora, w_uq, g_kv_lora, w_uk, w_uv, g_q_nope, g_q_rope, g_k_nope, g_k_rope,
              w_mla_out, w_o, w_router, w_e_gate, w_e_up, w_e_down):
    for l in range(DEPTH):
        p = dict(w_ada=w_ada[l], b_ada=b_ada[l], g_norm1=g_norm1[l], g_norm2=g_norm2[l], w_in=w_in[l],
                 conv_w=conv_w[l], conv_b=conv_b[l], rg_wa=rg_wa[l], rg_ba=rg_ba[l], rg_wx=rg_wx[l],
                 rg_bx=rg_bx[l], rg_lambda=rg_lambda[l], w_rnn_out=w_rnn_out[l], g_q_lora=g_q_lora[l],
                 w_uq=w_uq[l], g_kv_lora=g_kv_lora[l], w_uk=w_uk[l], w_uv=w_uv[l], g_q_nope=g_q_nope[l],
                 g_q_rope=g_q_rope[l], g_k_nope=g_k_nope[l], g_k_rope=g_k_rope[l], w_mla_out=w_mla_out[l],
                 w_o=w_o[l], w_router=w_router[l], w_e_gate=w_e_gate[l], w_e_up=w_e_up[l], w_e_down=w_e_down[l])
        x, ctx = layer(x, ctx, c, c_ctx, p, l < DEPTH - 1)
    return x
```

```python
import functools

import jax
import jax.numpy as jnp
from jax import lax
from jax.experimental import pallas as pl
from jax.experimental.pallas import tpu as pltpu

EPS = 1e-6
N_MOD = 6
GRID_W = 64
RG_BLOCKS = 8
CONV_W = 4
CONV_PAD_L = 2
RG_C = 8.0
MLA_HEADS = 8
QK_NOPE = 64
QK_ROPE = 32
V_HEAD = 64
Q_LORA = 256
KV_LORA = 128
ROPE_BASE = 10000.0
ATTN_SCALE = (QK_NOPE + QK_ROPE) ** -0.5
N_EXPERTS = 16
CAPACITY_FACTOR = 2

LANE = 128
SUBLANE = 8
HEAD_PAD = LANE
HALF_ROPE = QK_ROPE // 2
VMEM_LIMIT = 56 * 1024 * 1024

BF16 = jnp.bfloat16
F32 = jnp.float32


def _cparams(sem):
    return pltpu.CompilerParams(dimension_semantics=sem, vmem_limit_bytes=VMEM_LIMIT)


def _ada_kernel(c_ref, w_ref, b_ref, o_ref):
    cc = c_ref[...]
    s = cc * jax.nn.sigmoid(cc)
    o_ref[...] = jnp.dot(s, w_ref[...], preferred_element_type=F32) + b_ref[...]


def _ada(cc, w_ada, b_ada):
    rows, d = cc.shape
    n = w_ada.shape[1]
    bn = d
    return pl.pallas_call(
        _ada_kernel,
        grid=(n // bn,),
        in_specs=[pl.BlockSpec((rows, d), lambda j: (0, 0)),
                  pl.BlockSpec((d, bn), lambda j: (0, j)),
                  pl.BlockSpec((1, bn), lambda j: (0, j))],
        out_specs=pl.BlockSpec((rows, bn), lambda j: (0, j)),
        out_shape=jax.ShapeDtypeStruct((rows, n), F32),
        compiler_params=_cparams(("arbitrary",)),
        name="ada",
    )(cc, w_ada, b_ada.reshape(1, n))


def _inproj_kernel(x_ref, ctx_ref, mod_ref, g1_ref, win_ref, gq_ref, wuq_ref, wuqs_ref, gkv_ref, wuk_ref,
                   wuv_ref, gkn_ref, tqc_ref, tqs_ref, tkc_ref, tks_ref,
                   z0_ref, gz_ref, q_ref, k_ref, v_ref, sg_ref, *, d, n_lat_tiles):
    i = pl.program_id(1)
    is_ctx = i == n_lat_tiles
    xt = jnp.where(is_ctx, ctx_ref[0], x_ref[0])
    shift = mod_ref[0, :, 0:d]
    scale = mod_ref[0, :, d:2 * d]
    inv = lax.rsqrt(jnp.mean(xt * xt, axis=-1, keepdims=True) + EPS)
    xn = ((xt * inv * g1_ref[...]) * (1.0 + scale) + shift).astype(BF16)

    def seg(lo, hi):
        return jnp.dot(xn, win_ref[:, lo:hi], preferred_element_type=F32)

    o = 0
    z0_ref[0] = seg(o, o + d)
    o += d
    gz_ref[0] = jax.nn.gelu(seg(o, o + d))
    o += d
    zq = seg(o, o + Q_LORA)
    o += Q_LORA
    zkv = seg(o, o + KV_LORA)
    o += KV_LORA
    kr_a = seg(o, o + LANE)
    o += LANE
    kr_s = seg(o, o + LANE)
    o += LANE
    sg_ref[0] = jax.nn.sigmoid(seg(o, o + 2 * d))

    lane = lax.broadcasted_iota(jnp.int32, (1, LANE), 1)
    is_nope = lane < QK_NOPE
    is_rope = jnp.logical_and(lane >= QK_NOPE, lane < QK_NOPE + QK_ROPE)

    zqn = (zq * lax.rsqrt(jnp.mean(zq * zq, axis=-1, keepdims=True) + EPS) * gq_ref[...]).astype(BF16)
    qa = jnp.dot(zqn, wuq_ref[...], preferred_element_type=F32)
    qs = jnp.dot(zqn, wuqs_ref[...], preferred_element_type=F32)
    tqc = tqc_ref[...]
    tqs = tqs_ref[...]
    for h in range(MLA_HEADS):
        a = qa[:, h * HEAD_PAD:(h + 1) * HEAD_PAD]
        s = qs[:, h * HEAD_PAD:(h + 1) * HEAD_PAD]
        sq = a * a
        ssn = jnp.sum(jnp.where(is_nope, sq, 0.0), axis=-1, keepdims=True)
        ssr = jnp.sum(jnp.where(is_rope, sq, 0.0), axis=-1, keepdims=True)
        invh = jnp.where(is_nope, lax.rsqrt(ssn * (1.0 / QK_NOPE) + EPS), lax.rsqrt(ssr * (1.0 / QK_ROPE) + EPS))
        q_ref[0, :, h * HEAD_PAD:(h + 1) * HEAD_PAD] = (invh * (a * tqc + s * tqs)).astype(BF16)

    ckv = (zkv * lax.rsqrt(jnp.mean(zkv * zkv, axis=-1, keepdims=True) + EPS) * gkv_ref[...]).astype(BF16)
    v_ref[0] = jnp.dot(ckv, wuv_ref[...], preferred_element_type=F32).astype(BF16)
    ka = jnp.dot(ckv, wuk_ref[...], preferred_element_type=F32)
    inv_r = lax.rsqrt(jnp.sum(kr_a * kr_a, axis=-1, keepdims=True) * (1.0 / QK_ROPE) + EPS)
    kr = inv_r * (kr_a * tkc_ref[...] + kr_s * tks_ref[...])
    gkn = gkn_ref[...]
    for h in range(MLA_HEADS):
        a = ka[:, h * HEAD_PAD:(h + 1) * HEAD_PAD]
        invh = lax.rsqrt(jnp.sum(a * a, axis=-1, keepdims=True) * (1.0 / QK_NOPE) + EPS)
        k_ref[0, :, h * HEAD_PAD:(h + 1) * HEAD_PAD] = (a * invh * gkn + kr).astype(BF16)


def _inproj(x, ctx, mod, g1, win_p, gq, wuq_p, wuq_s, gkv, wuk_p, wuv, gkn, tqc, tqs, tkc, tks, tile):
    b, s, d = x.shape
    lc = ctx.shape[1]
    assert lc == tile and s % tile == 0
    nl = s // tile
    ltot = s + lc
    n_in = win_p.shape[1]
    hp = MLA_HEADS * HEAD_PAD
    const = lambda bb, i: (0, 0)
    row = lambda bb, i: (bb, i, 0)
    tab = lambda bb, i: (i, 0)
    out_shapes = (
        jax.ShapeDtypeStruct((b, ltot, d), F32),
        jax.ShapeDtypeStruct((b, ltot, d), F32),
        jax.ShapeDtypeStruct((b, ltot, hp), BF16),
        jax.ShapeDtypeStruct((b, ltot, hp), BF16),
        jax.ShapeDtypeStruct((b, ltot, MLA_HEADS * V_HEAD), BF16),
        jax.ShapeDtypeStruct((b, ltot, 2 * d), F32),
    )
    return pl.pallas_call(
        functools.partial(_inproj_kernel, d=d, n_lat_tiles=nl),
        grid=(b, nl + 1),
        in_specs=[
            pl.BlockSpec((1, tile, d), lambda bb, i: (bb, jnp.minimum(i, nl - 1), 0)),
            pl.BlockSpec((1, tile, d), lambda bb, i: (bb, 0, 0)),
            pl.BlockSpec((1, 1, N_MOD * d), lambda bb, i: (2 * bb + i // nl, 0, 0)),
            pl.BlockSpec((1, d), const),
            pl.BlockSpec((d, n_in), const),
            pl.BlockSpec((1, Q_LORA), const),
            pl.BlockSpec((Q_LORA, hp), const),
            pl.BlockSpec((Q_LORA, hp), const),
            pl.BlockSpec((1, KV_LORA), const),
            pl.BlockSpec((KV_LORA, hp), const),
            pl.BlockSpec((KV_LORA, MLA_HEADS * V_HEAD), const),
            pl.BlockSpec((1, LANE), const),
            pl.BlockSpec((tile, LANE), tab),
            pl.BlockSpec((tile, LANE), tab),
            pl.BlockSpec((tile, LANE), tab),
            pl.BlockSpec((tile, LANE), tab),
        ],
        out_specs=[
            pl.BlockSpec((1, tile, d), row),
            pl.BlockSpec((1, tile, d), row),
            pl.BlockSpec((1, tile, hp), row),
            pl.BlockSpec((1, tile, hp), row),
            pl.BlockSpec((1, tile, MLA_HEADS * V_HEAD), row),
            pl.BlockSpec((1, tile, 2 * d), row),
        ],
        out_shape=out_shapes,
        compiler_params=_cparams(("arbitrary", "arbitrary")),
        name="inproj",
    )(x, ctx, mod, g1, win_p, gq, wuq_p, wuq_s, gkv, wuk_p, wuv, gkn, tqc, tqs, tkc, tks)


def _softplus(x):
    e = jnp.exp(-jnp.abs(x))
    u = 1.0 + e
    tiny = u == 1.0
    log1p_e = jnp.where(tiny, e, jnp.log(u) * (e / jnp.where(tiny, 1.0, u - 1.0)))
    return jnp.maximum(x, 0.0) + log1p_e


def _rnn_kernel(*refs, d, tile, n_lat, reverse):
    if reverse:
        (z_ref, zp_ref, zn_ref, cw_ref, cb_ref, wa_ref, wx_ref, ba_ref, bx_ref, lam_ref, hf_ref, gz_ref,
         o_ref, ext_s, a_s, b_s, hb_s, h_s) = refs
    else:
        (z_ref, zp_ref, zn_ref, cw_ref, cb_ref, wa_ref, wx_ref, ba_ref, bx_ref, lam_ref,
         o_ref, ext_s, a_s, b_s, h_s) = refs
    j = pl.program_id(1)
    if reverse:
        c = jnp.where(j == 0, n_lat, n_lat - j)
    else:
        c = jnp.where(j == 0, n_lat, j - 1)
    prev_zero = jnp.logical_or(c == 0, c == n_lat)
    next_zero = c >= n_lat - 1

    ext_s[0:SUBLANE, :] = jnp.where(prev_zero, 0.0, zp_ref[0])
    ext_s[SUBLANE:SUBLANE + tile, :] = z_ref[0]
    ext_s[SUBLANE + tile:2 * SUBLANE + tile, :] = jnp.where(next_zero, 0.0, zn_ref[0])
    xr = cb_ref[...] + sum(
        ext_s[pl.ds(SUBLANE - CONV_PAD_L + k, tile), :] * cw_ref[k:k + 1, :] for k in range(CONV_W))

    xb = xr.astype(BF16)
    pair = 2 * (d // RG_BLOCKS)
    ya = jnp.concatenate([jnp.dot(xb[:, m * pair:(m + 1) * pair], wa_ref[m], preferred_element_type=F32)
                          for m in range(d // pair)], axis=-1)
    yx = jnp.concatenate([jnp.dot(xb[:, m * pair:(m + 1) * pair], wx_ref[m], preferred_element_type=F32)
                          for m in range(d // pair)], axis=-1)
    r = jax.nn.sigmoid(ya + ba_ref[...])
    gi = jax.nn.sigmoid(yx + bx_ref[...])
    log_a = (-RG_C * _softplus(-lam_ref[...])) * r
    a_s[...] = jnp.exp(log_a)
    th = jnp.tanh(log_a)
    b_s[...] = jnp.sqrt(-2.0 * th / (1.0 - th)) * (gi * xr)

    @pl.when(j == 0)
    def _():
        h_s[...] = jnp.zeros_like(h_s)

    dst = hb_s if reverse else o_ref.at[0]

    def step(t, h):
        tt = tile - 1 - t if reverse else t
        h = a_s[pl.ds(tt, 1), :] * h + b_s[pl.ds(tt, 1), :]
        dst[pl.ds(tt, 1), :] = h
        return h

    h_s[...] = lax.fori_loop(0, tile, step, h_s[...], unroll=8)
    if reverse:
        o_ref[0] = ((hf_ref[0] + hb_s[...]) * gz_ref[0]).astype(BF16)


def _rnn(z0, conv_w, conv_b, wa_p, wx_p, ba, bx, lam, tile, n_lat, reverse, hf=None, gz=None):
    b, ltot, d = z0.shape
    n_halo = ltot // SUBLANE
    per = tile // SUBLANE
    if reverse:
        cidx = lambda j: jnp.where(j == 0, n_lat, n_lat - j)
        oidx = lambda j: jnp.where(j == 0, n_lat - 1, n_lat - j)
    else:
        cidx = lambda j: jnp.where(j == 0, n_lat, j - 1)
        oidx = lambda j: jnp.maximum(j - 1, 0)
    const2 = lambda bb, j: (0, 0)
    const3 = lambda bb, j: (0, 0, 0)
    in_specs = [
        pl.BlockSpec((1, tile, d), lambda bb, j: (bb, cidx(j), 0)),
        pl.BlockSpec((1, SUBLANE, d), lambda bb, j: (bb, jnp.maximum(cidx(j) * per - 1, 0), 0)),
        pl.BlockSpec((1, SUBLANE, d), lambda bb, j: (bb, jnp.minimum((cidx(j) + 1) * per, n_halo - 1), 0)),
        pl.BlockSpec((CONV_W, d), const2),
        pl.BlockSpec((1, d), const2),
        pl.BlockSpec(wa_p.shape, const3),
        pl.BlockSpec(wx_p.shape, const3),
        pl.BlockSpec((1, d), const2),
        pl.BlockSpec((1, d), const2),
        pl.BlockSpec((1, d), const2),
    ]
    args = [z0, z0, z0, conv_w, conv_b, wa_p, wx_p, ba, bx, lam]
    scratch = [pltpu.VMEM((tile + 2 * SUBLANE, d), F32), pltpu.VMEM((tile, d), F32), pltpu.VMEM((tile, d), F32)]
    if reverse:
        in_specs += [pl.BlockSpec((1, tile, d), lambda bb, j: (bb, oidx(j), 0)),
                     pl.BlockSpec((1, tile, d), lambda bb, j: (bb, cidx(j), 0))]
        args += [hf, gz]
        scratch += [pltpu.VMEM((tile, d), F32)]
    scratch += [pltpu.VMEM((1, d), F32)]
    return pl.pallas_call(
        functools.partial(_rnn_kernel, d=d, tile=tile, n_lat=n_lat, reverse=reverse),
        grid=(b, n_lat + 1),
        in_specs=in_specs,
        out_specs=pl.BlockSpec((1, tile, d), lambda bb, j: (bb, oidx(j), 0)),
        out_shape=jax.ShapeDtypeStruct((b, n_lat * tile, d), BF16 if reverse else F32),
        scratch_shapes=scratch,
        compiler_params=_cparams(("arbitrary", "arbitrary")),
        name="rnn_bwd" if reverse else "rnn_fwd",
    )(*args)


def _attn_kernel(q_ref, k_ref, v_ref, o_ref):
    outs = []
    for hh in range(2):
        qh = q_ref[0, :, hh * HEAD_PAD:(hh + 1) * HEAD_PAD]
        kh = k_ref[0, :, hh * HEAD_PAD:(hh + 1) * HEAD_PAD]
        s = lax.dot_general(qh, kh, (((1,), (1,)), ((), ())), preferred_element_type=F32)
        m = jnp.max(s, axis=-1, keepdims=True)
        p = jnp.exp(s - m)
        l = jnp.sum(p, axis=-1, keepdims=True)
        o = jnp.dot(p.astype(BF16), v_ref[0], preferred_element_type=F32)
        outs.append(o / l)
    lane = lax.broadcasted_iota(jnp.int32, (1, 2 * V_HEAD), 1)
    o_ref[0] = jnp.where(lane < V_HEAD, outs[0], outs[1]).astype(BF16)


def _attention(q, k, v, s, tq):
    b, ltot, _ = q.shape
    pairs = MLA_HEADS // 2
    return pl.pallas_call(
        _attn_kernel,
        grid=(b, pairs, s // tq),
        in_specs=[pl.BlockSpec((1, tq, 2 * HEAD_PAD), lambda bb, p, i: (bb, i, p)),
                  pl.BlockSpec((1, ltot, 2 * HEAD_PAD), lambda bb, p, i: (bb, 0, p)),
                  pl.BlockSpec((1, ltot, 2 * V_HEAD), lambda bb, p, i: (bb, 0, p))],
        out_specs=pl.BlockSpec((1, tq, 2 * V_HEAD), lambda bb, p, i: (bb, i, p)),
        out_shape=jax.ShapeDtypeStruct((b, s, MLA_HEADS * V_HEAD), BF16),
        compiler_params=_cparams(("arbitrary", "arbitrary", "arbitrary")),
        name="attn",
    )(q, k, v)


def _merge_kernel(rnn_ref, attn_ref, sg_ref, x_ref, mod_ref, g2_ref, wr_ref, wm_ref, wo_ref, wrt_ref,
                  xn_ref, h2_ref, aff_ref, *, d):
    gate_mod = mod_ref[0, :, 2 * d:3 * d]
    shift = mod_ref[0, :, 3 * d:4 * d]
    scale = mod_ref[0, :, 4 * d:5 * d]
    pr = jnp.dot(rnn_ref[0], wr_ref[...], preferred_element_type=F32)
    pa = jnp.dot(attn_ref[0], wm_ref[...], preferred_element_type=F32)
    merged = sg_ref[0, :, 0:d] * pr + sg_ref[0, :, d:2 * d] * pa
    xn = x_ref[0] + gate_mod * jnp.dot(merged.astype(BF16), wo_ref[...], preferred_element_type=F32)
    xn_ref[0] = xn
    inv = lax.rsqrt(jnp.mean(xn * xn, axis=-1, keepdims=True) + EPS)
    h2 = (xn * inv * g2_ref[...]) * (1.0 + scale) + shift
    h2_ref[0] = h2
    logits = jnp.dot(h2.astype(BF16), wrt_ref[...], preferred_element_type=F32)[:, 0:N_EXPERTS]
    e = jnp.exp(logits - jnp.max(logits, axis=-1, keepdims=True))
    aff_ref[0] = e / jnp.sum(e, axis=-1, keepdims=True)


def _merge(rnn, attn, sg, x, mod, g2, wr, wm, wo, wrt, tile):
    b, s, d = x.shape
    const = lambda bb, i: (0, 0)
    row = lambda bb, i: (bb, i, 0)
    return pl.pallas_call(
        functools.partial(_merge_kernel, d=d),
        grid=(b, s // tile),
        in_specs=[pl.BlockSpec((1, tile, d), row),
                  pl.BlockSpec((1, tile, attn.shape[2]), row),
                  pl.BlockSpec((1, tile, 2 * d), row),
                  pl.BlockSpec((1, tile, d), row),
                  pl.BlockSpec((1, 1, N_MOD * d), lambda bb, i: (2 * bb, 0, 0)),
                  pl.BlockSpec((1, d), const),
                  pl.BlockSpec(wr.shape, const),
                  pl.BlockSpec(wm.shape, const),
                  pl.BlockSpec(wo.shape, const),
                  pl.BlockSpec(wrt.shape, const)],
        out_specs=[pl.BlockSpec((1, tile, d), row),
                   pl.BlockSpec((1, tile, d), row),
                   pl.BlockSpec((1, tile, N_EXPERTS), row)],
        out_shape=(jax.ShapeDtypeStruct((b, s, d), F32),
                   jax.ShapeDtypeStruct((b, s, d), F32),
                   jax.ShapeDtypeStruct((b, s, N_EXPERTS), F32)),
        compiler_params=_cparams(("arbitrary", "arbitrary")),
        name="merge",
    )(rnn, attn, sg, x, mod, g2, wr, wm, wo, wrt)


def _excl_cumsum_lanes(x01, tri):
    rows, length = x01.shape
    offset = jnp.zeros((rows, 1), F32)
    pieces = []
    for blk in range(length // LANE):
        xb = x01[:, blk * LANE:(blk + 1) * LANE]
        inc = jnp.dot(xb.astype(BF16), tri, preferred_element_type=F32)
        pieces.append(inc - xb + offset)
        offset = offset + jnp.sum(xb, axis=-1, keepdims=True)
    return jnp.concatenate(pieces, axis=-1)


def _topk_kernel(aff_ref, idx_ref, gate_ref, pos_s, sel_s, *, cap):
    a = aff_ref[0]
    n_e, length = a.shape
    keys = lax.bitcast_convert_type(a, jnp.int32)

    def bisect(i, cand):
        t = cand | (jnp.int32(1) << (30 - i))
        cnt = jnp.sum((keys >= t).astype(jnp.int32), axis=-1, keepdims=True)
        return jnp.where(cnt >= cap, t, cand)

    thr = lax.fori_loop(0, 31, bisect, jnp.zeros((n_e, 1), jnp.int32))
    gt = (keys > thr).astype(F32)
    eq = (keys == thr).astype(F32)
    need = cap - jnp.sum(gt, axis=-1, keepdims=True)
    r_i = lax.broadcasted_iota(jnp.int32, (LANE, LANE), 0)
    c_i = lax.broadcasted_iota(jnp.int32, (LANE, LANE), 1)
    tri = (r_i <= c_i).astype(BF16)
    tie_rank = _excl_cumsum_lanes(eq, tri)
    sel = gt + eq * (tie_rank < need).astype(F32)
    pos_s[...] = _excl_cumsum_lanes(sel, tri)
    sel_s[...] = sel

    tok = lax.broadcasted_iota(jnp.int32, (1, length), 1)
    tok_hi = (tok >> 6).astype(F32)
    tok_lo = (tok & 63).astype(F32)
    slot = lax.broadcasted_iota(jnp.int32, (cap, 1), 0).astype(F32)
    zeros = jnp.zeros((2 * SUBLANE - 5, length), F32)

    def compact(e, carry):
        pos_row = pos_s[pl.ds(e, 1), :]
        sel_row = sel_s[pl.ds(e, 1), :]
        onehot = jnp.where(jnp.logical_and(slot == pos_row, sel_row > 0.5), 1.0, 0.0).astype(BF16)
        g = aff_ref[0, pl.ds(e, 1), :]
        g1 = g.astype(BF16).astype(F32)
        g2 = (g - g1).astype(BF16).astype(F32)
        g3 = g - g1 - g2
        vals = jnp.concatenate([tok_hi, tok_lo, g1, g2, g3, zeros], axis=0).astype(BF16)
        res = lax.dot_general(vals, onehot, (((1,), (1,)), ((), ())), preferred_element_type=F32)
        idx_ref[0, pl.ds(e, 1), :] = (res[0:1] * 64.0 + res[1:2]).astype(jnp.int32)
        gate_ref[0, pl.ds(e, 1), :] = res[2:3] + res[3:4] + res[4:5]
        return carry

    lax.fori_loop(0, n_e, compact, 0)


def _topk(aff_t, cap):
    b, n_e, length = aff_t.shape
    blk = lambda bb: (bb, 0, 0)
    return pl.pallas_call(
        functools.partial(_topk_kernel, cap=cap),
        grid=(b,),
        in_specs=[pl.BlockSpec((1, n_e, length), blk)],
        out_specs=[pl.BlockSpec((1, n_e, cap), blk), pl.BlockSpec((1, n_e, cap), blk)],
        out_shape=(jax.ShapeDtypeStruct((b, n_e, cap), jnp.int32), jax.ShapeDtypeStruct((b, n_e, cap), F32)),
        scratch_shapes=[pltpu.VMEM((n_e, length), F32), pltpu.VMEM((n_e, length), F32)],
        compiler_params=_cparams(("arbitrary",)),
        name="topk",
    )(aff_t)


def _gather_kernel(idx_ref, h_ref, o_ref, *, cap):
    def body(j, carry):
        r = idx_ref[0, 0, j]
        o_ref[0, 0, pl.ds(j, 1), :] = h_ref[0, pl.ds(r, 1), :]
        return carry

    lax.fori_loop(0, cap, body, 0, unroll=8)


def _gather(idx, h2, cap):
    b, s, d = h2.shape
    return pl.pallas_call(
        functools.partial(_gather_kernel, cap=cap),
        grid=(b, N_EXPERTS),
        in_specs=[pl.BlockSpec((1, 1, cap), lambda bb, e: (bb * N_EXPERTS + e, 0, 0), memory_space=pltpu.SMEM),
                  pl.BlockSpec((1, s, d), lambda bb, e: (bb, 0, 0))],
        out_specs=pl.BlockSpec((1, 1, cap, d), lambda bb, e: (bb, e, 0, 0)),
        out_shape=jax.ShapeDtypeStruct((b, N_EXPERTS, cap, d), F32),
        compiler_params=_cparams(("arbitrary", "arbitrary")),
        name="gather",
    )(idx, h2)


def _expert_kernel(x_ref, wg_ref, wu_ref, wd_ref, mod_ref, y_ref, wg_s, wu_s, wd_s, *, d):
    @pl.when(pl.program_id(1) == 0)
    def _():
        wg_s[...] = wg_ref[0].astype(BF16)
        wu_s[...] = wu_ref[0].astype(BF16)
        wd_s[...] = wd_ref[0].astype(BF16)

    xb = x_ref[0, 0].astype(BF16)
    g = jnp.dot(xb, wg_s[...], preferred_element_type=F32)
    u = jnp.dot(xb, wu_s[...], preferred_element_type=F32)
    hid = (g * jax.nn.sigmoid(g) * u).astype(BF16)
    y = jnp.dot(hid, wd_s[...], preferred_element_type=F32)
    y_ref[0, 0] = y * mod_ref[0, :, 5 * d:6 * d]


def _experts(xg, w_gate, w_up, w_down, mod):
    b, n_e, cap, d = xg.shape
    f = w_gate.shape[2]
    return pl.pallas_call(
        functools.partial(_expert_kernel, d=d),
        grid=(n_e, b),
        in_specs=[pl.BlockSpec((1, 1, cap, d), lambda e, bb: (bb, e, 0, 0)),
                  pl.BlockSpec((1, d, f), lambda e, bb: (e, 0, 0)),
                  pl.BlockSpec((1, d, f), lambda e, bb: (e, 0, 0)),
                  pl.BlockSpec((1, f, d), lambda e, bb: (e, 0, 0)),
                  pl.BlockSpec((1, 1, N_MOD * d), lambda e, bb: (2 * bb, 0, 0))],
        out_specs=pl.BlockSpec((1, 1, cap, d), lambda e, bb: (bb, e, 0, 0)),
        out_shape=jax.ShapeDtypeStruct((b, n_e, cap, d), F32),
        scratch_shapes=[pltpu.VMEM((d, f), BF16), pltpu.VMEM((d, f), BF16), pltpu.VMEM((f, d), BF16)],
        compiler_params=_cparams(("arbitrary", "arbitrary")),
        name="experts",
    )(xg, w_gate, w_up, w_down, mod)


def _scatter_kernel(idx_ref, gate_ref, y_ref, x_ref, o_ref, *, cap):
    @pl.when(pl.program_id(2) == 0)
    def _():
        o_ref[...] = x_ref[...]

    def body(j, carry):
        r = idx_ref[0, 0, j]
        g = gate_ref[0, 0, j]
        o_ref[0, pl.ds(r, 1), :] = o_ref[0, pl.ds(r, 1), :] + g * y_ref[0, 0, pl.ds(j, 1), :]
        return carry

    lax.fori_loop(0, cap, body, 0, unroll=8)


def _scatter(idx, gate, y, x_new, cap, col):
    b, s, d = x_new.shape
    return pl.pallas_call(
        functools.partial(_scatter_kernel, cap=cap),
        grid=(b, d // col, N_EXPERTS),
        in_specs=[pl.BlockSpec((1, 1, cap), lambda bb, c, e: (bb * N_EXPERTS + e, 0, 0), memory_space=pltpu.SMEM),
                  pl.BlockSpec((1, 1, cap), lambda bb, c, e: (bb * N_EXPERTS + e, 0, 0), memory_space=pltpu.SMEM),
                  pl.BlockSpec((1, 1, cap, col), lambda bb, c, e: (bb, e, 0, c)),
                  pl.BlockSpec((1, s, col), lambda bb, c, e: (bb, 0, c))],
        out_specs=pl.BlockSpec((1, s, col), lambda bb, c, e: (bb, 0, c)),
        out_shape=jax.ShapeDtypeStruct((b, s, d), F32),
        compiler_params=_cparams(("arbitrary", "arbitrary", "arbitrary")),
        name="scatter",
    )(idx, gate, y, x_new)


def _pad_lanes(a, width):
    return jnp.pad(a, [(0, 0)] * (a.ndim - 1) + [(0, width - a.shape[-1])])


def _rope_tables(s, lc, g_q_nope, g_q_rope, g_k_nope, g_k_rope):
    rows = s // GRID_W
    row = jnp.repeat(jnp.arange(rows, dtype=F32), GRID_W)
    col = jnp.tile(jnp.arange(GRID_W, dtype=F32), rows)
    n_pairs = QK_ROPE // 4
    inv_freq = ROPE_BASE ** (-jnp.arange(n_pairs, dtype=F32) / n_pairs)
    ang = jnp.concatenate([row[:, None] * inv_freq, col[:, None] * inv_freq], axis=-1)
    cos = jnp.concatenate([jnp.cos(ang), jnp.ones((lc, HALF_ROPE), F32)], axis=0)
    sin = jnp.concatenate([jnp.sin(ang), jnp.zeros((lc, HALF_ROPE), F32)], axis=0)
    ltot = s + lc
    ones = jnp.ones((ltot, 1), F32)

    def tables(g_nope, g_rope, scale):
        g1, g2 = g_rope[:HALF_ROPE], g_rope[HALF_ROPE:]
        tc = jnp.concatenate([ones * g_nope[None, :], cos * g1, cos * g2], axis=-1) * scale
        ts = jnp.concatenate([jnp.zeros((ltot, QK_NOPE), F32), -sin * g2, sin * g1], axis=-1) * scale
        return _pad_lanes(tc, LANE), _pad_lanes(ts, LANE)

    tqc, tqs = tables(g_q_nope, g_q_rope, ATTN_SCALE)
    tkc, tks = tables(jnp.zeros((QK_NOPE,), F32), g_k_rope, 1.0)
    gkn = _pad_lanes(g_k_nope[None, :], LANE)
    return tqc, tqs, tkc, tks, gkn


def _swap_halves(a):
    return jnp.concatenate([a[..., HALF_ROPE:], a[..., :HALF_ROPE]], axis=-1)


def _layout_weights(w_in, w_uq, w_uk, d):
    o_q = 2 * d
    o_kv = o_q + Q_LORA
    o_kr = o_kv + KV_LORA
    o_g = o_kr + QK_ROPE
    kr = w_in[:, o_kr:o_g]
    zl = jnp.zeros((w_in.shape[0], QK_NOPE), F32)
    kr_a = _pad_lanes(jnp.concatenate([zl, kr], axis=-1), LANE)
    kr_s = _pad_lanes(jnp.concatenate([zl, _swap_halves(kr)], axis=-1), LANE)
    win_p = jnp.concatenate([w_in[:, :o_kr], kr_a, kr_s, w_in[:, o_g:]], axis=-1).astype(BF16)

    uq = w_uq.reshape(Q_LORA, MLA_HEADS, QK_NOPE + QK_ROPE)
    nope, rope = uq[..., :QK_NOPE], uq[..., QK_NOPE:]
    wuq_p = _pad_lanes(jnp.concatenate([nope, rope], axis=-1), HEAD_PAD).reshape(Q_LORA, -1).astype(BF16)
    wuq_s = _pad_lanes(jnp.concatenate([jnp.zeros_like(nope), _swap_halves(rope)], axis=-1),
                       HEAD_PAD).reshape(Q_LORA, -1).astype(BF16)
    wuk_p = _pad_lanes(w_uk.reshape(KV_LORA, MLA_HEADS, QK_NOPE), HEAD_PAD).reshape(KV_LORA, -1).astype(BF16)
    return win_p, wuq_p, wuq_s, wuk_p


def _pair_blocks(w):
    nb, n, _ = w.shape
    w = w.reshape(nb // 2, 2, n, n)
    z = jnp.zeros((nb // 2, n, n), w.dtype)
    top = jnp.concatenate([w[:, 0], z], axis=-1)
    bot = jnp.concatenate([z, w[:, 1]], axis=-1)
    return jnp.concatenate([top, bot], axis=-2).astype(BF16)


def kernel(x, c, ctx, c_ctx, w_ada, b_ada, g_norm1, g_norm2, w_in, conv_w, conv_b, rg_wa, rg_ba, rg_wx, rg_bx,
           rg_lambda, w_rnn_out, g_q_lora, w_uq, g_kv_lora, w_uk, w_uv, g_q_nope, g_q_rope, g_k_nope, g_k_rope,
           w_mla_out, w_o, w_router, w_e_gate, w_e_up, w_e_down):
    assert w_ada.shape[0] == 1, "single-layer kernel"
    b, s, d = x.shape
    lc = ctx.shape[1]
    tile = lc
    n_lat = s // tile
    cap = CAPACITY_FACTOR * s // N_EXPERTS

    pad_rows = (-(b + 1)) % SUBLANE
    cc = jnp.concatenate([c, c_ctx[None, :], jnp.zeros((pad_rows, d), F32)], axis=0)
    mod_all = _ada(cc, w_ada[0], b_ada[0])
    mod = jnp.stack([mod_all[:b], jnp.broadcast_to(mod_all[b], (b, N_MOD * d))], axis=1).reshape(2 * b, 1, N_MOD * d)

    win_p, wuq_p, wuq_s, wuk_p = _layout_weights(w_in[0], w_uq[0], w_uk[0], d)
    tqc, tqs, tkc, tks, gkn = _rope_tables(s, lc, g_q_nope[0], g_q_rope[0], g_k_nope[0], g_k_rope[0])
    z0, gz, q, k, v, sg = _inproj(x, ctx, mod, g_norm1, win_p, g_q_lora, wuq_p, wuq_s, g_kv_lora, wuk_p,
                                  w_uv[0].astype(BF16), gkn, tqc, tqs, tkc, tks, tile)

    h_f = _rnn(z0, conv_w[0], conv_b, _pair_blocks(rg_wa[0, 0]), _pair_blocks(rg_wx[0, 0]),
               rg_ba[0, 0:1], rg_bx[0, 0:1], rg_lambda[0, 0:1], tile, n_lat, False)
    rnn = _rnn(z0, conv_w[0], conv_b, _pair_blocks(rg_wa[0, 1]), _pair_blocks(rg_wx[0, 1]),
               rg_ba[0, 1:2], rg_bx[0, 1:2], rg_lambda[0, 1:2], tile, n_lat, True, hf=h_f, gz=gz)

    attn = _attention(q, k, v, s, 2 * tile)

    x_new, h2, aff = _merge(rnn, attn, sg, x, mod, g_norm2, w_rnn_out[0].astype(BF16), w_mla_out[0].astype(BF16),
                            w_o[0].astype(BF16), _pad_lanes(w_router[0], LANE).astype(BF16), 2 * tile)

    idx, gate = _topk(jnp.swapaxes(aff, 1, 2), cap)
    idx = idx.reshape(b * N_EXPERTS, 1, cap)
    gate = gate.reshape(b * N_EXPERTS, 1, cap)
    xg = _gather(idx, h2, cap)
    y = _experts(xg, w_e_gate[0], w_e_up[0], w_e_down[0], mod)
    return _scatter(idx, gate, y, x_new, cap, d // 2)
```

```python
import functools

import jax
import jax.numpy as jnp
from jax import lax
from jax.experimental import pallas as pl
from jax.experimental.pallas import tpu as pltpu

EPS = 1e-6
N_MOD = 6
GRID_W = 64
RG_BLOCKS = 8
CONV_W = 4
CONV_PAD_L = 2
RG_C = 8.0
MLA_HEADS = 8
QK_NOPE = 64
QK_ROPE = 32
V_HEAD = 64
Q_LORA = 256
KV_LORA = 128
ROPE_BASE = 10000.0
ATTN_SCALE = (QK_NOPE + QK_ROPE) ** -0.5
LOG2_E = 1.4426950408889634
N_EXPERTS = 16
CAPACITY_FACTOR = 2

LANE = 128
SUBLANE = 8
MXU_COLS = 256
HEAD_PAD = LANE
HALF_ROPE = QK_ROPE // 2
VMEM_LIMIT = 56 * 1024 * 1024

BF16 = jnp.bfloat16
F32 = jnp.float32


def _cparams(sem):
    return pltpu.CompilerParams(dimension_semantics=sem, vmem_limit_bytes=VMEM_LIMIT)


def _ada_kernel(c_ref, w_ref, b_ref, o_ref):
    cc = c_ref[...]
    s = cc * jax.nn.sigmoid(cc)
    o_ref[...] = jnp.dot(s, w_ref[...], preferred_element_type=F32) + b_ref[...]


def _ada(cc, w_ada, b_ada):
    rows, d = cc.shape
    n = w_ada.shape[1]
    bn = d
    return pl.pallas_call(
        _ada_kernel,
        grid=(n // bn,),
        in_specs=[pl.BlockSpec((rows, d), lambda j: (0, 0)),
                  pl.BlockSpec((d, bn), lambda j: (0, j)),
                  pl.BlockSpec((1, bn), lambda j: (0, j))],
        out_specs=pl.BlockSpec((rows, bn), lambda j: (0, j)),
        out_shape=jax.ShapeDtypeStruct((rows, n), F32),
        compiler_params=_cparams(("arbitrary",)),
        name="ada",
    )(cc, w_ada, b_ada.reshape(1, n))


def _inproj_kernel(x_ref, ctx_ref, mod_ref, g1_ref, win_ref, gq_ref, wuq_ref, wuqs_ref, gkv_ref, wuk_ref,
                   wuv_ref, gkn_ref, tqc_ref, tqs_ref, tkc_ref, tks_ref,
                   z0_ref, gz_ref, q_ref, k_ref, v_ref, sg_ref, *, d, n_lat_tiles):
    i = pl.program_id(1)
    is_ctx = i == n_lat_tiles
    xt = jnp.where(is_ctx, ctx_ref[0], x_ref[0])
    shift = mod_ref[0, :, 0:d]
    scale = mod_ref[0, :, d:2 * d]
    inv = lax.rsqrt(jnp.mean(xt * xt, axis=-1, keepdims=True) + EPS)
    xn = ((xt * inv * g1_ref[...]) * (1.0 + scale) + shift).astype(BF16)

    def seg(lo, hi):
        return jnp.dot(xn, win_ref[:, lo:hi], preferred_element_type=F32)

    o = 0
    z0_ref[0] = seg(o, o + d)
    o += d
    gz_ref[0] = jax.nn.gelu(seg(o, o + d))
    o += d
    zq = seg(o, o + Q_LORA)
    o += Q_LORA
    zkv = seg(o, o + KV_LORA)
    o += KV_LORA
    kr_a = seg(o, o + LANE)
    o += LANE
    kr_s = seg(o, o + LANE)
    o += LANE
    sg_ref[0] = jax.nn.sigmoid(seg(o, o + 2 * d))

    lane = lax.broadcasted_iota(jnp.int32, (1, LANE), 1)
    is_nope = lane < QK_NOPE
    is_rope = jnp.logical_and(lane >= QK_NOPE, lane < QK_NOPE + QK_ROPE)

    zqn = (zq * lax.rsqrt(jnp.mean(zq * zq, axis=-1, keepdims=True) + EPS) * gq_ref[...]).astype(BF16)
    qa = jnp.dot(zqn, wuq_ref[...], preferred_element_type=F32)
    qs = jnp.dot(zqn, wuqs_ref[...], preferred_element_type=F32)
    tqc = tqc_ref[...]
    tqs = tqs_ref[...]
    for h in range(MLA_HEADS):
        a = qa[:, h * HEAD_PAD:(h + 1) * HEAD_PAD]
        s = qs[:, h * HEAD_PAD:(h + 1) * HEAD_PAD]
        sq = a * a
        ssn = jnp.sum(jnp.where(is_nope, sq, 0.0), axis=-1, keepdims=True)
        ssr = jnp.sum(jnp.where(is_rope, sq, 0.0), axis=-1, keepdims=True)
        invh = jnp.where(is_nope, lax.rsqrt(ssn * (1.0 / QK_NOPE) + EPS), lax.rsqrt(ssr * (1.0 / QK_ROPE) + EPS))
        q_ref[0, :, h * HEAD_PAD:(h + 1) * HEAD_PAD] = (invh * (a * tqc + s * tqs)).astype(BF16)

    ckv = (zkv * lax.rsqrt(jnp.mean(zkv * zkv, axis=-1, keepdims=True) + EPS) * gkv_ref[...]).astype(BF16)
    lane_h = lax.broadcasted_iota(jnp.int32, (1, MLA_HEADS * HEAD_PAD), 1) % HEAD_PAD
    vp = jnp.dot(ckv, wuv_ref[...], preferred_element_type=F32) + jnp.where(lane_h == V_HEAD, 1.0, 0.0)
    v_ref[0] = vp.T.astype(BF16)
    ka = jnp.dot(ckv, wuk_ref[...], preferred_element_type=F32)
    inv_r = lax.rsqrt(jnp.sum(kr_a * kr_a, axis=-1, keepdims=True) * (1.0 / QK_ROPE) + EPS)
    kr = inv_r * (kr_a * tkc_ref[...] + kr_s * tks_ref[...])
    gkn = gkn_ref[...]
    for h in range(MLA_HEADS):
        a = ka[:, h * HEAD_PAD:(h + 1) * HEAD_PAD]
        invh = lax.rsqrt(jnp.sum(a * a, axis=-1, keepdims=True) * (1.0 / QK_NOPE) + EPS)
        k_ref[0, :, h * HEAD_PAD:(h + 1) * HEAD_PAD] = (a * invh * gkn + kr).astype(BF16)


def _inproj(x, ctx, mod, g1, win_p, gq, wuq_p, wuq_s, gkv, wuk_p, wuv, gkn, tqc, tqs, tkc, tks, tile):
    b, s, d = x.shape
    lc = ctx.shape[1]
    assert lc == tile and s % tile == 0
    nl = s // tile
    ltot = s + lc
    n_in = win_p.shape[1]
    hp = MLA_HEADS * HEAD_PAD
    const = lambda bb, i: (0, 0)
    row = lambda bb, i: (bb, i, 0)
    tab = lambda bb, i: (i, 0)
    out_shapes = (
        jax.ShapeDtypeStruct((b, ltot, d), F32),
        jax.ShapeDtypeStruct((b, ltot, d), F32),
        jax.ShapeDtypeStruct((b, ltot, hp), BF16),
        jax.ShapeDtypeStruct((b, ltot, hp), BF16),
        jax.ShapeDtypeStruct((b, hp, ltot), BF16),
        jax.ShapeDtypeStruct((b, ltot, 2 * d), F32),
    )
    return pl.pallas_call(
        functools.partial(_inproj_kernel, d=d, n_lat_tiles=nl),
        grid=(b, nl + 1),
        in_specs=[
            pl.BlockSpec((1, tile, d), lambda bb, i: (bb, jnp.minimum(i, nl - 1), 0)),
            pl.BlockSpec((1, tile, d), lambda bb, i: (bb, 0, 0)),
            pl.BlockSpec((1, 1, N_MOD * d), lambda bb, i: (2 * bb + i // nl, 0, 0)),
            pl.BlockSpec((1, d), const),
            pl.BlockSpec((d, n_in), const),
            pl.BlockSpec((1, Q_LORA), const),
            pl.BlockSpec((Q_LORA, hp), const),
            pl.BlockSpec((Q_LORA, hp), const),
            pl.BlockSpec((1, KV_LORA), const),
            pl.BlockSpec((KV_LORA, hp), const),
            pl.BlockSpec((KV_LORA, hp), const),
            pl.BlockSpec((1, LANE), const),
            pl.BlockSpec((tile, LANE), tab),
            pl.BlockSpec((tile, LANE), tab),
            pl.BlockSpec((tile, LANE), tab),
            pl.BlockSpec((tile, LANE), tab),
        ],
        out_specs=[
            pl.BlockSpec((1, tile, d), row),
            pl.BlockSpec((1, tile, d), row),
            pl.BlockSpec((1, tile, hp), row),
            pl.BlockSpec((1, tile, hp), row),
            pl.BlockSpec((1, hp, tile), lambda bb, i: (bb, 0, i)),
            pl.BlockSpec((1, tile, 2 * d), row),
        ],
        out_shape=out_shapes,
        compiler_params=_cparams(("arbitrary", "arbitrary")),
        name="inproj",
    )(x, ctx, mod, g1, win_p, gq, wuq_p, wuq_s, gkv, wuk_p, wuv, gkn, tqc, tqs, tkc, tks)


def _softplus(x):
    e = jnp.exp(-jnp.abs(x))
    u = 1.0 + e
    tiny = u == 1.0
    log1p_e = jnp.where(tiny, e, jnp.log(u) * (e / jnp.where(tiny, 1.0, u - 1.0)))
    return jnp.maximum(x, 0.0) + log1p_e


def _rnn_kernel(*refs, d, tile, n_lat, reverse):
    if reverse:
        (z_ref, zp_ref, zn_ref, cw_ref, cb_ref, wa_ref, wx_ref, ba_ref, bx_ref, lam_ref, hf_ref, gz_ref,
         o_ref, ext_s, a_s, b_s, hb_s, h_s) = refs
    else:
        (z_ref, zp_ref, zn_ref, cw_ref, cb_ref, wa_ref, wx_ref, ba_ref, bx_ref, lam_ref,
         o_ref, ext_s, a_s, b_s, h_s) = refs
    j = pl.program_id(1)
    if reverse:
        c = jnp.where(j == 0, n_lat, n_lat - j)
    else:
        c = jnp.where(j == 0, n_lat, j - 1)
    prev_zero = jnp.logical_or(c == 0, c == n_lat)
    next_zero = c >= n_lat - 1

    ext_s[0:SUBLANE, :] = jnp.where(prev_zero, 0.0, zp_ref[0])
    ext_s[SUBLANE:SUBLANE + tile, :] = z_ref[0]
    ext_s[SUBLANE + tile:2 * SUBLANE + tile, :] = jnp.where(next_zero, 0.0, zn_ref[0])
    xr = cb_ref[...] + sum(
        ext_s[pl.ds(SUBLANE - CONV_PAD_L + k, tile), :] * cw_ref[k:k + 1, :] for k in range(CONV_W))

    xb = xr.astype(BF16)
    pair = 2 * (d // RG_BLOCKS)
    ya = jnp.concatenate([jnp.dot(xb[:, m * pair:(m + 1) * pair], wa_ref[m], preferred_element_type=F32)
                          for m in range(d // pair)], axis=-1)
    yx = jnp.concatenate([jnp.dot(xb[:, m * pair:(m + 1) * pair], wx_ref[m], preferred_element_type=F32)
                          for m in range(d // pair)], axis=-1)
    r = jax.nn.sigmoid(ya + ba_ref[...])
    gi = jax.nn.sigmoid(yx + bx_ref[...])
    log_a = (-RG_C * _softplus(-lam_ref[...])) * r
    a_s[...] = jnp.exp(log_a)
    th = jnp.tanh(log_a)
    b_s[...] = jnp.sqrt(-2.0 * th / (1.0 - th)) * (gi * xr)

    @pl.when(j == 0)
    def _():
        h_s[...] = jnp.zeros_like(h_s)

    dst = hb_s if reverse else o_ref.at[0]

    def step(t, h):
        tt = tile - 1 - t if reverse else t
        h = a_s[pl.ds(tt, 1), :] * h + b_s[pl.ds(tt, 1), :]
        dst[pl.ds(tt, 1), :] = h
        return h

    h_s[...] = lax.fori_loop(0, tile, step, h_s[...], unroll=8)
    if reverse:
        o_ref[0] = ((hf_ref[0] + hb_s[...]) * gz_ref[0]).astype(BF16)


def _rnn(z0, conv_w, conv_b, wa_p, wx_p, ba, bx, lam, tile, n_lat, reverse, hf=None, gz=None):
    b, ltot, d = z0.shape
    n_halo = ltot // SUBLANE
    per = tile // SUBLANE
    if reverse:
        cidx = lambda j: jnp.where(j == 0, n_lat, n_lat - j)
        oidx = lambda j: jnp.where(j == 0, n_lat - 1, n_lat - j)
    else:
        cidx = lambda j: jnp.where(j == 0, n_lat, j - 1)
        oidx = lambda j: jnp.maximum(j - 1, 0)
    const2 = lambda bb, j: (0, 0)
    const3 = lambda bb, j: (0, 0, 0)
    in_specs = [
        pl.BlockSpec((1, tile, d), lambda bb, j: (bb, cidx(j), 0)),
        pl.BlockSpec((1, SUBLANE, d), lambda bb, j: (bb, jnp.maximum(cidx(j) * per - 1, 0), 0)),
        pl.BlockSpec((1, SUBLANE, d), lambda bb, j: (bb, jnp.minimum((cidx(j) + 1) * per, n_halo - 1), 0)),
        pl.BlockSpec((CONV_W, d), const2),
        pl.BlockSpec((1, d), const2),
        pl.BlockSpec(wa_p.shape, const3),
        pl.BlockSpec(wx_p.shape, const3),
        pl.BlockSpec((1, d), const2),
        pl.BlockSpec((1, d), const2),
        pl.BlockSpec((1, d), const2),
    ]
    args = [z0, z0, z0, conv_w, conv_b, wa_p, wx_p, ba, bx, lam]
    scratch = [pltpu.VMEM((tile + 2 * SUBLANE, d), F32), pltpu.VMEM((tile, d), F32), pltpu.VMEM((tile, d), F32)]
    if reverse:
        in_specs += [pl.BlockSpec((1, tile, d), lambda bb, j: (bb, oidx(j), 0)),
                     pl.BlockSpec((1, tile, d), lambda bb, j: (bb, cidx(j), 0))]
        args += [hf, gz]
        scratch += [pltpu.VMEM((tile, d), F32)]
    scratch += [pltpu.VMEM((1, d), F32)]
    return pl.pallas_call(
        functools.partial(_rnn_kernel, d=d, tile=tile, n_lat=n_lat, reverse=reverse),
        grid=(b, n_lat + 1),
        in_specs=in_specs,
        out_specs=pl.BlockSpec((1, tile, d), lambda bb, j: (bb, oidx(j), 0)),
        out_shape=jax.ShapeDtypeStruct((b, n_lat * tile, d), BF16 if reverse else F32),
        scratch_shapes=scratch,
        compiler_params=_cparams(("arbitrary", "arbitrary")),
        name="rnn_bwd" if reverse else "rnn_fwd",
    )(*args)


def _attn_kernel(q_ref, k_ref, vt_ref, o_ref, s_even, s_odd):
    g = pl.program_id(0)

    @pl.when(g == 0)
    def _():
        s_odd[...] = jnp.zeros_like(s_odd)

    def step(s_new, s_prev):
        for hh in range(2):
            qh = q_ref[0, :, hh * HEAD_PAD:(hh + 1) * HEAD_PAD]
            kh = k_ref[0, :, hh * HEAD_PAD:(hh + 1) * HEAD_PAD]
            s_new[hh] = lax.dot_general(kh, qh, (((1,), (1,)), ((), ())), preferred_element_type=F32)
        outs = []
        for hh in range(2):
            st = s_prev[hh]
            pt = jnp.exp2(st - jnp.max(st, axis=0, keepdims=True)).astype(BF16)
            ot = jnp.dot(vt_ref[0, hh * HEAD_PAD:(hh + 1) * HEAD_PAD, :], pt, preferred_element_type=F32)
            outs.append(ot[0:V_HEAD] / ot[V_HEAD:V_HEAD + 1])
        o_ref[0] = jnp.concatenate(outs, axis=0).T.astype(BF16)

    @pl.when(g % 2 == 0)
    def _():
        step(s_even, s_odd)

    @pl.when(g % 2 == 1)
    def _():
        step(s_odd, s_even)


def _attention(q, k, vt, s, tq):
    b, ltot, _ = q.shape
    pairs = MLA_HEADS // 2
    nq = s // tq
    n_items = b * pairs * nq

    def item(g):
        return g // (pairs * nq), (g // nq) % pairs, g % nq

    def cur(g):
        return item(jnp.minimum(g, n_items - 1))

    def prev(g):
        return item(jnp.maximum(g - 1, 0))

    return pl.pallas_call(
        _attn_kernel,
        grid=(n_items + 1,),
        in_specs=[pl.BlockSpec((1, tq, 2 * HEAD_PAD), lambda g: (cur(g)[0], cur(g)[2], cur(g)[1])),
                  pl.BlockSpec((1, ltot, 2 * HEAD_PAD), lambda g: (cur(g)[0], 0, cur(g)[1])),
                  pl.BlockSpec((1, 2 * HEAD_PAD, ltot), lambda g: (prev(g)[0], prev(g)[1], 0))],
        out_specs=pl.BlockSpec((1, tq, 2 * V_HEAD), lambda g: (prev(g)[0], prev(g)[2], prev(g)[1])),
        out_shape=jax.ShapeDtypeStruct((b, s, MLA_HEADS * V_HEAD), BF16),
        scratch_shapes=[pltpu.VMEM((2, ltot, tq), F32), pltpu.VMEM((2, ltot, tq), F32)],
        compiler_params=_cparams(("arbitrary",)),
        name="attn",
    )(q, k, vt)


def _merge_kernel(rnn_ref, attn_ref, sg_ref, x_ref, mod_ref, g2_ref, wr_ref, wm_ref, wo_ref, wrt_ref,
                  xn_ref, h2_ref, aff_ref, *, d):
    gate_mod = mod_ref[0, :, 2 * d:3 * d]
    shift = mod_ref[0, :, 3 * d:4 * d]
    scale = mod_ref[0, :, 4 * d:5 * d]
    pr = jnp.dot(rnn_ref[0], wr_ref[...], preferred_element_type=F32)
    pa = jnp.dot(attn_ref[0], wm_ref[...], preferred_element_type=F32)
    merged = sg_ref[0, :, 0:d] * pr + sg_ref[0, :, d:2 * d] * pa
    xn = x_ref[0] + gate_mod * jnp.dot(merged.astype(BF16), wo_ref[...], preferred_element_type=F32)
    xn_ref[0] = xn
    inv = lax.rsqrt(jnp.mean(xn * xn, axis=-1, keepdims=True) + EPS)
    h2 = (xn * inv * g2_ref[...]) * (1.0 + scale) + shift
    h2_ref[0] = h2
    logits = jnp.dot(h2.astype(BF16), wrt_ref[...], preferred_element_type=F32)[:, 0:N_EXPERTS]
    e = jnp.exp(logits - jnp.max(logits, axis=-1, keepdims=True))
    aff_ref[0] = e / jnp.sum(e, axis=-1, keepdims=True)


def _merge(rnn, attn, sg, x, mod, g2, wr, wm, wo, wrt, tile):
    b, s, d = x.shape
    const = lambda bb, i: (0, 0)
    row = lambda bb, i: (bb, i, 0)
    return pl.pallas_call(
        functools.partial(_merge_kernel, d=d),
        grid=(b, s // tile),
        in_specs=[pl.BlockSpec((1, tile, d), row),
                  pl.BlockSpec((1, tile, attn.shape[2]), row),
                  pl.BlockSpec((1, tile, 2 * d), row),
                  pl.BlockSpec((1, tile, d), row),
                  pl.BlockSpec((1, 1, N_MOD * d), lambda bb, i: (2 * bb, 0, 0)),
                  pl.BlockSpec((1, d), const),
                  pl.BlockSpec(wr.shape, const),
                  pl.BlockSpec(wm.shape, const),
                  pl.BlockSpec(wo.shape, const),
                  pl.BlockSpec(wrt.shape, const)],
        out_specs=[pl.BlockSpec((1, tile, d), row),
                   pl.BlockSpec((1, tile, d), row),
                   pl.BlockSpec((1, tile, N_EXPERTS), row)],
        out_shape=(jax.ShapeDtypeStruct((b, s, d), F32),
                   jax.ShapeDtypeStruct((b, s, d), F32),
                   jax.ShapeDtypeStruct((b, s, N_EXPERTS), F32)),
        compiler_params=_cparams(("arbitrary", "arbitrary")),
        name="merge",
    )(rnn, attn, sg, x, mod, g2, wr, wm, wo, wrt)


def _excl_cumsum_lanes(x01, tri):
    rows, length = x01.shape
    offset = jnp.zeros((rows, 1), F32)
    pieces = []
    for blk in range(length // LANE):
        xb = x01[:, blk * LANE:(blk + 1) * LANE]
        inc = jnp.dot(xb.astype(BF16), tri, preferred_element_type=F32)
        pieces.append(inc - xb + offset)
        offset = offset + jnp.sum(xb, axis=-1, keepdims=True)
    return jnp.concatenate(pieces, axis=-1)


def _topk_kernel(aff_ref, idx_ref, gate_ref, pos_s, sel_s, *, cap):
    a = aff_ref[0]
    n_e, length = a.shape
    keys = lax.bitcast_convert_type(a, jnp.int32)

    def bisect(i, cand):
        t = cand | (jnp.int32(1) << (30 - i))
        cnt = jnp.sum((keys >= t).astype(jnp.int32), axis=-1, keepdims=True)
        return jnp.where(cnt >= cap, t, cand)

    thr = lax.fori_loop(0, 31, bisect, jnp.zeros((n_e, 1), jnp.int32))
    gt = (keys > thr).astype(F32)
    eq = (keys == thr).astype(F32)
    need = cap - jnp.sum(gt, axis=-1, keepdims=True)
    r_i = lax.broadcasted_iota(jnp.int32, (LANE, LANE), 0)
    c_i = lax.broadcasted_iota(jnp.int32, (LANE, LANE), 1)
    tri = (r_i <= c_i).astype(BF16)
    tie_rank = _excl_cumsum_lanes(eq, tri)
    sel = gt + eq * (tie_rank < need).astype(F32)
    pos_s[...] = _excl_cumsum_lanes(sel, tri)
    sel_s[...] = sel

    tok = lax.broadcasted_iota(jnp.int32, (1, length), 1)
    tok_hi = (tok >> 6).astype(F32)
    tok_lo = (tok & 63).astype(F32)
    slot = lax.broadcasted_iota(jnp.int32, (cap, 1), 0).astype(F32)
    zeros = jnp.zeros((2 * SUBLANE - 5, length), F32)

    def compact(e, carry):
        pos_row = pos_s[pl.ds(e, 1), :]
        sel_row = sel_s[pl.ds(e, 1), :]
        onehot = jnp.where(jnp.logical_and(slot == pos_row, sel_row > 0.5), 1.0, 0.0).astype(BF16)
        g = aff_ref[0, pl.ds(e, 1), :]
        g1 = g.astype(BF16).astype(F32)
        g2 = (g - g1).astype(BF16).astype(F32)
        g3 = g - g1 - g2
        vals = jnp.concatenate([tok_hi, tok_lo, g1, g2, g3, zeros], axis=0).astype(BF16)
        res = lax.dot_general(vals, onehot, (((1,), (1,)), ((), ())), preferred_element_type=F32)
        idx_ref[0, pl.ds(e, 1), :] = (res[0:1] * 64.0 + res[1:2]).astype(jnp.int32)
        gate_ref[0, pl.ds(e, 1), :] = res[2:3] + res[3:4] + res[4:5]
        return carry

    lax.fori_loop(0, n_e, compact, 0)


def _topk(aff_t, cap):
    b, n_e, length = aff_t.shape
    blk = lambda bb: (bb, 0, 0)
    return pl.pallas_call(
        functools.partial(_topk_kernel, cap=cap),
        grid=(b,),
        in_specs=[pl.BlockSpec((1, n_e, length), blk)],
        out_specs=[pl.BlockSpec((1, n_e, cap), blk), pl.BlockSpec((1, n_e, cap), blk)],
        out_shape=(jax.ShapeDtypeStruct((b, n_e, cap), jnp.int32), jax.ShapeDtypeStruct((b, n_e, cap), F32)),
        scratch_shapes=[pltpu.VMEM((n_e, length), F32), pltpu.VMEM((n_e, length), F32)],
        compiler_params=_cparams(("arbitrary",)),
        name="topk",
    )(aff_t)


def _gather_kernel(idx_ref, h_ref, o_ref, *, cap):
    def body(j, carry):
        r = idx_ref[0, 0, j]
        o_ref[0, 0, pl.ds(j, 1), :] = h_ref[0, pl.ds(r, 1), :]
        return carry

    lax.fori_loop(0, cap, body, 0, unroll=8)


def _gather(idx, h2, cap):
    b, s, d = h2.shape
    return pl.pallas_call(
        functools.partial(_gather_kernel, cap=cap),
        grid=(b, N_EXPERTS),
        in_specs=[pl.BlockSpec((1, 1, cap), lambda bb, e: (bb * N_EXPERTS + e, 0, 0), memory_space=pltpu.SMEM),
                  pl.BlockSpec((1, s, d), lambda bb, e: (bb, 0, 0))],
        out_specs=pl.BlockSpec((1, 1, cap, d), lambda bb, e: (bb, e, 0, 0)),
        out_shape=jax.ShapeDtypeStruct((b, N_EXPERTS, cap, d), F32),
        compiler_params=_cparams(("arbitrary", "arbitrary")),
        name="gather",
    )(idx, h2)


def _expert_kernel(x_ref, wg_ref, wu_ref, wd_ref, mod_ref, y_ref, wg_s, wu_s, wd_s, *, d):
    @pl.when(pl.program_id(1) == 0)
    def _():
        wg_s[...] = wg_ref[0].astype(BF16)
        wu_s[...] = wu_ref[0].astype(BF16)
        wd_s[...] = wd_ref[0].astype(BF16)

    xb = x_ref[0, 0].astype(BF16)
    g = jnp.dot(xb, wg_s[...], preferred_element_type=F32)
    u = jnp.dot(xb, wu_s[...], preferred_element_type=F32)
    hid = (g * jax.nn.sigmoid(g) * u).astype(BF16)
    y = jnp.dot(hid, wd_s[...], preferred_element_type=F32)
    y_ref[0, 0] = y * mod_ref[0, :, 5 * d:6 * d]


def _experts(xg, w_gate, w_up, w_down, mod):
    b, n_e, cap, d = xg.shape
    f = w_gate.shape[2]
    return pl.pallas_call(
        functools.partial(_expert_kernel, d=d),
        grid=(n_e, b),
        in_specs=[pl.BlockSpec((1, 1, cap, d), lambda e, bb: (bb, e, 0, 0)),
                  pl.BlockSpec((1, d, f), lambda e, bb: (e, 0, 0)),
                  pl.BlockSpec((1, d, f), lambda e, bb: (e, 0, 0)),
                  pl.BlockSpec((1, f, d), lambda e, bb: (e, 0, 0)),
                  pl.BlockSpec((1, 1, N_MOD * d), lambda e, bb: (2 * bb, 0, 0))],
        out_specs=pl.BlockSpec((1, 1, cap, d), lambda e, bb: (bb, e, 0, 0)),
        out_shape=jax.ShapeDtypeStruct((b, n_e, cap, d), F32),
        scratch_shapes=[pltpu.VMEM((d, f), BF16), pltpu.VMEM((d, f), BF16), pltpu.VMEM((f, d), BF16)],
        compiler_params=_cparams(("arbitrary", "arbitrary")),
        name="experts",
    )(xg, w_gate, w_up, w_down, mod)


def _scatter_kernel(idx_ref, gate_ref, y_ref, x_ref, o_ref, *, cap):
    @pl.when(pl.program_id(2) == 0)
    def _():
        o_ref[...] = x_ref[...]

    def body(j, carry):
        r = idx_ref[0, 0, j]
        g = gate_ref[0, 0, j]
        o_ref[0, pl.ds(r, 1), :] = o_ref[0, pl.ds(r, 1), :] + g * y_ref[0, 0, pl.ds(j, 1), :]
        return carry

    lax.fori_loop(0, cap, body, 0, unroll=8)


def _scatter(idx, gate, y, x_new, cap, col):
    b, s, d = x_new.shape
    return pl.pallas_call(
        functools.partial(_scatter_kernel, cap=cap),
        grid=(b, d // col, N_EXPERTS),
        in_specs=[pl.BlockSpec((1, 1, cap), lambda bb, c, e: (bb * N_EXPERTS + e, 0, 0), memory_space=pltpu.SMEM),
                  pl.BlockSpec((1, 1, cap), lambda bb, c, e: (bb * N_EXPERTS + e, 0, 0), memory_space=pltpu.SMEM),
                  pl.BlockSpec((1, 1, cap, col), lambda bb, c, e: (bb, e, 0, c)),
                  pl.BlockSpec((1, s, col), lambda bb, c, e: (bb, 0, c))],
        out_specs=pl.BlockSpec((1, s, col), lambda bb, c, e: (bb, 0, c)),
        out_shape=jax.ShapeDtypeStruct((b, s, d), F32),
        compiler_params=_cparams(("arbitrary", "arbitrary", "arbitrary")),
        name="scatter",
    )(idx, gate, y, x_new)


def _pad_lanes(a, width):
    return jnp.pad(a, [(0, 0)] * (a.ndim - 1) + [(0, width - a.shape[-1])])


def _rope_tables(s, lc, g_q_nope, g_q_rope, g_k_nope, g_k_rope):
    rows = s // GRID_W
    row = jnp.repeat(jnp.arange(rows, dtype=F32), GRID_W)
    col = jnp.tile(jnp.arange(GRID_W, dtype=F32), rows)
    n_pairs = QK_ROPE // 4
    inv_freq = ROPE_BASE ** (-jnp.arange(n_pairs, dtype=F32) / n_pairs)
    ang = jnp.concatenate([row[:, None] * inv_freq, col[:, None] * inv_freq], axis=-1)
    cos = jnp.concatenate([jnp.cos(ang), jnp.ones((lc, HALF_ROPE), F32)], axis=0)
    sin = jnp.concatenate([jnp.sin(ang), jnp.zeros((lc, HALF_ROPE), F32)], axis=0)
    ltot = s + lc
    ones = jnp.ones((ltot, 1), F32)

    def tables(g_nope, g_rope, scale):
        g1, g2 = g_rope[:HALF_ROPE], g_rope[HALF_ROPE:]
        tc = jnp.concatenate([ones * g_nope[None, :], cos * g1, cos * g2], axis=-1) * scale
        ts = jnp.concatenate([jnp.zeros((ltot, QK_NOPE), F32), -sin * g2, sin * g1], axis=-1) * scale
        return _pad_lanes(tc, LANE), _pad_lanes(ts, LANE)

    tqc, tqs = tables(g_q_nope, g_q_rope, ATTN_SCALE * LOG2_E)
    tkc, tks = tables(jnp.zeros((QK_NOPE,), F32), g_k_rope, 1.0)
    gkn = _pad_lanes(g_k_nope[None, :], LANE)
    return tqc, tqs, tkc, tks, gkn


def _swap_halves(a):
    return jnp.concatenate([a[..., HALF_ROPE:], a[..., :HALF_ROPE]], axis=-1)


def _layout_weights(w_in, w_uq, w_uk, w_uv, d):
    o_q = 2 * d
    o_kv = o_q + Q_LORA
    o_kr = o_kv + KV_LORA
    o_g = o_kr + QK_ROPE
    kr = w_in[:, o_kr:o_g]
    zl = jnp.zeros((w_in.shape[0], QK_NOPE), F32)
    kr_a = _pad_lanes(jnp.concatenate([zl, kr], axis=-1), LANE)
    kr_s = _pad_lanes(jnp.concatenate([zl, _swap_halves(kr)], axis=-1), LANE)
    win_p = jnp.concatenate([w_in[:, :o_kr], kr_a, kr_s, w_in[:, o_g:]], axis=-1).astype(BF16)

    uq = w_uq.reshape(Q_LORA, MLA_HEADS, QK_NOPE + QK_ROPE)
    nope, rope = uq[..., :QK_NOPE], uq[..., QK_NOPE:]
    wuq_p = _pad_lanes(jnp.concatenate([nope, rope], axis=-1), HEAD_PAD).reshape(Q_LORA, -1).astype(BF16)
    wuq_s = _pad_lanes(jnp.concatenate([jnp.zeros_like(nope), _swap_halves(rope)], axis=-1),
                       HEAD_PAD).reshape(Q_LORA, -1).astype(BF16)
    wuk_p = _pad_lanes(w_uk.reshape(KV_LORA, MLA_HEADS, QK_NOPE), HEAD_PAD).reshape(KV_LORA, -1).astype(BF16)
    wuv_p = _pad_lanes(w_uv.reshape(KV_LORA, MLA_HEADS, V_HEAD), HEAD_PAD).reshape(KV_LORA, -1).astype(BF16)
    return win_p, wuq_p, wuq_s, wuk_p, wuv_p


def _pair_blocks(w):
    nb, n, _ = w.shape
    w = w.reshape(nb // 2, 2, n, n)
    z = jnp.zeros((nb // 2, n, n), w.dtype)
    top = jnp.concatenate([w[:, 0], z], axis=-1)
    bot = jnp.concatenate([z, w[:, 1]], axis=-1)
    return jnp.concatenate([top, bot], axis=-2).astype(BF16)


def kernel(x, c, ctx, c_ctx, w_ada, b_ada, g_norm1, g_norm2, w_in, conv_w, conv_b, rg_wa, rg_ba, rg_wx, rg_bx,
           rg_lambda, w_rnn_out, g_q_lora, w_uq, g_kv_lora, w_uk, w_uv, g_q_nope, g_q_rope, g_k_nope, g_k_rope,
           w_mla_out, w_o, w_router, w_e_gate, w_e_up, w_e_down):
    assert w_ada.shape[0] == 1, "single-layer kernel"
    b, s, d = x.shape
    lc = ctx.shape[1]
    tile = lc
    n_lat = s // tile
    cap = CAPACITY_FACTOR * s // N_EXPERTS

    pad_rows = (-(b + 1)) % SUBLANE
    cc = jnp.concatenate([c, c_ctx[None, :], jnp.zeros((pad_rows, d), F32)], axis=0)
    mod_all = _ada(cc, w_ada[0], b_ada[0])
    mod = jnp.stack([mod_all[:b], jnp.broadcast_to(mod_all[b], (b, N_MOD * d))], axis=1).reshape(2 * b, 1, N_MOD * d)

    win_p, wuq_p, wuq_s, wuk_p, wuv_p = _layout_weights(w_in[0], w_uq[0], w_uk[0], w_uv[0], d)
    tqc, tqs, tkc, tks, gkn = _rope_tables(s, lc, g_q_nope[0], g_q_rope[0], g_k_nope[0], g_k_rope[0])
    z0, gz, q, k, v, sg = _inproj(x, ctx, mod, g_norm1, win_p, g_q_lora, wuq_p, wuq_s, g_kv_lora, wuk_p,
                                  wuv_p, gkn, tqc, tqs, tkc, tks, tile)

    h_f = _rnn(z0, conv_w[0], conv_b, _pair_blocks(rg_wa[0, 0]), _pair_blocks(rg_wx[0, 0]),
               rg_ba[0, 0:1], rg_bx[0, 0:1], rg_lambda[0, 0:1], tile, n_lat, False)
    rnn = _rnn(z0, conv_w[0], conv_b, _pair_blocks(rg_wa[0, 1]), _pair_blocks(rg_wx[0, 1]),
               rg_ba[0, 1:2], rg_bx[0, 1:2], rg_lambda[0, 1:2], tile, n_lat, True, hf=h_f, gz=gz)

    attn = _attention(q, k, v, s, MXU_COLS)

    x_new, h2, aff = _merge(rnn, attn, sg, x, mod, g_norm2, w_rnn_out[0].astype(BF16), w_mla_out[0].astype(BF16),
                            w_o[0].astype(BF16), _pad_lanes(w_router[0], LANE).astype(BF16), 2 * tile)

    idx, gate = _topk(jnp.swapaxes(aff, 1, 2), cap)
    idx = idx.reshape(b * N_EXPERTS, 1, cap)
    gate = gate.reshape(b * N_EXPERTS, 1, cap)
    xg = _gather(idx, h2, cap)
    y = _experts(xg, w_e_gate[0], w_e_up[0], w_e_down[0], mod)
    return _scatter(idx, gate, y, x_new, cap, d // 2)
```

```python
import functools

import jax
import jax.numpy as jnp
from jax import lax
from jax.experimental import pallas as pl
from jax.experimental.pallas import tpu as pltpu

EPS = 1e-6
N_MOD = 6
GRID_W = 64
RG_BLOCKS = 8
CONV_W = 4
CONV_PAD_L = 2
RG_C = 8.0
MLA_HEADS = 8
QK_NOPE = 64
QK_ROPE = 32
V_HEAD = 64
Q_LORA = 256
KV_LORA = 128
ROPE_BASE = 10000.0
ATTN_SCALE = (QK_NOPE + QK_ROPE) ** -0.5
LOG2_E = 1.4426950408889634
N_EXPERTS = 16
CAPACITY_FACTOR = 2

LANE = 128
SUBLANE = 8
MXU_COLS = 256
HEAD_PAD = LANE
HALF_ROPE = QK_ROPE // 2
VMEM_LIMIT = 56 * 1024 * 1024

BF16 = jnp.bfloat16
F32 = jnp.float32


def _cparams(sem):
    return pltpu.CompilerParams(dimension_semantics=sem, vmem_limit_bytes=VMEM_LIMIT)


def _ada_kernel(c_ref, w_ref, b_ref, o_ref):
    cc = c_ref[...]
    s = cc * jax.nn.sigmoid(cc)
    o_ref[...] = jnp.dot(s, w_ref[...], preferred_element_type=F32) + b_ref[...]


def _ada(cc, w_ada, b_ada):
    rows, d = cc.shape
    n = w_ada.shape[1]
    bn = d
    return pl.pallas_call(
        _ada_kernel,
        grid=(n // bn,),
        in_specs=[pl.BlockSpec((rows, d), lambda j: (0, 0)),
                  pl.BlockSpec((d, bn), lambda j: (0, j)),
                  pl.BlockSpec((1, bn), lambda j: (0, j))],
        out_specs=pl.BlockSpec((rows, bn), lambda j: (0, j)),
        out_shape=jax.ShapeDtypeStruct((rows, n), F32),
        compiler_params=_cparams(("arbitrary",)),
        name="ada",
    )(cc, w_ada, b_ada.reshape(1, n))


def _inproj_kernel(x_ref, ctx_ref, mod_ref, g1_ref, win_ref, gq_ref, wuq_ref, wuqs_ref, gkv_ref, wuk_ref,
                   wuv_ref, gkn_ref, tqc_ref, tqs_ref, tkc_ref, tks_ref,
                   z0_ref, gz_ref, q_ref, k_ref, v_ref, sg_ref, *, d, n_lat_tiles):
    i = pl.program_id(1)
    is_ctx = i == n_lat_tiles
    xt = jnp.where(is_ctx, ctx_ref[0], x_ref[0])
    shift = mod_ref[0, :, 0:d]
    scale = mod_ref[0, :, d:2 * d]
    inv = lax.rsqrt(jnp.mean(xt * xt, axis=-1, keepdims=True) + EPS)
    xn = ((xt * inv * g1_ref[...]) * (1.0 + scale) + shift).astype(BF16)

    def seg(lo, hi):
        return jnp.dot(xn, win_ref[:, lo:hi], preferred_element_type=F32)

    o = 0
    z0_ref[0] = seg(o, o + d)
    o += d
    gz_ref[0] = jax.nn.gelu(seg(o, o + d))
    o += d
    zq = seg(o, o + Q_LORA)
    o += Q_LORA
    zkv = seg(o, o + KV_LORA)
    o += KV_LORA
    kr_a = seg(o, o + LANE)
    o += LANE
    kr_s = seg(o, o + LANE)
    o += LANE
    sg_ref[0] = jax.nn.sigmoid(seg(o, o + 2 * d))

    lane = lax.broadcasted_iota(jnp.int32, (1, LANE), 1)
    is_nope = lane < QK_NOPE
    is_rope = jnp.logical_and(lane >= QK_NOPE, lane < QK_NOPE + QK_ROPE)

    zqn = (zq * lax.rsqrt(jnp.mean(zq * zq, axis=-1, keepdims=True) + EPS) * gq_ref[...]).astype(BF16)
    qa = jnp.dot(zqn, wuq_ref[...], preferred_element_type=F32)
    qs = jnp.dot(zqn, wuqs_ref[...], preferred_element_type=F32)
    tqc = tqc_ref[...]
    tqs = tqs_ref[...]
    for h in range(MLA_HEADS):
        a = qa[:, h * HEAD_PAD:(h + 1) * HEAD_PAD]
        s = qs[:, h * HEAD_PAD:(h + 1) * HEAD_PAD]
        sq = a * a
        ssn = jnp.sum(jnp.where(is_nope, sq, 0.0), axis=-1, keepdims=True)
        ssr = jnp.sum(jnp.where(is_rope, sq, 0.0), axis=-1, keepdims=True)
        invh = jnp.where(is_nope, lax.rsqrt(ssn * (1.0 / QK_NOPE) + EPS), lax.rsqrt(ssr * (1.0 / QK_ROPE) + EPS))
        q_ref[0, :, h * HEAD_PAD:(h + 1) * HEAD_PAD] = (invh * (a * tqc + s * tqs)).astype(BF16)

    ckv = (zkv * lax.rsqrt(jnp.mean(zkv * zkv, axis=-1, keepdims=True) + EPS) * gkv_ref[...]).astype(BF16)
    lane_h = lax.broadcasted_iota(jnp.int32, (1, MLA_HEADS * HEAD_PAD), 1) % HEAD_PAD
    vp = jnp.dot(ckv, wuv_ref[...], preferred_element_type=F32) + jnp.where(lane_h == V_HEAD, 1.0, 0.0)
    v_ref[0] = vp.T.astype(BF16)
    ka = jnp.dot(ckv, wuk_ref[...], preferred_element_type=F32)
    inv_r = lax.rsqrt(jnp.sum(kr_a * kr_a, axis=-1, keepdims=True) * (1.0 / QK_ROPE) + EPS)
    kr = inv_r * (kr_a * tkc_ref[...] + kr_s * tks_ref[...])
    gkn = gkn_ref[...]
    for h in range(MLA_HEADS):
        a = ka[:, h * HEAD_PAD:(h + 1) * HEAD_PAD]
        invh = lax.rsqrt(jnp.sum(a * a, axis=-1, keepdims=True) * (1.0 / QK_NOPE) + EPS)
        k_ref[0, :, h * HEAD_PAD:(h + 1) * HEAD_PAD] = (a * invh * gkn + kr).astype(BF16)


def _inproj(x, ctx, mod, g1, win_p, gq, wuq_p, wuq_s, gkv, wuk_p, wuv, gkn, tqc, tqs, tkc, tks, tile):
    b, s, d = x.shape
    lc = ctx.shape[1]
    assert lc == tile and s % tile == 0
    nl = s // tile
    ltot = s + lc
    n_in = win_p.shape[1]
    hp = MLA_HEADS * HEAD_PAD
    const = lambda bb, i: (0, 0)
    row = lambda bb, i: (bb, i, 0)
    tab = lambda bb, i: (i, 0)
    out_shapes = (
        jax.ShapeDtypeStruct((b, ltot, d), F32),
        jax.ShapeDtypeStruct((b, ltot, d), F32),
        jax.ShapeDtypeStruct((b, ltot, hp), BF16),
        jax.ShapeDtypeStruct((b, ltot, hp), BF16),
        jax.ShapeDtypeStruct((b, hp, ltot), BF16),
        jax.ShapeDtypeStruct((b, ltot, 2 * d), F32),
    )
    return pl.pallas_call(
        functools.partial(_inproj_kernel, d=d, n_lat_tiles=nl),
        grid=(b, nl + 1),
        in_specs=[
            pl.BlockSpec((1, tile, d), lambda bb, i: (bb, jnp.minimum(i, nl - 1), 0)),
            pl.BlockSpec((1, tile, d), lambda bb, i: (bb, 0, 0)),
            pl.BlockSpec((1, 1, N_MOD * d), lambda bb, i: (2 * bb + i // nl, 0, 0)),
            pl.BlockSpec((1, d), const),
            pl.BlockSpec((d, n_in), const),
            pl.BlockSpec((1, Q_LORA), const),
            pl.BlockSpec((Q_LORA, hp), const),
            pl.BlockSpec((Q_LORA, hp), const),
            pl.BlockSpec((1, KV_LORA), const),
            pl.BlockSpec((KV_LORA, hp), const),
            pl.BlockSpec((KV_LORA, hp), const),
            pl.BlockSpec((1, LANE), const),
            pl.BlockSpec((tile, LANE), tab),
            pl.BlockSpec((tile, LANE), tab),
            pl.BlockSpec((tile, LANE), tab),
            pl.BlockSpec((tile, LANE), tab),
        ],
        out_specs=[
            pl.BlockSpec((1, tile, d), row),
            pl.BlockSpec((1, tile, d), row),
            pl.BlockSpec((1, tile, hp), row),
            pl.BlockSpec((1, tile, hp), row),
            pl.BlockSpec((1, hp, tile), lambda bb, i: (bb, 0, i)),
            pl.BlockSpec((1, tile, 2 * d), row),
        ],
        out_shape=out_shapes,
        compiler_params=_cparams(("arbitrary", "arbitrary")),
        name="inproj",
    )(x, ctx, mod, g1, win_p, gq, wuq_p, wuq_s, gkv, wuk_p, wuv, gkn, tqc, tqs, tkc, tks)


def _softplus(x):
    e = jnp.exp(-jnp.abs(x))
    u = 1.0 + e
    tiny = u == 1.0
    log1p_e = jnp.where(tiny, e, jnp.log(u) * (e / jnp.where(tiny, 1.0, u - 1.0)))
    return jnp.maximum(x, 0.0) + log1p_e


def _rnn_kernel(*refs, d, tile, n_lat, reverse):
    if reverse:
        (z_ref, zp_ref, zn_ref, cw_ref, cb_ref, wa_ref, wx_ref, ba_ref, bx_ref, lam_ref, hf_ref, gz_ref,
         o_ref, a_s, b_s, hb_s, h_s) = refs
    else:
        (z_ref, zp_ref, zn_ref, cw_ref, cb_ref, wa_ref, wx_ref, ba_ref, bx_ref, lam_ref,
         o_ref, a_s, b_s, h_s) = refs
    j = pl.program_id(1)
    if reverse:
        c = jnp.where(j == 0, n_lat, n_lat - j)
    else:
        c = jnp.where(j == 0, n_lat, j - 1)
    prev_zero = jnp.logical_or(c == 0, c == n_lat)
    next_zero = c >= n_lat - 1

    ext = jnp.concatenate([jnp.where(prev_zero, 0.0, zp_ref[0]), z_ref[0], jnp.where(next_zero, 0.0, zn_ref[0])],
                          axis=0)
    n_ext = tile + 2 * SUBLANE
    xr = cb_ref[...]
    for k in range(CONV_W):
        shift = (CONV_PAD_L - k) % n_ext
        tap = ext if shift == 0 else pltpu.roll(ext, shift, 0)
        xr = xr + tap[SUBLANE:SUBLANE + tile] * cw_ref[k:k + 1, :]

    xb = xr.astype(BF16)
    pair = 2 * (d // RG_BLOCKS)
    ya = jnp.concatenate([jnp.dot(xb[:, m * pair:(m + 1) * pair], wa_ref[m], preferred_element_type=F32)
                          for m in range(d // pair)], axis=-1)
    yx = jnp.concatenate([jnp.dot(xb[:, m * pair:(m + 1) * pair], wx_ref[m], preferred_element_type=F32)
                          for m in range(d // pair)], axis=-1)
    r = jax.nn.sigmoid(ya + ba_ref[...])
    gi = jax.nn.sigmoid(yx + bx_ref[...])
    log_a = (-RG_C * _softplus(-lam_ref[...])) * r
    a_s[...] = jnp.exp(log_a)
    th = jnp.tanh(log_a)
    v = -2.0 * th / (1.0 - th)
    b_s[...] = (v * lax.rsqrt(jnp.maximum(v, 1e-30))) * (gi * xr)

    @pl.when(j == 0)
    def _():
        h_s[...] = jnp.zeros_like(h_s)

    dst = hb_s if reverse else o_ref.at[0]

    def step(t, h):
        tt = tile - 1 - t if reverse else t
        h = a_s[pl.ds(tt, 1), :] * h + b_s[pl.ds(tt, 1), :]
        dst[pl.ds(tt, 1), :] = h
        return h

    h_s[...] = lax.fori_loop(0, tile, step, h_s[...], unroll=8)
    if reverse:
        o_ref[0] = ((hf_ref[0] + hb_s[...]) * gz_ref[0]).astype(BF16)


def _rnn(z0, conv_w, conv_b, wa_p, wx_p, ba, bx, lam, tile, n_lat, reverse, hf=None, gz=None):
    b, ltot, d = z0.shape
    n_halo = ltot // SUBLANE
    per = tile // SUBLANE
    if reverse:
        cidx = lambda j: jnp.where(j == 0, n_lat, n_lat - j)
        oidx = lambda j: jnp.where(j == 0, n_lat - 1, n_lat - j)
    else:
        cidx = lambda j: jnp.where(j == 0, n_lat, j - 1)
        oidx = lambda j: jnp.maximum(j - 1, 0)
    const2 = lambda bb, j: (0, 0)
    const3 = lambda bb, j: (0, 0, 0)
    in_specs = [
        pl.BlockSpec((1, tile, d), lambda bb, j: (bb, cidx(j), 0)),
        pl.BlockSpec((1, SUBLANE, d), lambda bb, j: (bb, jnp.maximum(cidx(j) * per - 1, 0), 0)),
        pl.BlockSpec((1, SUBLANE, d), lambda bb, j: (bb, jnp.minimum((cidx(j) + 1) * per, n_halo - 1), 0)),
        pl.BlockSpec((CONV_W, d), const2),
        pl.BlockSpec((1, d), const2),
        pl.BlockSpec(wa_p.shape, const3),
        pl.BlockSpec(wx_p.shape, const3),
        pl.BlockSpec((1, d), const2),
        pl.BlockSpec((1, d), const2),
        pl.BlockSpec((1, d), const2),
    ]
    args = [z0, z0, z0, conv_w, conv_b, wa_p, wx_p, ba, bx, lam]
    scratch = [pltpu.VMEM((tile, d), F32), pltpu.VMEM((tile, d), F32)]
    if reverse:
        in_specs += [pl.BlockSpec((1, tile, d), lambda bb, j: (bb, oidx(j), 0)),
                     pl.BlockSpec((1, tile, d), lambda bb, j: (bb, cidx(j), 0))]
        args += [hf, gz]
        scratch += [pltpu.VMEM((tile, d), F32)]
    scratch += [pltpu.VMEM((1, d), F32)]
    return pl.pallas_call(
        functools.partial(_rnn_kernel, d=d, tile=tile, n_lat=n_lat, reverse=reverse),
        grid=(b, n_lat + 1),
        in_specs=in_specs,
        out_specs=pl.BlockSpec((1, tile, d), lambda bb, j: (bb, oidx(j), 0)),
        out_shape=jax.ShapeDtypeStruct((b, n_lat * tile, d), BF16 if reverse else F32),
        scratch_shapes=scratch,
        compiler_params=_cparams(("arbitrary", "arbitrary")),
        name="rnn_bwd" if reverse else "rnn_fwd",
    )(*args)


def _attn_kernel(q_ref, k_ref, vt_ref, o_ref, s_even, s_odd, m_even, m_odd):
    g = pl.program_id(0)

    @pl.when(g == 0)
    def _():
        s_odd[...] = jnp.zeros_like(s_odd)
        m_odd[...] = jnp.zeros_like(m_odd)

    n_keys = k_ref.shape[1]
    tq = q_ref.shape[1]

    def step(s_new, m_new, s_prev, m_prev):
        heads = range(2)
        lanes = [slice(hh * HEAD_PAD, (hh + 1) * HEAD_PAD) for hh in heads]
        qs = [q_ref[0, :, lanes[hh]] for hh in heads]
        m_run = [None, None]
        acc = [jnp.zeros((HEAD_PAD, tq), F32) for _ in heads]
        for c in range(n_keys // MXU_COLS):
            rows = slice(c * MXU_COLS, (c + 1) * MXU_COLS)
            for hh in heads:
                st = lax.dot_general(k_ref[0, rows, lanes[hh]], qs[hh], (((1,), (1,)), ((), ())),
                                     preferred_element_type=F32)
                s_new[hh, rows, :] = st
                cm = jnp.max(st, axis=0, keepdims=True)
                m_run[hh] = cm if c == 0 else jnp.maximum(m_run[hh], cm)
                pt = jnp.exp2(s_prev[hh, rows, :] - m_prev[hh]).astype(BF16)
                acc[hh] = acc[hh] + jnp.dot(vt_ref[0, lanes[hh], rows], pt, preferred_element_type=F32)
        for hh in heads:
            m_new[hh] = m_run[hh]
        outs = [acc[hh][0:V_HEAD] / acc[hh][V_HEAD:V_HEAD + 1] for hh in heads]
        o_ref[0] = jnp.concatenate(outs, axis=0).T.astype(BF16)

    @pl.when(g % 2 == 0)
    def _():
        step(s_even, m_even, s_odd, m_odd)

    @pl.when(g % 2 == 1)
    def _():
        step(s_odd, m_odd, s_even, m_even)


def _attention(q, k, vt, s, tq):
    b, ltot, _ = q.shape
    pairs = MLA_HEADS // 2
    nq = s // tq
    n_items = b * pairs * nq

    def item(g):
        return g // (pairs * nq), (g // nq) % pairs, g % nq

    def cur(g):
        return item(jnp.minimum(g, n_items - 1))

    def prev(g):
        return item(jnp.maximum(g - 1, 0))

    scores = pltpu.VMEM((2, ltot, tq), F32)
    colmax = pltpu.VMEM((2, 1, tq), F32)
    return pl.pallas_call(
        _attn_kernel,
        grid=(n_items + 1,),
        in_specs=[pl.BlockSpec((1, tq, 2 * HEAD_PAD), lambda g: (cur(g)[0], cur(g)[2], cur(g)[1])),
                  pl.BlockSpec((1, ltot, 2 * HEAD_PAD), lambda g: (cur(g)[0], 0, cur(g)[1])),
                  pl.BlockSpec((1, 2 * HEAD_PAD, ltot), lambda g: (prev(g)[0], prev(g)[1], 0))],
        out_specs=pl.BlockSpec((1, tq, 2 * V_HEAD), lambda g: (prev(g)[0], prev(g)[2], prev(g)[1])),
        out_shape=jax.ShapeDtypeStruct((b, s, MLA_HEADS * V_HEAD), BF16),
        scratch_shapes=[scores, scores, colmax, colmax],
        compiler_params=_cparams(("arbitrary",)),
        name="attn",
    )(q, k, vt)


def _merge_kernel(rnn_ref, attn_ref, sg_ref, x_ref, mod_ref, g2_ref, wr_ref, wm_ref, wo_ref, wrt_ref,
                  xn_ref, h2_ref, aff_ref, *, d):
    gate_mod = mod_ref[0, :, 2 * d:3 * d]
    shift = mod_ref[0, :, 3 * d:4 * d]
    scale = mod_ref[0, :, 4 * d:5 * d]
    pr = jnp.dot(rnn_ref[0], wr_ref[...], preferred_element_type=F32)
    pa = jnp.dot(attn_ref[0], wm_ref[...], preferred_element_type=F32)
    merged = sg_ref[0, :, 0:d] * pr + sg_ref[0, :, d:2 * d] * pa
    xn = x_ref[0] + gate_mod * jnp.dot(merged.astype(BF16), wo_ref[...], preferred_element_type=F32)
    xn_ref[0] = xn
    inv = lax.rsqrt(jnp.mean(xn * xn, axis=-1, keepdims=True) + EPS)
    h2 = (xn * inv * g2_ref[...]) * (1.0 + scale) + shift
    h2_ref[0] = h2
    logits = jnp.dot(h2.astype(BF16), wrt_ref[...], preferred_element_type=F32)[:, 0:N_EXPERTS]
    e = jnp.exp(logits - jnp.max(logits, axis=-1, keepdims=True))
    aff_ref[0] = e / jnp.sum(e, axis=-1, keepdims=True)


def _merge(rnn, attn, sg, x, mod, g2, wr, wm, wo, wrt, tile):
    b, s, d = x.shape
    const = lambda bb, i: (0, 0)
    row = lambda bb, i: (bb, i, 0)
    return pl.pallas_call(
        functools.partial(_merge_kernel, d=d),
        grid=(b, s // tile),
        in_specs=[pl.BlockSpec((1, tile, d), row),
                  pl.BlockSpec((1, tile, attn.shape[2]), row),
                  pl.BlockSpec((1, tile, 2 * d), row),
                  pl.BlockSpec((1, tile, d), row),
                  pl.BlockSpec((1, 1, N_MOD * d), lambda bb, i: (2 * bb, 0, 0)),
                  pl.BlockSpec((1, d), const),
                  pl.BlockSpec(wr.shape, const),
                  pl.BlockSpec(wm.shape, const),
                  pl.BlockSpec(wo.shape, const),
                  pl.BlockSpec(wrt.shape, const)],
        out_specs=[pl.BlockSpec((1, tile, d), row),
                   pl.BlockSpec((1, tile, d), row),
                   pl.BlockSpec((1, tile, N_EXPERTS), row)],
        out_shape=(jax.ShapeDtypeStruct((b, s, d), F32),
                   jax.ShapeDtypeStruct((b, s, d), F32),
                   jax.ShapeDtypeStruct((b, s, N_EXPERTS), F32)),
        compiler_params=_cparams(("arbitrary", "arbitrary")),
        name="merge",
    )(rnn, attn, sg, x, mod, g2, wr, wm, wo, wrt)


def _excl_cumsum_lanes(x01, tri):
    rows, length = x01.shape
    offset = jnp.zeros((rows, 1), F32)
    pieces = []
    for blk in range(length // LANE):
        xb = x01[:, blk * LANE:(blk + 1) * LANE]
        inc = jnp.dot(xb.astype(BF16), tri, preferred_element_type=F32)
        pieces.append(inc - xb + offset)
        offset = offset + jnp.sum(xb, axis=-1, keepdims=True)
    return jnp.concatenate(pieces, axis=-1)


def _topk_kernel(aff_ref, idx_ref, gate_ref, pos_s, sel_s, *, cap):
    a = aff_ref[0]
    n_e, length = a.shape
    keys = lax.bitcast_convert_type(a, jnp.int32)

    def bisect(i, cand):
        t = cand | (jnp.int32(1) << (30 - i))
        cnt = jnp.sum((keys >= t).astype(jnp.int32), axis=-1, keepdims=True)
        return jnp.where(cnt >= cap, t, cand)

    thr = lax.fori_loop(0, 31, bisect, jnp.zeros((n_e, 1), jnp.int32))
    gt = (keys > thr).astype(F32)
    eq = (keys == thr).astype(F32)
    need = cap - jnp.sum(gt, axis=-1, keepdims=True)
    r_i = lax.broadcasted_iota(jnp.int32, (LANE, LANE), 0)
    c_i = lax.broadcasted_iota(jnp.int32, (LANE, LANE), 1)
    tri = (r_i <= c_i).astype(BF16)
    tie_rank = _excl_cumsum_lanes(eq, tri)
    sel = gt + eq * (tie_rank < need).astype(F32)
    pos_s[...] = _excl_cumsum_lanes(sel, tri)
    sel_s[...] = sel

    tok = lax.broadcasted_iota(jnp.int32, (1, length), 1)
    tok_hi = (tok >> 6).astype(F32)
    tok_lo = (tok & 63).astype(F32)
    slot = lax.broadcasted_iota(jnp.int32, (cap, 1), 0).astype(F32)
    zeros = jnp.zeros((2 * SUBLANE - 5, length), F32)

    def compact(e, carry):
        pos_row = pos_s[pl.ds(e, 1), :]
        sel_row = sel_s[pl.ds(e, 1), :]
        onehot = jnp.where(jnp.logical_and(slot == pos_row, sel_row > 0.5), 1.0, 0.0).astype(BF16)
        g = aff_ref[0, pl.ds(e, 1), :]
        g1 = g.astype(BF16).astype(F32)
        g2 = (g - g1).astype(BF16).astype(F32)
        g3 = g - g1 - g2
        vals = jnp.concatenate([tok_hi, tok_lo, g1, g2, g3, zeros], axis=0).astype(BF16)
        res = lax.dot_general(vals, onehot, (((1,), (1,)), ((), ())), preferred_element_type=F32)
        idx_ref[0, pl.ds(e, 1), :] = (res[0:1] * 64.0 + res[1:2]).astype(jnp.int32)
        gate_ref[0, pl.ds(e, 1), :] = res[2:3] + res[3:4] + res[4:5]
        return carry

    lax.fori_loop(0, n_e, compact, 0)


def _topk(aff_t, cap):
    b, n_e, length = aff_t.shape
    blk = lambda bb: (bb, 0, 0)
    return pl.pallas_call(
        functools.partial(_topk_kernel, cap=cap),
        grid=(b,),
        in_specs=[pl.BlockSpec((1, n_e, length), blk)],
        out_specs=[pl.BlockSpec((1, n_e, cap), blk), pl.BlockSpec((1, n_e, cap), blk)],
        out_shape=(jax.ShapeDtypeStruct((b, n_e, cap), jnp.int32), jax.ShapeDtypeStruct((b, n_e, cap), F32)),
        scratch_shapes=[pltpu.VMEM((n_e, length), F32), pltpu.VMEM((n_e, length), F32)],
        compiler_params=_cparams(("arbitrary",)),
        name="topk",
    )(aff_t)


def _gather_kernel(idx_ref, h_ref, o_ref, *, cap):
    def body(j, carry):
        r = idx_ref[0, 0, j]
        o_ref[0, 0, pl.ds(j, 1), :] = h_ref[0, pl.ds(r, 1), :]
        return carry

    lax.fori_loop(0, cap, body, 0, unroll=8)


def _gather(idx, h2, cap):
    b, s, d = h2.shape
    return pl.pallas_call(
        functools.partial(_gather_kernel, cap=cap),
        grid=(b, N_EXPERTS),
        in_specs=[pl.BlockSpec((1, 1, cap), lambda bb, e: (bb * N_EXPERTS + e, 0, 0), memory_space=pltpu.SMEM),
                  pl.BlockSpec((1, s, d), lambda bb, e: (bb, 0, 0))],
        out_specs=pl.BlockSpec((1, 1, cap, d), lambda bb, e: (bb, e, 0, 0)),
        out_shape=jax.ShapeDtypeStruct((b, N_EXPERTS, cap, d), F32),
        compiler_params=_cparams(("arbitrary", "arbitrary")),
        name="gather",
    )(idx, h2)


def _expert_kernel(x_ref, wg_ref, wu_ref, wd_ref, mod_ref, y_ref, wg_s, wu_s, wd_s, *, d):
    @pl.when(pl.program_id(1) == 0)
    def _():
        wg_s[...] = wg_ref[0].astype(BF16)
        wu_s[...] = wu_ref[0].astype(BF16)
        wd_s[...] = wd_ref[0].astype(BF16)

    xb = x_ref[0, 0].astype(BF16)
    g = jnp.dot(xb, wg_s[...], preferred_element_type=F32)
    u = jnp.dot(xb, wu_s[...], preferred_element_type=F32)
    hid = (g * jax.nn.sigmoid(g) * u).astype(BF16)
    y = jnp.dot(hid, wd_s[...], preferred_element_type=F32)
    y_ref[0, 0] = y * mod_ref[0, :, 5 * d:6 * d]


def _experts(xg, w_gate, w_up, w_down, mod):
    b, n_e, cap, d = xg.shape
    f = w_gate.shape[2]
    return pl.pallas_call(
        functools.partial(_expert_kernel, d=d),
        grid=(n_e, b),
        in_specs=[pl.BlockSpec((1, 1, cap, d), lambda e, bb: (bb, e, 0, 0)),
                  pl.BlockSpec((1, d, f), lambda e, bb: (e, 0, 0)),
                  pl.BlockSpec((1, d, f), lambda e, bb: (e, 0, 0)),
                  pl.BlockSpec((1, f, d), lambda e, bb: (e, 0, 0)),
                  pl.BlockSpec((1, 1, N_MOD * d), lambda e, bb: (2 * bb, 0, 0))],
        out_specs=pl.BlockSpec((1, 1, cap, d), lambda e, bb: (bb, e, 0, 0)),
        out_shape=jax.ShapeDtypeStruct((b, n_e, cap, d), F32),
        scratch_shapes=[pltpu.VMEM((d, f), BF16), pltpu.VMEM((d, f), BF16), pltpu.VMEM((f, d), BF16)],
        compiler_params=_cparams(("arbitrary", "arbitrary")),
        name="experts",
    )(xg, w_gate, w_up, w_down, mod)


def _scatter_kernel(idx_ref, gate_ref, y_ref, x_ref, o_ref, *, cap):
    e = pl.program_id(1)
    piece = x_ref.shape[1]

    @pl.when(e == 0)
    def _():
        o_ref[...] = jnp.zeros_like(o_ref)

    rows = pl.ds(pl.multiple_of(e * piece, piece), piece)
    o_ref[0, rows, :] = o_ref[0, rows, :] + x_ref[0]

    def body(j, carry):
        r = idx_ref[0, 0, j]
        g = gate_ref[0, 0, j]
        o_ref[0, pl.ds(r, 1), :] = o_ref[0, pl.ds(r, 1), :] + g * y_ref[0, 0, pl.ds(j, 1), :]
        return carry

    lax.fori_loop(0, cap, body, 0, unroll=8)


def _scatter(idx, gate, y, x_new, cap):
    b, s, d = x_new.shape
    assert s % N_EXPERTS == 0
    return pl.pallas_call(
        functools.partial(_scatter_kernel, cap=cap),
        grid=(b, N_EXPERTS),
        in_specs=[pl.BlockSpec((1, 1, cap), lambda bb, e: (bb * N_EXPERTS + e, 0, 0), memory_space=pltpu.SMEM),
                  pl.BlockSpec((1, 1, cap), lambda bb, e: (bb * N_EXPERTS + e, 0, 0), memory_space=pltpu.SMEM),
                  pl.BlockSpec((1, 1, cap, d), lambda bb, e: (bb, e, 0, 0)),
                  pl.BlockSpec((1, s // N_EXPERTS, d), lambda bb, e: (bb, e, 0))],
        out_specs=pl.BlockSpec((1, s, d), lambda bb, e: (bb, 0, 0)),
        out_shape=jax.ShapeDtypeStruct((b, s, d), F32),
        compiler_params=_cparams(("arbitrary", "arbitrary")),
        name="scatter",
    )(idx, gate, y, x_new)


def _pad_lanes(a, width):
    return jnp.pad(a, [(0, 0)] * (a.ndim - 1) + [(0, width - a.shape[-1])])


def _rope_tables(s, lc, g_q_nope, g_q_rope, g_k_nope, g_k_rope):
    rows = s // GRID_W
    row = jnp.repeat(jnp.arange(rows, dtype=F32), GRID_W)
    col = jnp.tile(jnp.arange(GRID_W, dtype=F32), rows)
    n_pairs = QK_ROPE // 4
    inv_freq = ROPE_BASE ** (-jnp.arange(n_pairs, dtype=F32) / n_pairs)
    ang = jnp.concatenate([row[:, None] * inv_freq, col[:, None] * inv_freq], axis=-1)
    cos = jnp.concatenate([jnp.cos(ang), jnp.ones((lc, HALF_ROPE), F32)], axis=0)
    sin = jnp.concatenate([jnp.sin(ang), jnp.zeros((lc, HALF_ROPE), F32)], axis=0)
    ltot = s + lc
    ones = jnp.ones((ltot, 1), F32)

    def tables(g_nope, g_rope, scale):
        g1, g2 = g_rope[:HALF_ROPE], g_rope[HALF_ROPE:]
        tc = jnp.concatenate([ones * g_nope[None, :], cos * g1, cos * g2], axis=-1) * scale
        ts = jnp.concatenate([jnp.zeros((ltot, QK_NOPE), F32), -sin * g2, sin * g1], axis=-1) * scale
        return _pad_lanes(tc, LANE), _pad_lanes(ts, LANE)

    tqc, tqs = tables(g_q_nope, g_q_rope, ATTN_SCALE * LOG2_E)
    tkc, tks = tables(jnp.zeros((QK_NOPE,), F32), g_k_rope, 1.0)
    gkn = _pad_lanes(g_k_nope[None, :], LANE)
    return tqc, tqs, tkc, tks, gkn


def _swap_halves(a):
    return jnp.concatenate([a[..., HALF_ROPE:], a[..., :HALF_ROPE]], axis=-1)


def _layout_weights(w_in, w_uq, w_uk, w_uv, d):
    o_q = 2 * d
    o_kv = o_q + Q_LORA
    o_kr = o_kv + KV_LORA
    o_g = o_kr + QK_ROPE
    kr = w_in[:, o_kr:o_g]
    zl = jnp.zeros((w_in.shape[0], QK_NOPE), F32)
    kr_a = _pad_lanes(jnp.concatenate([zl, kr], axis=-1), LANE)
    kr_s = _pad_lanes(jnp.concatenate([zl, _swap_halves(kr)], axis=-1), LANE)
    win_p = jnp.concatenate([w_in[:, :o_kr], kr_a, kr_s, w_in[:, o_g:]], axis=-1).astype(BF16)

    uq = w_uq.reshape(Q_LORA, MLA_HEADS, QK_NOPE + QK_ROPE)
    nope, rope = uq[..., :QK_NOPE], uq[..., QK_NOPE:]
    wuq_p = _pad_lanes(jnp.concatenate([nope, rope], axis=-1), HEAD_PAD).reshape(Q_LORA, -1).astype(BF16)
    wuq_s = _pad_lanes(jnp.concatenate([jnp.zeros_like(nope), _swap_halves(rope)], axis=-1),
                       HEAD_PAD).reshape(Q_LORA, -1).astype(BF16)
    wuk_p = _pad_lanes(w_uk.reshape(KV_LORA, MLA_HEADS, QK_NOPE), HEAD_PAD).reshape(KV_LORA, -1).astype(BF16)
    wuv_p = _pad_lanes(w_uv.reshape(KV_LORA, MLA_HEADS, V_HEAD), HEAD_PAD).reshape(KV_LORA, -1).astype(BF16)
    return win_p, wuq_p, wuq_s, wuk_p, wuv_p


def _pair_blocks(w):
    nb, n, _ = w.shape
    w = w.reshape(nb // 2, 2, n, n)
    z = jnp.zeros((nb // 2, n, n), w.dtype)
    top = jnp.concatenate([w[:, 0], z], axis=-1)
    bot = jnp.concatenate([z, w[:, 1]], axis=-1)
    return jnp.concatenate([top, bot], axis=-2).astype(BF16)


def kernel(x, c, ctx, c_ctx, w_ada, b_ada, g_norm1, g_norm2, w_in, conv_w, conv_b, rg_wa, rg_ba, rg_wx, rg_bx,
           rg_lambda, w_rnn_out, g_q_lora, w_uq, g_kv_lora, w_uk, w_uv, g_q_nope, g_q_rope, g_k_nope, g_k_rope,
           w_mla_out, w_o, w_router, w_e_gate, w_e_up, w_e_down):
    assert w_ada.shape[0] == 1, "single-layer kernel"
    b, s, d = x.shape
    lc = ctx.shape[1]
    tile = lc
    n_lat = s // tile
    cap = CAPACITY_FACTOR * s // N_EXPERTS

    pad_rows = (-(b + 1)) % SUBLANE
    cc = jnp.concatenate([c, c_ctx[None, :], jnp.zeros((pad_rows, d), F32)], axis=0)
    mod_all = _ada(cc, w_ada[0], b_ada[0])
    mod = jnp.stack([mod_all[:b], jnp.broadcast_to(mod_all[b], (b, N_MOD * d))], axis=1).reshape(2 * b, 1, N_MOD * d)

    win_p, wuq_p, wuq_s, wuk_p, wuv_p = _layout_weights(w_in[0], w_uq[0], w_uk[0], w_uv[0], d)
    tqc, tqs, tkc, tks, gkn = _rope_tables(s, lc, g_q_nope[0], g_q_rope[0], g_k_nope[0], g_k_rope[0])
    z0, gz, q, k, v, sg = _inproj(x, ctx, mod, g_norm1, win_p, g_q_lora, wuq_p, wuq_s, g_kv_lora, wuk_p,
                                  wuv_p, gkn, tqc, tqs, tkc, tks, tile)

    h_f = _rnn(z0, conv_w[0], conv_b, _pair_blocks(rg_wa[0, 0]), _pair_blocks(rg_wx[0, 0]),
               rg_ba[0, 0:1], rg_bx[0, 0:1], rg_lambda[0, 0:1], tile, n_lat, False)
    rnn = _rnn(z0, conv_w[0], conv_b, _pair_blocks(rg_wa[0, 1]), _pair_blocks(rg_wx[0, 1]),
               rg_ba[0, 1:2], rg_bx[0, 1:2], rg_lambda[0, 1:2], tile, n_lat, True, hf=h_f, gz=gz)

    attn = _attention(q, k, v, s, MXU_COLS)

    x_new, h2, aff = _merge(rnn, attn, sg, x, mod, g_norm2, w_rnn_out[0].astype(BF16), w_mla_out[0].astype(BF16),
                            w_o[0].astype(BF16), _pad_lanes(w_router[0], LANE).astype(BF16), 2 * tile)

    idx, gate = _topk(jnp.swapaxes(aff, 1, 2), cap)
    idx = idx.reshape(b * N_EXPERTS, 1, cap)
    gate = gate.reshape(b * N_EXPERTS, 1, cap)
    xg = _gather(idx, h2, cap)
    y = _experts(xg, w_e_gate[0], w_e_up[0], w_e_down[0], mod)
    return _scatter(idx, gate, y, x_new, cap)
```

```python
import functools

import jax
import jax.numpy as jnp
from jax import lax
from jax.experimental import pallas as pl
from jax.experimental.pallas import tpu as pltpu

EPS = 1e-6
N_MOD = 6
GRID_W = 64
RG_BLOCKS = 8
CONV_W = 4
CONV_PAD_L = 2
RG_C = 8.0
MLA_HEADS = 8
QK_NOPE = 64
QK_ROPE = 32
V_HEAD = 64
Q_LORA = 256
KV_LORA = 128
ROPE_BASE = 10000.0
ATTN_SCALE = (QK_NOPE + QK_ROPE) ** -0.5
LOG2_E = 1.4426950408889634
N_EXPERTS = 16
CAPACITY_FACTOR = 2

LANE = 128
SUBLANE = 8
MXU_COLS = 256
HEAD_PAD = LANE
HALF_ROPE = QK_ROPE // 2
RNN_SUB = 64
SLOT_LO = 32
VMEM_LIMIT = 56 * 1024 * 1024

BF16 = jnp.bfloat16
F32 = jnp.float32


def _cparams(sem):
    return pltpu.CompilerParams(dimension_semantics=sem, vmem_limit_bytes=VMEM_LIMIT)


def _ada_kernel(c_ref, w_ref, b_ref, o_ref):
    cc = c_ref[...]
    s = cc * jax.nn.sigmoid(cc)
    o_ref[...] = jnp.dot(s, w_ref[...], preferred_element_type=F32) + b_ref[...]


def _ada(cc, w_ada, b_ada):
    rows, d = cc.shape
    n = w_ada.shape[1]
    bn = d
    return pl.pallas_call(
        _ada_kernel,
        grid=(n // bn,),
        in_specs=[pl.BlockSpec((rows, d), lambda j: (0, 0)),
                  pl.BlockSpec((d, bn), lambda j: (0, j)),
                  pl.BlockSpec((1, bn), lambda j: (0, j))],
        out_specs=pl.BlockSpec((rows, bn), lambda j: (0, j)),
        out_shape=jax.ShapeDtypeStruct((rows, n), F32),
        compiler_params=_cparams(("arbitrary",)),
        name="ada",
    )(cc, w_ada, b_ada.reshape(1, n))


def _inproj_kernel(x_ref, ctx_ref, mod_ref, g1_ref, win_ref, gq_ref, wuq_ref, wuqs_ref, gkv_ref, wuk_ref,
                   wuv_ref, gkn_ref, tqc_ref, tqs_ref, tkc_ref, tks_ref,
                   z0_ref, gz_ref, q_ref, k_ref, v_ref, sg_ref, *, d, n_lat_tiles):
    i = pl.program_id(1)
    is_ctx = i == n_lat_tiles
    xt = jnp.where(is_ctx, ctx_ref[0], x_ref[0])
    shift = mod_ref[0, :, 0:d]
    scale = mod_ref[0, :, d:2 * d]
    inv = lax.rsqrt(jnp.mean(xt * xt, axis=-1, keepdims=True) + EPS)
    xn = ((xt * inv * g1_ref[...]) * (1.0 + scale) + shift).astype(BF16)

    def seg(lo, hi):
        return jnp.dot(xn, win_ref[:, lo:hi], preferred_element_type=F32)

    o = 0
    z0_ref[0] = seg(o, o + d)
    o += d
    gz_ref[0] = jax.nn.gelu(seg(o, o + d))
    o += d
    zq = seg(o, o + Q_LORA)
    o += Q_LORA
    zkv = seg(o, o + KV_LORA)
    o += KV_LORA
    kr_a = seg(o, o + LANE)
    o += LANE
    kr_s = seg(o, o + LANE)
    o += LANE
    sg_ref[0] = jax.nn.sigmoid(seg(o, o + 2 * d))

    lane = lax.broadcasted_iota(jnp.int32, (1, LANE), 1)
    is_nope = lane < QK_NOPE
    is_rope = jnp.logical_and(lane >= QK_NOPE, lane < QK_NOPE + QK_ROPE)

    zqn = (zq * lax.rsqrt(jnp.mean(zq * zq, axis=-1, keepdims=True) + EPS) * gq_ref[...]).astype(BF16)
    qa = jnp.dot(zqn, wuq_ref[...], preferred_element_type=F32)
    qs = jnp.dot(zqn, wuqs_ref[...], preferred_element_type=F32)
    tqc = tqc_ref[...]
    tqs = tqs_ref[...]
    for h in range(MLA_HEADS):
        a = qa[:, h * HEAD_PAD:(h + 1) * HEAD_PAD]
        s = qs[:, h * HEAD_PAD:(h + 1) * HEAD_PAD]
        sq = a * a
        ssn = jnp.sum(jnp.where(is_nope, sq, 0.0), axis=-1, keepdims=True)
        ssr = jnp.sum(jnp.where(is_rope, sq, 0.0), axis=-1, keepdims=True)
        invh = jnp.where(is_nope, lax.rsqrt(ssn * (1.0 / QK_NOPE) + EPS), lax.rsqrt(ssr * (1.0 / QK_ROPE) + EPS))
        q_ref[0, :, h * HEAD_PAD:(h + 1) * HEAD_PAD] = (invh * (a * tqc + s * tqs)).astype(BF16)

    ckv = (zkv * lax.rsqrt(jnp.mean(zkv * zkv, axis=-1, keepdims=True) + EPS) * gkv_ref[...]).astype(BF16)
    lane_h = lax.broadcasted_iota(jnp.int32, (1, MLA_HEADS * HEAD_PAD), 1) % HEAD_PAD
    vp = jnp.dot(ckv, wuv_ref[...], preferred_element_type=F32) + jnp.where(lane_h == V_HEAD, 1.0, 0.0)
    v_ref[0] = vp.T.astype(BF16)
    ka = jnp.dot(ckv, wuk_ref[...], preferred_element_type=F32)
    inv_r = lax.rsqrt(jnp.sum(kr_a * kr_a, axis=-1, keepdims=True) * (1.0 / QK_ROPE) + EPS)
    kr = inv_r * (kr_a * tkc_ref[...] + kr_s * tks_ref[...])
    gkn = gkn_ref[...]
    for h in range(MLA_HEADS):
        a = ka[:, h * HEAD_PAD:(h + 1) * HEAD_PAD]
        invh = lax.rsqrt(jnp.sum(a * a, axis=-1, keepdims=True) * (1.0 / QK_NOPE) + EPS)
        k_ref[0, :, h * HEAD_PAD:(h + 1) * HEAD_PAD] = (a * invh * gkn + kr).astype(BF16)


def _inproj(x, ctx, mod, g1, win_p, gq, wuq_p, wuq_s, gkv, wuk_p, wuv, gkn, tqc, tqs, tkc, tks, tile):
    b, s, d = x.shape
    lc = ctx.shape[1]
    assert lc == tile and s % tile == 0
    nl = s // tile
    ltot = s + lc
    n_in = win_p.shape[1]
    hp = MLA_HEADS * HEAD_PAD
    const = lambda bb, i: (0, 0)
    row = lambda bb, i: (bb, i, 0)
    tab = lambda bb, i: (i, 0)
    out_shapes = (
        jax.ShapeDtypeStruct((b, ltot, d), F32),
        jax.ShapeDtypeStruct((b, ltot, d), F32),
        jax.ShapeDtypeStruct((b, ltot, hp), BF16),
        jax.ShapeDtypeStruct((b, ltot, hp), BF16),
        jax.ShapeDtypeStruct((b, hp, ltot), BF16),
        jax.ShapeDtypeStruct((b, ltot, 2 * d), F32),
    )
    return pl.pallas_call(
        functools.partial(_inproj_kernel, d=d, n_lat_tiles=nl),
        grid=(b, nl + 1),
        in_specs=[
            pl.BlockSpec((1, tile, d), lambda bb, i: (bb, jnp.minimum(i, nl - 1), 0)),
            pl.BlockSpec((1, tile, d), lambda bb, i: (bb, 0, 0)),
            pl.BlockSpec((1, 1, N_MOD * d), lambda bb, i: (2 * bb + i // nl, 0, 0)),
            pl.BlockSpec((1, d), const),
            pl.BlockSpec((d, n_in), const),
            pl.BlockSpec((1, Q_LORA), const),
            pl.BlockSpec((Q_LORA, hp), const),
            pl.BlockSpec((Q_LORA, hp), const),
            pl.BlockSpec((1, KV_LORA), const),
            pl.BlockSpec((KV_LORA, hp), const),
            pl.BlockSpec((KV_LORA, hp), const),
            pl.BlockSpec((1, LANE), const),
            pl.BlockSpec((tile, LANE), tab),
            pl.BlockSpec((tile, LANE), tab),
            pl.BlockSpec((tile, LANE), tab),
            pl.BlockSpec((tile, LANE), tab),
        ],
        out_specs=[
            pl.BlockSpec((1, tile, d), row),
            pl.BlockSpec((1, tile, d), row),
            pl.BlockSpec((1, tile, hp), row),
            pl.BlockSpec((1, tile, hp), row),
            pl.BlockSpec((1, hp, tile), lambda bb, i: (bb, 0, i)),
            pl.BlockSpec((1, tile, 2 * d), row),
        ],
        out_shape=out_shapes,
        compiler_params=_cparams(("arbitrary", "arbitrary")),
        name="inproj",
    )(x, ctx, mod, g1, win_p, gq, wuq_p, wuq_s, gkv, wuk_p, wuv, gkn, tqc, tqs, tkc, tks)


def _softplus(x):
    e = jnp.exp(-jnp.abs(x))
    u = 1.0 + e
    tiny = u == 1.0
    log1p_e = jnp.where(tiny, e, jnp.log(u) * (e / jnp.where(tiny, 1.0, u - 1.0)))
    return jnp.maximum(x, 0.0) + log1p_e


def _rnn_kernel(*refs, d, tile, n_lat, reverse):
    if reverse:
        (xr_ref, wa_ref, wx_ref, ba_ref, bx_ref, lam_ref, hf_ref, gz_ref, o_ref, a_s, b_s, hb_s, h_s) = refs
    else:
        (z_ref, zp_ref, zn_ref, cw_ref, cb_ref, wa_ref, wx_ref, ba_ref, bx_ref, lam_ref,
         o_ref, xr_ref, a_s, b_s, h_s) = refs
    j = pl.program_id(1)
    if not reverse:
        c = jnp.where(j == 0, n_lat, j - 1)
        prev_zero = jnp.logical_or(c == 0, c == n_lat)
        next_zero = c >= n_lat - 1
        ext = jnp.concatenate([jnp.where(prev_zero, 0.0, zp_ref[0]), z_ref[0], jnp.where(next_zero, 0.0, zn_ref[0])],
                              axis=0)
        n_ext = tile + 2 * SUBLANE
        taps = [ext if (CONV_PAD_L - k) % n_ext == 0 else pltpu.roll(ext, (CONV_PAD_L - k) % n_ext, 0)
                for k in range(CONV_W)]
    pair = 2 * (d // RG_BLOCKS)
    neg_c_softplus = -RG_C * _softplus(-lam_ref[...])
    dst = hb_s if reverse else o_ref.at[0]

    h = jnp.where(j == 0, 0.0, h_s[...])
    subs = range(tile // RNN_SUB)
    for sb in (reversed(subs) if reverse else subs):
        r0 = sb * RNN_SUB
        if reverse:
            xr = xr_ref[0, r0:r0 + RNN_SUB, :]
        else:
            xr = cb_ref[...]
            for k in range(CONV_W):
                xr = xr + taps[k][SUBLANE + r0:SUBLANE + r0 + RNN_SUB] * cw_ref[k:k + 1, :]
            xr_ref[0, r0:r0 + RNN_SUB, :] = xr
        xb = xr.astype(BF16)
        ya = jnp.concatenate([jnp.dot(xb[:, m * pair:(m + 1) * pair], wa_ref[m], preferred_element_type=F32)
                              for m in range(d // pair)], axis=-1)
        yx = jnp.concatenate([jnp.dot(xb[:, m * pair:(m + 1) * pair], wx_ref[m], preferred_element_type=F32)
                              for m in range(d // pair)], axis=-1)
        r = jax.nn.sigmoid(ya + ba_ref[...])
        gi = jax.nn.sigmoid(yx + bx_ref[...])
        log_a = neg_c_softplus * r
        a_s[r0:r0 + RNN_SUB, :] = jnp.exp(log_a)
        th = jnp.tanh(log_a)
        v = -2.0 * th / (1.0 - th)
        b_s[r0:r0 + RNN_SUB, :] = (v * lax.rsqrt(jnp.maximum(v, 1e-30))) * (gi * xr)
        rows = range(r0, r0 + RNN_SUB)
        for t in (reversed(rows) if reverse else rows):
            h = a_s[t:t + 1, :] * h + b_s[t:t + 1, :]
            dst[t:t + 1, :] = h
    h_s[...] = h
    if reverse:
        o_ref[0] = ((hf_ref[0] + hb_s[...]) * gz_ref[0]).astype(BF16)


def _rnn(z, wa_p, wx_p, ba, bx, lam, tile, n_lat, reverse, conv=None, hf=None, gz=None):
    b, ltot, d = z.shape
    n_halo = ltot // SUBLANE
    per = tile // SUBLANE
    if reverse:
        cidx = lambda j: jnp.where(j == 0, n_lat, n_lat - j)
        oidx = lambda j: jnp.where(j == 0, n_lat - 1, n_lat - j)
    else:
        cidx = lambda j: jnp.where(j == 0, n_lat, j - 1)
        oidx = lambda j: jnp.maximum(j - 1, 0)
    const2 = lambda bb, j: (0, 0)
    const3 = lambda bb, j: (0, 0, 0)
    chunk = pl.BlockSpec((1, tile, d), lambda bb, j: (bb, cidx(j), 0))
    out_chunk = pl.BlockSpec((1, tile, d), lambda bb, j: (bb, oidx(j), 0))
    gate_specs = [pl.BlockSpec(wa_p.shape, const3), pl.BlockSpec(wx_p.shape, const3),
                  pl.BlockSpec((1, d), const2), pl.BlockSpec((1, d), const2), pl.BlockSpec((1, d), const2)]
    gate_args = [wa_p, wx_p, ba, bx, lam]
    scratch = [pltpu.VMEM((tile, d), F32), pltpu.VMEM((tile, d), F32)]
    if reverse:
        in_specs = [chunk] + gate_specs + [out_chunk, chunk]
        args = [z] + gate_args + [hf, gz]
        out_specs = out_chunk
        out_shape = jax.ShapeDtypeStruct((b, n_lat * tile, d), BF16)
        scratch += [pltpu.VMEM((tile, d), F32)]
    else:
        in_specs = [chunk,
                    pl.BlockSpec((1, SUBLANE, d), lambda bb, j: (bb, jnp.maximum(cidx(j) * per - 1, 0), 0)),
                    pl.BlockSpec((1, SUBLANE, d), lambda bb, j: (bb, jnp.minimum((cidx(j) + 1) * per, n_halo - 1), 0)),
                    pl.BlockSpec((CONV_W, d), const2), pl.BlockSpec((1, d), const2)] + gate_specs
        args = [z, z, z, conv[0], conv[1]] + gate_args
        out_specs = [out_chunk, chunk]
        out_shape = (jax.ShapeDtypeStruct((b, n_lat * tile, d), F32), jax.ShapeDtypeStruct((b, ltot, d), F32))
    scratch += [pltpu.VMEM((1, d), F32)]
    return pl.pallas_call(
        functools.partial(_rnn_kernel, d=d, tile=tile, n_lat=n_lat, reverse=reverse),
        grid=(b, n_lat + 1),
        in_specs=in_specs,
        out_specs=out_specs,
        out_shape=out_shape,
        scratch_shapes=scratch,
        compiler_params=_cparams(("arbitrary", "arbitrary")),
        name="rnn_bwd" if reverse else "rnn_fwd",
    )(*args)


def _attn_kernel(q_ref, k_ref, vt_ref, o_ref, s_even, s_odd, m_even, m_odd):
    g = pl.program_id(0)

    @pl.when(g == 0)
    def _():
        s_odd[...] = jnp.zeros_like(s_odd)
        m_odd[...] = jnp.zeros_like(m_odd)

    n_keys = k_ref.shape[1]
    tq = q_ref.shape[1]

    def step(s_new, m_new, s_prev, m_prev):
        heads = range(2)
        lanes = [slice(hh * HEAD_PAD, (hh + 1) * HEAD_PAD) for hh in heads]
        qs = [q_ref[0, :, lanes[hh]] for hh in heads]
        m_run = [None, None]
        acc = [jnp.zeros((HEAD_PAD, tq), F32) for _ in heads]
        for c in range(n_keys // MXU_COLS):
            rows = slice(c * MXU_COLS, (c + 1) * MXU_COLS)
            for hh in heads:
                st = lax.dot_general(k_ref[0, rows, lanes[hh]], qs[hh], (((1,), (1,)), ((), ())),
                                     preferred_element_type=F32)
                s_new[hh, rows, :] = st
                cm = jnp.max(st, axis=0, keepdims=True)
                m_run[hh] = cm if c == 0 else jnp.maximum(m_run[hh], cm)
                pt = jnp.exp2(s_prev[hh, rows, :] - m_prev[hh]).astype(BF16)
                acc[hh] = acc[hh] + jnp.dot(vt_ref[0, lanes[hh], rows], pt, preferred_element_type=F32)
        for hh in heads:
            m_new[hh] = m_run[hh]
        outs = [acc[hh][0:V_HEAD] / acc[hh][V_HEAD:V_HEAD + 1] for hh in heads]
        o_ref[0] = jnp.concatenate(outs, axis=0).T.astype(BF16)

    @pl.when(g % 2 == 0)
    def _():
        step(s_even, m_even, s_odd, m_odd)

    @pl.when(g % 2 == 1)
    def _():
        step(s_odd, m_odd, s_even, m_even)


def _attention(q, k, vt, s, tq):
    b, ltot, _ = q.shape
    pairs = MLA_HEADS // 2
    nq = s // tq
    n_items = b * pairs * nq

    def item(g):
        return g // (pairs * nq), (g // nq) % pairs, g % nq

    def cur(g):
        return item(jnp.minimum(g, n_items - 1))

    def prev(g):
        return item(jnp.maximum(g - 1, 0))

    scores = pltpu.VMEM((2, ltot, tq), F32)
    colmax = pltpu.VMEM((2, 1, tq), F32)
    return pl.pallas_call(
        _attn_kernel,
        grid=(n_items + 1,),
        in_specs=[pl.BlockSpec((1, tq, 2 * HEAD_PAD), lambda g: (cur(g)[0], cur(g)[2], cur(g)[1])),
                  pl.BlockSpec((1, ltot, 2 * HEAD_PAD), lambda g: (cur(g)[0], 0, cur(g)[1])),
                  pl.BlockSpec((1, 2 * HEAD_PAD, ltot), lambda g: (prev(g)[0], prev(g)[1], 0))],
        out_specs=pl.BlockSpec((1, tq, 2 * V_HEAD), lambda g: (prev(g)[0], prev(g)[2], prev(g)[1])),
        out_shape=jax.ShapeDtypeStruct((b, s, MLA_HEADS * V_HEAD), BF16),
        scratch_shapes=[scores, scores, colmax, colmax],
        compiler_params=_cparams(("arbitrary",)),
        name="attn",
    )(q, k, vt)


def _merge_kernel(rnn_ref, attn_ref, sg_ref, x_ref, mod_ref, g2_ref, wr_ref, wm_ref, wo_ref, wrt_ref,
                  xn_ref, h2_ref, aff_ref, *, d):
    gate_mod = mod_ref[0, :, 2 * d:3 * d]
    shift = mod_ref[0, :, 3 * d:4 * d]
    scale = mod_ref[0, :, 4 * d:5 * d]
    pr = jnp.dot(rnn_ref[0], wr_ref[...], preferred_element_type=F32)
    pa = jnp.dot(attn_ref[0], wm_ref[...], preferred_element_type=F32)
    merged = sg_ref[0, :, 0:d] * pr + sg_ref[0, :, d:2 * d] * pa
    xn = x_ref[0] + gate_mod * jnp.dot(merged.astype(BF16), wo_ref[...], preferred_element_type=F32)
    xn_ref[0] = xn
    inv = lax.rsqrt(jnp.mean(xn * xn, axis=-1, keepdims=True) + EPS)
    h2 = (xn * inv * g2_ref[...]) * (1.0 + scale) + shift
    h2_ref[0] = h2
    logits = jnp.dot(h2.astype(BF16), wrt_ref[...], preferred_element_type=F32)[:, 0:N_EXPERTS]
    e = jnp.exp(logits - jnp.max(logits, axis=-1, keepdims=True))
    aff_ref[0] = e / jnp.sum(e, axis=-1, keepdims=True)


def _merge(rnn, attn, sg, x, mod, g2, wr, wm, wo, wrt, tile):
    b, s, d = x.shape
    const = lambda bb, i: (0, 0)
    row = lambda bb, i: (bb, i, 0)
    return pl.pallas_call(
        functools.partial(_merge_kernel, d=d),
        grid=(b, s // tile),
        in_specs=[pl.BlockSpec((1, tile, d), row),
                  pl.BlockSpec((1, tile, attn.shape[2]), row),
                  pl.BlockSpec((1, tile, 2 * d), row),
                  pl.BlockSpec((1, tile, d), row),
                  pl.BlockSpec((1, 1, N_MOD * d), lambda bb, i: (2 * bb, 0, 0)),
                  pl.BlockSpec((1, d), const),
                  pl.BlockSpec(wr.shape, const),
                  pl.BlockSpec(wm.shape, const),
                  pl.BlockSpec(wo.shape, const),
                  pl.BlockSpec(wrt.shape, const)],
        out_specs=[pl.BlockSpec((1, tile, d), row),
                   pl.BlockSpec((1, tile, d), row),
                   pl.BlockSpec((1, tile, N_EXPERTS), row)],
        out_shape=(jax.ShapeDtypeStruct((b, s, d), F32),
                   jax.ShapeDtypeStruct((b, s, d), F32),
                   jax.ShapeDtypeStruct((b, s, N_EXPERTS), F32)),
        compiler_params=_cparams(("arbitrary", "arbitrary")),
        name="merge",
    )(rnn, attn, sg, x, mod, g2, wr, wm, wo, wrt)


def _excl_cumsum_lanes(x01, tri):
    rows, length = x01.shape
    offset = jnp.zeros((rows, 1), F32)
    pieces = []
    for blk in range(length // LANE):
        xb = x01[:, blk * LANE:(blk + 1) * LANE]
        inc = jnp.dot(xb.astype(BF16), tri, preferred_element_type=F32)
        pieces.append(inc - xb + offset)
        offset = offset + jnp.sum(xb, axis=-1, keepdims=True)
    return jnp.concatenate(pieces, axis=-1)


def _topk_kernel(aff_ref, idx_ref, gate_ref, pos_s, sel_s, *, cap):
    a = aff_ref[0]
    n_e, length = a.shape
    keys = lax.bitcast_convert_type(a, jnp.int32)

    def bisect(i, cand):
        t = cand | (jnp.int32(1) << (30 - i))
        cnt = jnp.sum((keys >= t).astype(jnp.int32), axis=-1, keepdims=True)
        return jnp.where(cnt >= cap, t, cand)

    thr = lax.fori_loop(0, 31, bisect, jnp.zeros((n_e, 1), jnp.int32))
    gt = (keys > thr).astype(F32)
    eq = (keys == thr).astype(F32)
    need = cap - jnp.sum(gt, axis=-1, keepdims=True)
    r_i = lax.broadcasted_iota(jnp.int32, (LANE, LANE), 0)
    c_i = lax.broadcasted_iota(jnp.int32, (LANE, LANE), 1)
    tri = (r_i <= c_i).astype(BF16)
    tie_rank = _excl_cumsum_lanes(eq, tri)
    sel = gt + eq * (tie_rank < need).astype(F32)
    pos_s[...] = _excl_cumsum_lanes(sel, tri)
    sel_s[...] = sel

    n_hi = idx_ref.shape[2]
    tok = lax.broadcasted_iota(jnp.int32, (1, length), 1)
    tok_hi = (tok >> 6).astype(F32)
    tok_lo = (tok & 63).astype(F32)
    a_iota = lax.broadcasted_iota(jnp.int32, (n_hi, 1), 0)
    c_iota = lax.broadcasted_iota(jnp.int32, (SLOT_LO, 1), 0)

    def compact(e, carry):
        pos_row = pos_s[pl.ds(e, 1), :].astype(jnp.int32)
        sel_row = sel_s[pl.ds(e, 1), :] > 0.5
        hot_a = jnp.where(jnp.logical_and((pos_row // SLOT_LO) == a_iota, sel_row), 1.0, 0.0)
        hot_c = jnp.where((pos_row % SLOT_LO) == c_iota, 1.0, 0.0).astype(BF16)
        g = aff_ref[0, pl.ds(e, 1), :]
        g1 = g.astype(BF16).astype(F32)
        g2 = (g - g1).astype(BF16).astype(F32)
        g3 = g - g1 - g2
        lhs = jnp.concatenate([hot_a * v for v in (tok_hi, tok_lo, g1, g2, g3)], axis=0).astype(BF16)
        res = lax.dot_general(lhs, hot_c, (((1,), (1,)), ((), ())), preferred_element_type=F32)
        part = [res[v * n_hi:(v + 1) * n_hi] for v in range(5)]
        idx_ref[0, e] = (part[0] * 64.0 + part[1]).astype(jnp.int32)
        gate_ref[0, e] = part[2] + part[3] + part[4]
        return carry

    lax.fori_loop(0, n_e, compact, 0)


def _topk(aff_t, cap):
    b, n_e, length = aff_t.shape
    n_hi = max(cap // SLOT_LO, SUBLANE)
    blk = lambda bb: (bb, 0, 0)
    blk4 = lambda bb: (bb, 0, 0, 0)
    idx, gate = pl.pallas_call(
        functools.partial(_topk_kernel, cap=cap),
        grid=(b,),
        in_specs=[pl.BlockSpec((1, n_e, length), blk)],
        out_specs=[pl.BlockSpec((1, n_e, n_hi, SLOT_LO), blk4), pl.BlockSpec((1, n_e, n_hi, SLOT_LO), blk4)],
        out_shape=(jax.ShapeDtypeStruct((b, n_e, n_hi, SLOT_LO), jnp.int32),
                   jax.ShapeDtypeStruct((b, n_e, n_hi, SLOT_LO), F32)),
        scratch_shapes=[pltpu.VMEM((n_e, length), F32), pltpu.VMEM((n_e, length), F32)],
        compiler_params=_cparams(("arbitrary",)),
        name="topk",
    )(aff_t)
    flat = lambda a: a.reshape(b * n_e, 1, n_hi * SLOT_LO)[:, :, :cap]
    return flat(idx), flat(gate)


def _gather_kernel(idx_ref, h_ref, o_ref, *, cap):
    def body(j, carry):
        r = idx_ref[0, 0, j]
        o_ref[0, 0, pl.ds(j, 1), :] = h_ref[0, pl.ds(r, 1), :]
        return carry

    lax.fori_loop(0, cap, body, 0, unroll=8)


def _gather(idx, h2, cap):
    b, s, d = h2.shape
    return pl.pallas_call(
        functools.partial(_gather_kernel, cap=cap),
        grid=(b, N_EXPERTS),
        in_specs=[pl.BlockSpec((1, 1, cap), lambda bb, e: (bb * N_EXPERTS + e, 0, 0), memory_space=pltpu.SMEM),
                  pl.BlockSpec((1, s, d), lambda bb, e: (bb, 0, 0))],
        out_specs=pl.BlockSpec((1, 1, cap, d), lambda bb, e: (bb, e, 0, 0)),
        out_shape=jax.ShapeDtypeStruct((b, N_EXPERTS, cap, d), F32),
        compiler_params=_cparams(("arbitrary", "arbitrary")),
        name="gather",
    )(idx, h2)


def _moe_kernel(idx_ref, gate_ref, x_ref, wg_ref, wu_ref, wd_ref, mod_ref, res_ref, o_ref, y_even, y_odd, *, d, cap):
    g = pl.program_id(0)
    e_prev = jnp.maximum(g - 1, 0) % N_EXPERTS
    piece = res_ref.shape[1]

    @pl.when(g == 0)
    def _():
        y_odd[...] = jnp.zeros_like(y_odd)

    @pl.when(e_prev == 0)
    def _():
        o_ref[...] = jnp.zeros_like(o_ref)

    def step(y_new, y_prev):
        xb = x_ref[0, 0].astype(BF16)
        gg = jnp.dot(xb, wg_ref[0], preferred_element_type=F32)
        uu = jnp.dot(xb, wu_ref[0], preferred_element_type=F32)
        hid = (gg * jax.nn.sigmoid(gg) * uu).astype(BF16)
        y_new[...] = jnp.dot(hid, wd_ref[0], preferred_element_type=F32) * mod_ref[0, :, 5 * d:6 * d]

        rows = pl.ds(pl.multiple_of(e_prev * piece, piece), piece)
        o_ref[0, rows, :] = o_ref[0, rows, :] + res_ref[0]
        for j in range(cap):
            r = idx_ref[0, 0, j]
            o_ref[0, pl.ds(r, 1), :] = o_ref[0, pl.ds(r, 1), :] + gate_ref[0, 0, j] * y_prev[j:j + 1, :]

    @pl.when(g % 2 == 0)
    def _():
        step(y_even, y_odd)

    @pl.when(g % 2 == 1)
    def _():
        step(y_odd, y_even)


def _moe(idx, gate, xg, w_gate, w_up, w_down, mod, x_new):
    b, n_e, cap, d = xg.shape
    s = x_new.shape[1]
    f = w_gate.shape[2]
    n_items = b * n_e
    cur = lambda g: jnp.minimum(g, n_items - 1)
    prev = lambda g: jnp.maximum(g - 1, 0)
    smem = lambda: pl.BlockSpec((1, 1, cap), lambda g: (prev(g), 0, 0), memory_space=pltpu.SMEM)
    wspec = lambda shape: pl.BlockSpec(shape, lambda g: (cur(g) % n_e, 0, 0))
    return pl.pallas_call(
        functools.partial(_moe_kernel, d=d, cap=cap),
        grid=(n_items + 1,),
        in_specs=[smem(), smem(),
                  pl.BlockSpec((1, 1, cap, d), lambda g: (cur(g) // n_e, cur(g) % n_e, 0, 0)),
                  wspec((1, d, f)), wspec((1, d, f)), wspec((1, f, d)),
                  pl.BlockSpec((1, 1, N_MOD * d), lambda g: (2 * (cur(g) // n_e), 0, 0)),
                  pl.BlockSpec((1, s // n_e, d), lambda g: (prev(g) // n_e, prev(g) % n_e, 0))],
        out_specs=pl.BlockSpec((1, s, d), lambda g: (prev(g) // n_e, 0, 0), pipeline_mode=pl.Buffered(1)),
        out_shape=jax.ShapeDtypeStruct((b, s, d), F32),
        scratch_shapes=[pltpu.VMEM((cap, d), F32), pltpu.VMEM((cap, d), F32)],
        compiler_params=_cparams(("arbitrary",)),
        name="moe",
    )(idx, gate, xg, w_gate, w_up, w_down, mod, x_new)


def _pad_lanes(a, width):
    return jnp.pad(a, [(0, 0)] * (a.ndim - 1) + [(0, width - a.shape[-1])])


def _rope_tables(s, lc, g_q_nope, g_q_rope, g_k_nope, g_k_rope):
    rows = s // GRID_W
    row = jnp.repeat(jnp.arange(rows, dtype=F32), GRID_W)
    col = jnp.tile(jnp.arange(GRID_W, dtype=F32), rows)
    n_pairs = QK_ROPE // 4
    inv_freq = ROPE_BASE ** (-jnp.arange(n_pairs, dtype=F32) / n_pairs)
    ang = jnp.concatenate([row[:, None] * inv_freq, col[:, None] * inv_freq], axis=-1)
    cos = jnp.concatenate([jnp.cos(ang), jnp.ones((lc, HALF_ROPE), F32)], axis=0)
    sin = jnp.concatenate([jnp.sin(ang), jnp.zeros((lc, HALF_ROPE), F32)], axis=0)
    ltot = s + lc
    ones = jnp.ones((ltot, 1), F32)

    def tables(g_nope, g_rope, scale):
        g1, g2 = g_rope[:HALF_ROPE], g_rope[HALF_ROPE:]
        tc = jnp.concatenate([ones * g_nope[None, :], cos * g1, cos * g2], axis=-1) * scale
        ts = jnp.concatenate([jnp.zeros((ltot, QK_NOPE), F32), -sin * g2, sin * g1], axis=-1) * scale
        return _pad_lanes(tc, LANE), _pad_lanes(ts, LANE)

    tqc, tqs = tables(g_q_nope, g_q_rope, ATTN_SCALE * LOG2_E)
    tkc, tks = tables(jnp.zeros((QK_NOPE,), F32), g_k_rope, 1.0)
    gkn = _pad_lanes(g_k_nope[None, :], LANE)
    return tqc, tqs, tkc, tks, gkn


def _swap_halves(a):
    return jnp.concatenate([a[..., HALF_ROPE:], a[..., :HALF_ROPE]], axis=-1)


def _layout_weights(w_in, w_uq, w_uk, w_uv, d):
    o_q = 2 * d
    o_kv = o_q + Q_LORA
    o_kr = o_kv + KV_LORA
    o_g = o_kr + QK_ROPE
    kr = w_in[:, o_kr:o_g]
    zl = jnp.zeros((w_in.shape[0], QK_NOPE), F32)
    kr_a = _pad_lanes(jnp.concatenate([zl, kr], axis=-1), LANE)
    kr_s = _pad_lanes(jnp.concatenate([zl, _swap_halves(kr)], axis=-1), LANE)
    win_p = jnp.concatenate([w_in[:, :o_kr], kr_a, kr_s, w_in[:, o_g:]], axis=-1).astype(BF16)

    uq = w_uq.reshape(Q_LORA, MLA_HEADS, QK_NOPE + QK_ROPE)
    nope, rope = uq[..., :QK_NOPE], uq[..., QK_NOPE:]
    wuq_p = _pad_lanes(jnp.concatenate([nope, rope], axis=-1), HEAD_PAD).reshape(Q_LORA, -1).astype(BF16)
    wuq_s = _pad_lanes(jnp.concatenate([jnp.zeros_like(nope), _swap_halves(rope)], axis=-1),
                       HEAD_PAD).reshape(Q_LORA, -1).astype(BF16)
    wuk_p = _pad_lanes(w_uk.reshape(KV_LORA, MLA_HEADS, QK_NOPE), HEAD_PAD).reshape(KV_LORA, -1).astype(BF16)
    wuv_p = _pad_lanes(w_uv.reshape(KV_LORA, MLA_HEADS, V_HEAD), HEAD_PAD).reshape(KV_LORA, -1).astype(BF16)
    return win_p, wuq_p, wuq_s, wuk_p, wuv_p


def _pair_blocks(w):
    nb, n, _ = w.shape
    w = w.reshape(nb // 2, 2, n, n)
    z = jnp.zeros((nb // 2, n, n), w.dtype)
    top = jnp.concatenate([w[:, 0], z], axis=-1)
    bot = jnp.concatenate([z, w[:, 1]], axis=-1)
    return jnp.concatenate([top, bot], axis=-2).astype(BF16)


def kernel(x, c, ctx, c_ctx, w_ada, b_ada, g_norm1, g_norm2, w_in, conv_w, conv_b, rg_wa, rg_ba, rg_wx, rg_bx,
           rg_lambda, w_rnn_out, g_q_lora, w_uq, g_kv_lora, w_uk, w_uv, g_q_nope, g_q_rope, g_k_nope, g_k_rope,
           w_mla_out, w_o, w_router, w_e_gate, w_e_up, w_e_down):
    assert w_ada.shape[0] == 1, "single-layer kernel"
    b, s, d = x.shape
    lc = ctx.shape[1]
    tile = lc
    n_lat = s // tile
    cap = CAPACITY_FACTOR * s // N_EXPERTS

    pad_rows = (-(b + 1)) % SUBLANE
    cc = jnp.concatenate([c, c_ctx[None, :], jnp.zeros((pad_rows, d), F32)], axis=0)
    mod_all = _ada(cc, w_ada[0], b_ada[0])
    mod = jnp.stack([mod_all[:b], jnp.broadcast_to(mod_all[b], (b, N_MOD * d))], axis=1).reshape(2 * b, 1, N_MOD * d)

    win_p, wuq_p, wuq_s, wuk_p, wuv_p = _layout_weights(w_in[0], w_uq[0], w_uk[0], w_uv[0], d)
    tqc, tqs, tkc, tks, gkn = _rope_tables(s, lc, g_q_nope[0], g_q_rope[0], g_k_nope[0], g_k_rope[0])
    z0, gz, q, k, v, sg = _inproj(x, ctx, mod, g_norm1, win_p, g_q_lora, wuq_p, wuq_s, g_kv_lora, wuk_p,
                                  wuv_p, gkn, tqc, tqs, tkc, tks, tile)

    h_f, xr = _rnn(z0, _pair_blocks(rg_wa[0, 0]), _pair_blocks(rg_wx[0, 0]), rg_ba[0, 0:1], rg_bx[0, 0:1],
                   rg_lambda[0, 0:1], tile, n_lat, False, conv=(conv_w[0], conv_b))
    rnn = _rnn(xr, _pair_blocks(rg_wa[0, 1]), _pair_blocks(rg_wx[0, 1]), rg_ba[0, 1:2], rg_bx[0, 1:2],
               rg_lambda[0, 1:2], tile, n_lat, True, hf=h_f, gz=gz)

    attn = _attention(q, k, v, s, MXU_COLS)

    x_new, h2, aff = _merge(rnn, attn, sg, x, mod, g_norm2, w_rnn_out[0].astype(BF16), w_mla_out[0].astype(BF16),
                            w_o[0].astype(BF16), _pad_lanes(w_router[0], LANE).astype(BF16), 2 * tile)

    idx, gate = _topk(jnp.swapaxes(aff, 1, 2), cap)
    xg = _gather(idx, h2, cap)
    return _moe(idx, gate, xg, w_e_gate[0].astype(BF16), w_e_up[0].astype(BF16), w_e_down[0].astype(BF16), mod, x_new)
```

```python
import functools

import jax
import jax.numpy as jnp
from jax import lax
from jax.experimental import pallas as pl
from jax.experimental.pallas import tpu as pltpu

EPS = 1e-6
N_MOD = 6
GRID_W = 64
RG_BLOCKS = 8
CONV_W = 4
CONV_PAD_L = 2
RG_C = 8.0
MLA_HEADS = 8
QK_NOPE = 64
QK_ROPE = 32
V_HEAD = 64
Q_LORA = 256
KV_LORA = 128
ROPE_BASE = 10000.0
ATTN_SCALE = (QK_NOPE + QK_ROPE) ** -0.5
LOG2_E = 1.4426950408889634
N_EXPERTS = 16
CAPACITY_FACTOR = 2

LANE = 128
SUBLANE = 8
MXU_COLS = 256
HEAD_PAD = LANE
HALF_ROPE = QK_ROPE // 2
RNN_SUB = 64
SLOT_LO = 32
VMEM_LIMIT = 56 * 1024 * 1024

BF16 = jnp.bfloat16
F32 = jnp.float32


def _cparams(sem):
    return pltpu.CompilerParams(dimension_semantics=sem, vmem_limit_bytes=VMEM_LIMIT)


def _ada_kernel(c_ref, w_ref, b_ref, o_ref):
    cc = c_ref[...]
    s = cc * jax.nn.sigmoid(cc)
    o_ref[...] = jnp.dot(s, w_ref[...], preferred_element_type=F32) + b_ref[...]


def _ada(cc, w_ada, b_ada):
    rows, d = cc.shape
    n = w_ada.shape[1]
    bn = d
    return pl.pallas_call(
        _ada_kernel,
        grid=(n // bn,),
        in_specs=[pl.BlockSpec((rows, d), lambda j: (0, 0)),
                  pl.BlockSpec((d, bn), lambda j: (0, j)),
                  pl.BlockSpec((1, bn), lambda j: (0, j))],
        out_specs=pl.BlockSpec((rows, bn), lambda j: (0, j)),
        out_shape=jax.ShapeDtypeStruct((rows, n), F32),
        compiler_params=_cparams(("arbitrary",)),
        name="ada",
    )(cc, w_ada, b_ada.reshape(1, n))


def _inproj_kernel(x_ref, ctx_ref, mod_ref, g1_ref, win_ref, gq_ref, wuq_ref, wuqs_ref, gkv_ref, wuk_ref,
                   wuv_ref, gkn_ref, tqc_ref, tqs_ref, tkc_ref, tks_ref,
                   z0_ref, gz_ref, q_ref, k_ref, v_ref, sg_ref, *, d, n_lat_tiles):
    i = pl.program_id(1)
    is_ctx = i == n_lat_tiles
    xt = jnp.where(is_ctx, ctx_ref[0], x_ref[0])
    shift = mod_ref[0, :, 0:d]
    scale = mod_ref[0, :, d:2 * d]
    inv = lax.rsqrt(jnp.mean(xt * xt, axis=-1, keepdims=True) + EPS)
    xn = ((xt * inv * g1_ref[...]) * (1.0 + scale) + shift).astype(BF16)

    def seg(lo, hi):
        return jnp.dot(xn, win_ref[:, lo:hi], preferred_element_type=F32)

    o_q = 2 * d
    o_kv = o_q + Q_LORA
    o_kr = o_kv + KV_LORA
    o_g = o_kr + 2 * LANE
    zq = seg(o_q, o_kv)
    zkv = seg(o_kv, o_kr)
    kr_a = seg(o_kr, o_kr + LANE)
    kr_s = seg(o_kr + LANE, o_g)

    lane = lax.broadcasted_iota(jnp.int32, (1, LANE), 1)
    is_nope = lane < QK_NOPE
    is_rope = jnp.logical_and(lane >= QK_NOPE, lane < QK_NOPE + QK_ROPE)

    zqn = (zq * lax.rsqrt(jnp.mean(zq * zq, axis=-1, keepdims=True) + EPS) * gq_ref[...]).astype(BF16)
    qa = jnp.dot(zqn, wuq_ref[...], preferred_element_type=F32)
    qs = jnp.dot(zqn, wuqs_ref[...], preferred_element_type=F32)
    tqc = tqc_ref[...]
    tqs = tqs_ref[...]
    for h in range(MLA_HEADS):
        a = qa[:, h * HEAD_PAD:(h + 1) * HEAD_PAD]
        s = qs[:, h * HEAD_PAD:(h + 1) * HEAD_PAD]
        sq = a * a
        ssn = jnp.sum(jnp.where(is_nope, sq, 0.0), axis=-1, keepdims=True)
        ssr = jnp.sum(jnp.where(is_rope, sq, 0.0), axis=-1, keepdims=True)
        invh = jnp.where(is_nope, lax.rsqrt(ssn * (1.0 / QK_NOPE) + EPS), lax.rsqrt(ssr * (1.0 / QK_ROPE) + EPS))
        q_ref[0, :, h * HEAD_PAD:(h + 1) * HEAD_PAD] = (invh * (a * tqc + s * tqs)).astype(BF16)

    ckv = (zkv * lax.rsqrt(jnp.mean(zkv * zkv, axis=-1, keepdims=True) + EPS) * gkv_ref[...]).astype(BF16)
    lane_h = lax.broadcasted_iota(jnp.int32, (1, MLA_HEADS * HEAD_PAD), 1) % HEAD_PAD
    vp = jnp.dot(ckv, wuv_ref[...], preferred_element_type=F32) + jnp.where(lane_h == V_HEAD, 1.0, 0.0)
    v_ref[0] = vp.T.astype(BF16)
    ka = jnp.dot(ckv, wuk_ref[...], preferred_element_type=F32)
    inv_r = lax.rsqrt(jnp.sum(kr_a * kr_a, axis=-1, keepdims=True) * (1.0 / QK_ROPE) + EPS)
    kr = inv_r * (kr_a * tkc_ref[...] + kr_s * tks_ref[...])
    gkn = gkn_ref[...]
    for h in range(MLA_HEADS):
        a = ka[:, h * HEAD_PAD:(h + 1) * HEAD_PAD]
        invh = lax.rsqrt(jnp.sum(a * a, axis=-1, keepdims=True) * (1.0 / QK_NOPE) + EPS)
        k_ref[0, :, h * HEAD_PAD:(h + 1) * HEAD_PAD] = (a * invh * gkn + kr).astype(BF16)

    z0_ref[0] = seg(0, d)
    gz_ref[0] = jax.nn.gelu(seg(d, o_q))
    sg_ref[0] = jax.nn.sigmoid(seg(o_g, o_g + 2 * d))


def _inproj(x, ctx, mod, g1, win_p, gq, wuq_p, wuq_s, gkv, wuk_p, wuv, gkn, tqc, tqs, tkc, tks, tile):
    b, s, d = x.shape
    lc = ctx.shape[1]
    assert lc == tile and s % tile == 0
    nl = s // tile
    ltot = s + lc
    n_in = win_p.shape[1]
    hp = MLA_HEADS * HEAD_PAD
    const = lambda bb, i: (0, 0)
    row = lambda bb, i: (bb, i, 0)
    tab = lambda bb, i: (i, 0)
    out_shapes = (
        jax.ShapeDtypeStruct((b, ltot, d), F32),
        jax.ShapeDtypeStruct((b, ltot, d), F32),
        jax.ShapeDtypeStruct((b, ltot, hp), BF16),
        jax.ShapeDtypeStruct((b, ltot, hp), BF16),
        jax.ShapeDtypeStruct((b, hp, ltot), BF16),
        jax.ShapeDtypeStruct((b, ltot, 2 * d), F32),
    )
    return pl.pallas_call(
        functools.partial(_inproj_kernel, d=d, n_lat_tiles=nl),
        grid=(b, nl + 1),
        in_specs=[
            pl.BlockSpec((1, tile, d), lambda bb, i: (bb, jnp.minimum(i, nl - 1), 0)),
            pl.BlockSpec((1, tile, d), lambda bb, i: (bb, 0, 0)),
            pl.BlockSpec((1, 1, N_MOD * d), lambda bb, i: (2 * bb + i // nl, 0, 0)),
            pl.BlockSpec((1, d), const),
            pl.BlockSpec((d, n_in), const),
            pl.BlockSpec((1, Q_LORA), const),
            pl.BlockSpec((Q_LORA, hp), const),
            pl.BlockSpec((Q_LORA, hp), const),
            pl.BlockSpec((1, KV_LORA), const),
            pl.BlockSpec((KV_LORA, hp), const),
            pl.BlockSpec((KV_LORA, hp), const),
            pl.BlockSpec((1, LANE), const),
            pl.BlockSpec((tile, LANE), tab),
            pl.BlockSpec((tile, LANE), tab),
            pl.BlockSpec((tile, LANE), tab),
            pl.BlockSpec((tile, LANE), tab),
        ],
        out_specs=[
            pl.BlockSpec((1, tile, d), row),
            pl.BlockSpec((1, tile, d), row),
            pl.BlockSpec((1, tile, hp), row),
            pl.BlockSpec((1, tile, hp), row),
            pl.BlockSpec((1, hp, tile), lambda bb, i: (bb, 0, i)),
            pl.BlockSpec((1, tile, 2 * d), row),
        ],
        out_shape=out_shapes,
        compiler_params=_cparams(("arbitrary", "arbitrary")),
        name="inproj",
    )(x, ctx, mod, g1, win_p, gq, wuq_p, wuq_s, gkv, wuk_p, wuv, gkn, tqc, tqs, tkc, tks)


def _sigmoid(x):
    return 0.5 * jnp.tanh(0.5 * x) + 0.5


def _softplus(x):
    e = jnp.exp(-jnp.abs(x))
    u = 1.0 + e
    tiny = u == 1.0
    log1p_e = jnp.where(tiny, e, jnp.log(u) * (e / jnp.where(tiny, 1.0, u - 1.0)))
    return jnp.maximum(x, 0.0) + log1p_e


def _rnn_kernel(*refs, d, tile, n_lat, reverse):
    if reverse:
        (xr_ref, wa_ref, wx_ref, ba_ref, bx_ref, lam_ref, hf_ref, gz_ref, o_ref, a_s, b_s, hb_s, h_s) = refs
    else:
        (z_ref, zp_ref, zn_ref, cw_ref, cb_ref, wa_ref, wx_ref, ba_ref, bx_ref, lam_ref,
         o_ref, xr_ref, a_s, b_s, h_s) = refs
    j = pl.program_id(1)
    if not reverse:
        c = jnp.where(j == 0, n_lat, j - 1)
        prev_zero = jnp.logical_or(c == 0, c == n_lat)
        next_zero = c >= n_lat - 1
        ext = jnp.concatenate([jnp.where(prev_zero, 0.0, zp_ref[0]), z_ref[0], jnp.where(next_zero, 0.0, zn_ref[0])],
                              axis=0)
        n_ext = tile + 2 * SUBLANE
        taps = [ext if (CONV_PAD_L - k) % n_ext == 0 else pltpu.roll(ext, (CONV_PAD_L - k) % n_ext, 0)
                for k in range(CONV_W)]
    pair = 2 * (d // RG_BLOCKS)
    neg_c_softplus = -RG_C * _softplus(-lam_ref[...])
    dst = hb_s if reverse else o_ref.at[0]

    h = jnp.where(j == 0, 0.0, h_s[...])
    subs = range(tile // RNN_SUB)
    for sb in (reversed(subs) if reverse else subs):
        r0 = sb * RNN_SUB
        if reverse:
            xr = xr_ref[0, r0:r0 + RNN_SUB, :]
        else:
            xr = cb_ref[...]
            for k in range(CONV_W):
                xr = xr + taps[k][SUBLANE + r0:SUBLANE + r0 + RNN_SUB] * cw_ref[k:k + 1, :]
            xr_ref[0, r0:r0 + RNN_SUB, :] = xr
        xb = xr.astype(BF16)
        ya = jnp.concatenate([jnp.dot(xb[:, m * pair:(m + 1) * pair], wa_ref[m], preferred_element_type=F32)
                              for m in range(d // pair)], axis=-1)
        yx = jnp.concatenate([jnp.dot(xb[:, m * pair:(m + 1) * pair], wx_ref[m], preferred_element_type=F32)
                              for m in range(d // pair)], axis=-1)
        r = _sigmoid(ya + ba_ref[...])
        gi = _sigmoid(yx + bx_ref[...])
        log_a = neg_c_softplus * r
        a_s[r0:r0 + RNN_SUB, :] = jnp.exp(log_a)
        th = jnp.tanh(log_a)
        v = -2.0 * th / (1.0 - th)
        b_s[r0:r0 + RNN_SUB, :] = (v * lax.rsqrt(jnp.maximum(v, 1e-30))) * (gi * xr)
        rows = range(r0, r0 + RNN_SUB)
        for t in (reversed(rows) if reverse else rows):
            h = a_s[t:t + 1, :] * h + b_s[t:t + 1, :]
            dst[t:t + 1, :] = h
    h_s[...] = h
    if reverse:
        o_ref[0] = ((hf_ref[0] + hb_s[...]) * gz_ref[0]).astype(BF16)


def _rnn(z, wa_p, wx_p, ba, bx, lam, tile, n_lat, reverse, conv=None, hf=None, gz=None):
    b, ltot, d = z.shape
    n_halo = ltot // SUBLANE
    per = tile // SUBLANE
    if reverse:
        cidx = lambda j: jnp.where(j == 0, n_lat, n_lat - j)
        oidx = lambda j: jnp.where(j == 0, n_lat - 1, n_lat - j)
    else:
        cidx = lambda j: jnp.where(j == 0, n_lat, j - 1)
        oidx = lambda j: jnp.maximum(j - 1, 0)
    const2 = lambda bb, j: (0, 0)
    const3 = lambda bb, j: (0, 0, 0)
    chunk = pl.BlockSpec((1, tile, d), lambda bb, j: (bb, cidx(j), 0))
    out_chunk = pl.BlockSpec((1, tile, d), lambda bb, j: (bb, oidx(j), 0))
    gate_specs = [pl.BlockSpec(wa_p.shape, const3), pl.BlockSpec(wx_p.shape, const3),
                  pl.BlockSpec((1, d), const2), pl.BlockSpec((1, d), const2), pl.BlockSpec((1, d), const2)]
    gate_args = [wa_p, wx_p, ba, bx, lam]
    scratch = [pltpu.VMEM((tile, d), F32), pltpu.VMEM((tile, d), F32)]
    if reverse:
        in_specs = [chunk] + gate_specs + [out_chunk, chunk]
        args = [z] + gate_args + [hf, gz]
        out_specs = out_chunk
        out_shape = jax.ShapeDtypeStruct((b, n_lat * tile, d), BF16)
        scratch += [pltpu.VMEM((tile, d), F32)]
    else:
        in_specs = [chunk,
                    pl.BlockSpec((1, SUBLANE, d), lambda bb, j: (bb, jnp.maximum(cidx(j) * per - 1, 0), 0)),
                    pl.BlockSpec((1, SUBLANE, d), lambda bb, j: (bb, jnp.minimum((cidx(j) + 1) * per, n_halo - 1), 0)),
                    pl.BlockSpec((CONV_W, d), const2), pl.BlockSpec((1, d), const2)] + gate_specs
        args = [z, z, z, conv[0], conv[1]] + gate_args
        out_specs = [out_chunk, chunk]
        out_shape = (jax.ShapeDtypeStruct((b, n_lat * tile, d), F32), jax.ShapeDtypeStruct((b, ltot, d), F32))
    scratch += [pltpu.VMEM((1, d), F32)]
    return pl.pallas_call(
        functools.partial(_rnn_kernel, d=d, tile=tile, n_lat=n_lat, reverse=reverse),
        grid=(b, n_lat + 1),
        in_specs=in_specs,
        out_specs=out_specs,
        out_shape=out_shape,
        scratch_shapes=scratch,
        compiler_params=_cparams(("arbitrary", "arbitrary")),
        name="rnn_bwd" if reverse else "rnn_fwd",
    )(*args)


def _attn_kernel(q_ref, k_ref, vt_ref, o_ref, s_even, s_odd, m_even, m_odd):
    g = pl.program_id(0)

    @pl.when(g == 0)
    def _():
        s_odd[...] = jnp.zeros_like(s_odd)
        m_odd[...] = jnp.zeros_like(m_odd)

    n_keys = k_ref.shape[1]
    tq = q_ref.shape[1]

    def step(s_new, m_new, s_prev, m_prev):
        heads = range(2)
        lanes = [slice(hh * HEAD_PAD, (hh + 1) * HEAD_PAD) for hh in heads]
        qs = [q_ref[0, :, lanes[hh]] for hh in heads]
        m_run = [None, None]
        acc = [jnp.zeros((HEAD_PAD, tq), F32) for _ in heads]
        for c in range(n_keys // MXU_COLS):
            rows = slice(c * MXU_COLS, (c + 1) * MXU_COLS)
            for hh in heads:
                st = lax.dot_general(k_ref[0, rows, lanes[hh]], qs[hh], (((1,), (1,)), ((), ())),
                                     preferred_element_type=F32)
                s_new[hh, rows, :] = st
                cm = jnp.max(st, axis=0, keepdims=True)
                m_run[hh] = cm if c == 0 else jnp.maximum(m_run[hh], cm)
                pt = jnp.exp2(s_prev[hh, rows, :] - m_prev[hh]).astype(BF16)
                acc[hh] = acc[hh] + jnp.dot(vt_ref[0, lanes[hh], rows], pt, preferred_element_type=F32)
        for hh in heads:
            m_new[hh] = m_run[hh]
        outs = [acc[hh][0:V_HEAD] / acc[hh][V_HEAD:V_HEAD + 1] for hh in heads]
        o_ref[0] = jnp.concatenate(outs, axis=0).T.astype(BF16)

    @pl.when(g % 2 == 0)
    def _():
        step(s_even, m_even, s_odd, m_odd)

    @pl.when(g % 2 == 1)
    def _():
        step(s_odd, m_odd, s_even, m_even)


def _attention(q, k, vt, s, tq):
    b, ltot, _ = q.shape
    pairs = MLA_HEADS // 2
    nq = s // tq
    n_items = b * pairs * nq

    def item(g):
        return g // (pairs * nq), (g // nq) % pairs, g % nq

    def cur(g):
        return item(jnp.minimum(g, n_items - 1))

    def prev(g):
        return item(jnp.maximum(g - 1, 0))

    scores = pltpu.VMEM((2, ltot, tq), F32)
    colmax = pltpu.VMEM((2, 1, tq), F32)
    return pl.pallas_call(
        _attn_kernel,
        grid=(n_items + 1,),
        in_specs=[pl.BlockSpec((1, tq, 2 * HEAD_PAD), lambda g: (cur(g)[0], cur(g)[2], cur(g)[1])),
                  pl.BlockSpec((1, ltot, 2 * HEAD_PAD), lambda g: (cur(g)[0], 0, cur(g)[1])),
                  pl.BlockSpec((1, 2 * HEAD_PAD, ltot), lambda g: (prev(g)[0], prev(g)[1], 0))],
        out_specs=pl.BlockSpec((1, tq, 2 * V_HEAD), lambda g: (prev(g)[0], prev(g)[2], prev(g)[1])),
        out_shape=jax.ShapeDtypeStruct((b, s, MLA_HEADS * V_HEAD), BF16),
        scratch_shapes=[scores, scores, colmax, colmax],
        compiler_params=_cparams(("arbitrary",)),
        name="attn",
    )(q, k, vt)


def _merge_kernel(rnn_ref, attn_ref, sg_ref, x_ref, mod_ref, g2_ref, wr_ref, wm_ref, wo_ref, wrt_ref,
                  xn_ref, h2_ref, aff_ref, *, d):
    gate_mod = mod_ref[0, :, 2 * d:3 * d]
    shift = mod_ref[0, :, 3 * d:4 * d]
    scale = mod_ref[0, :, 4 * d:5 * d]
    pr = jnp.dot(rnn_ref[0], wr_ref[...], preferred_element_type=F32)
    pa = jnp.dot(attn_ref[0], wm_ref[...], preferred_element_type=F32)
    merged = sg_ref[0, :, 0:d] * pr + sg_ref[0, :, d:2 * d] * pa
    xn = x_ref[0] + gate_mod * jnp.dot(merged.astype(BF16), wo_ref[...], preferred_element_type=F32)
    xn_ref[0] = xn
    inv = lax.rsqrt(jnp.mean(xn * xn, axis=-1, keepdims=True) + EPS)
    h2 = (xn * inv * g2_ref[...]) * (1.0 + scale) + shift
    n_tok = h2.shape[0]
    for t in range(d // LANE):
        h2_ref[0, pl.ds(t, n_tok, stride=d // LANE), :] = h2[:, t * LANE:(t + 1) * LANE]
    logits = jnp.dot(h2.astype(BF16), wrt_ref[...], preferred_element_type=F32)[:, 0:N_EXPERTS]
    e = jnp.exp(logits - jnp.max(logits, axis=-1, keepdims=True))
    aff_ref[0] = e / jnp.sum(e, axis=-1, keepdims=True)


def _merge(rnn, attn, sg, x, mod, g2, wr, wm, wo, wrt, tile):
    b, s, d = x.shape
    const = lambda bb, i: (0, 0)
    row = lambda bb, i: (bb, i, 0)
    return pl.pallas_call(
        functools.partial(_merge_kernel, d=d),
        grid=(b, s // tile),
        in_specs=[pl.BlockSpec((1, tile, d), row),
                  pl.BlockSpec((1, tile, attn.shape[2]), row),
                  pl.BlockSpec((1, tile, 2 * d), row),
                  pl.BlockSpec((1, tile, d), row),
                  pl.BlockSpec((1, 1, N_MOD * d), lambda bb, i: (2 * bb, 0, 0)),
                  pl.BlockSpec((1, d), const),
                  pl.BlockSpec(wr.shape, const),
                  pl.BlockSpec(wm.shape, const),
                  pl.BlockSpec(wo.shape, const),
                  pl.BlockSpec(wrt.shape, const)],
        out_specs=[pl.BlockSpec((1, tile, d), row),
                   pl.BlockSpec((1, tile * (d // LANE), LANE), row),
                   pl.BlockSpec((1, tile, N_EXPERTS), row)],
        out_shape=(jax.ShapeDtypeStruct((b, s, d), F32),
                   jax.ShapeDtypeStruct((b, s * (d // LANE), LANE), F32),
                   jax.ShapeDtypeStruct((b, s, N_EXPERTS), F32)),
        compiler_params=_cparams(("arbitrary", "arbitrary")),
        name="merge",
    )(rnn, attn, sg, x, mod, g2, wr, wm, wo, wrt)


def _excl_cumsum_lanes(x01, tri):
    rows, length = x01.shape
    offset = jnp.zeros((rows, 1), F32)
    pieces = []
    for blk in range(length // LANE):
        xb = x01[:, blk * LANE:(blk + 1) * LANE]
        inc = jnp.dot(xb.astype(BF16), tri, preferred_element_type=F32)
        pieces.append(inc - xb + offset)
        offset = offset + jnp.sum(xb, axis=-1, keepdims=True)
    return jnp.concatenate(pieces, axis=-1)


def _topk_kernel(aff_ref, idx_ref, gate_ref, pos_s, sel_s, *, cap):
    a = aff_ref[0]
    n_e, length = a.shape
    keys = lax.bitcast_convert_type(a, jnp.int32)

    def bisect(i, cand):
        t = cand | (jnp.int32(1) << (30 - i))
        cnt = jnp.sum((keys >= t).astype(jnp.int32), axis=-1, keepdims=True)
        return jnp.where(cnt >= cap, t, cand)

    thr = lax.fori_loop(0, 31, bisect, jnp.zeros((n_e, 1), jnp.int32))
    gt = (keys > thr).astype(F32)
    eq = (keys == thr).astype(F32)
    need = cap - jnp.sum(gt, axis=-1, keepdims=True)
    r_i = lax.broadcasted_iota(jnp.int32, (LANE, LANE), 0)
    c_i = lax.broadcasted_iota(jnp.int32, (LANE, LANE), 1)
    tri = (r_i <= c_i).astype(BF16)
    tie_rank = _excl_cumsum_lanes(eq, tri)
    sel = gt + eq * (tie_rank < need).astype(F32)
    pos_s[...] = _excl_cumsum_lanes(sel, tri)
    sel_s[...] = sel

    n_hi = idx_ref.shape[2]
    tok = lax.broadcasted_iota(jnp.int32, (1, length), 1)
    tok_hi = (tok >> 6).astype(F32)
    tok_lo = (tok & 63).astype(F32)
    a_iota = lax.broadcasted_iota(jnp.int32, (n_hi, 1), 0)
    c_iota = lax.broadcasted_iota(jnp.int32, (SLOT_LO, 1), 0)

    def compact(e, carry):
        pos_row = pos_s[pl.ds(e, 1), :].astype(jnp.int32)
        sel_row = sel_s[pl.ds(e, 1), :] > 0.5
        hot_a = jnp.where(jnp.logical_and((pos_row // SLOT_LO) == a_iota, sel_row), 1.0, 0.0)
        hot_c = jnp.where((pos_row % SLOT_LO) == c_iota, 1.0, 0.0).astype(BF16)
        g = aff_ref[0, pl.ds(e, 1), :]
        g1 = g.astype(BF16).astype(F32)
        g2 = (g - g1).astype(BF16).astype(F32)
        g3 = g - g1 - g2
        lhs = jnp.concatenate([hot_a * v for v in (tok_hi, tok_lo, g1, g2, g3)], axis=0).astype(BF16)
        res = lax.dot_general(lhs, hot_c, (((1,), (1,)), ((), ())), preferred_element_type=F32)
        part = [res[v * n_hi:(v + 1) * n_hi] for v in range(5)]
        idx_ref[0, e] = (part[0] * 64.0 + part[1]).astype(jnp.int32)
        gate_ref[0, e] = part[2] + part[3] + part[4]
        return carry

    lax.fori_loop(0, n_e, compact, 0)


def _topk(aff_t, cap):
    b, n_e, length = aff_t.shape
    n_hi = max(cap // SLOT_LO, SUBLANE)
    blk = lambda bb: (bb, 0, 0)
    blk4 = lambda bb: (bb, 0, 0, 0)
    idx, gate = pl.pallas_call(
        functools.partial(_topk_kernel, cap=cap),
        grid=(b,),
        in_specs=[pl.BlockSpec((1, n_e, length), blk)],
        out_specs=[pl.BlockSpec((1, n_e, n_hi, SLOT_LO), blk4), pl.BlockSpec((1, n_e, n_hi, SLOT_LO), blk4)],
        out_shape=(jax.ShapeDtypeStruct((b, n_e, n_hi, SLOT_LO), jnp.int32),
                   jax.ShapeDtypeStruct((b, n_e, n_hi, SLOT_LO), F32)),
        scratch_shapes=[pltpu.VMEM((n_e, length), F32), pltpu.VMEM((n_e, length), F32)],
        compiler_params=_cparams(("arbitrary",)),
        name="topk",
    )(aff_t)
    flat = lambda a: a.reshape(b * n_e, 1, n_hi * SLOT_LO)[:, :, :cap]
    return flat(idx), flat(gate)


def _gather_kernel(idx_ref, h_ref, o_ref, *, cap):
    def body(j, carry):
        src = pl.multiple_of(idx_ref[0, 0, j] * SUBLANE, SUBLANE)
        dst = pl.multiple_of(j * SUBLANE, SUBLANE)
        o_ref[0, 0, pl.ds(dst, SUBLANE), :] = h_ref[0, pl.ds(src, SUBLANE), :]
        return carry

    lax.fori_loop(0, cap, body, 0, unroll=8)


def _gather(idx, h2t, cap):
    b, rows, lanes = h2t.shape
    return pl.pallas_call(
        functools.partial(_gather_kernel, cap=cap),
        grid=(b, N_EXPERTS),
        in_specs=[pl.BlockSpec((1, 1, cap), lambda bb, e: (bb * N_EXPERTS + e, 0, 0), memory_space=pltpu.SMEM),
                  pl.BlockSpec((1, rows, lanes), lambda bb, e: (bb, 0, 0))],
        out_specs=pl.BlockSpec((1, 1, cap * SUBLANE, lanes), lambda bb, e: (bb, e, 0, 0)),
        out_shape=jax.ShapeDtypeStruct((b, N_EXPERTS, cap * SUBLANE, lanes), F32),
        compiler_params=_cparams(("arbitrary", "arbitrary")),
        name="gather",
    )(idx, h2t)


def _moe_kernel(idx_ref, gate_ref, x_ref, wg_ref, wu_ref, wd_ref, mod_ref, res_ref, o_ref, y_even, y_odd, *, d, cap):
    g = pl.program_id(0)
    e_prev = jnp.maximum(g - 1, 0) % N_EXPERTS
    piece = res_ref.shape[1]

    @pl.when(g == 0)
    def _():
        y_odd[...] = jnp.zeros_like(y_odd)

    @pl.when(e_prev == 0)
    def _():
        o_ref[...] = jnp.zeros_like(o_ref)

    def step(y_new, y_prev):
        xb = jnp.concatenate([x_ref[0, 0, pl.ds(t, cap, stride=d // LANE), :] for t in range(d // LANE)],
                             axis=-1).astype(BF16)
        gg = jnp.dot(xb, wg_ref[0], preferred_element_type=F32)
        uu = jnp.dot(xb, wu_ref[0], preferred_element_type=F32)
        hid = (gg * jax.nn.sigmoid(gg) * uu).astype(BF16)
        y_new[...] = jnp.dot(hid, wd_ref[0], preferred_element_type=F32) * mod_ref[0, :, 5 * d:6 * d]

        rows = pl.ds(pl.multiple_of(e_prev * piece, piece), piece)
        o_ref[0, rows, :] = o_ref[0, rows, :] + res_ref[0]
        for j in range(cap):
            r = idx_ref[0, 0, j]
            o_ref[0, pl.ds(r, 1), :] = o_ref[0, pl.ds(r, 1), :] + gate_ref[0, 0, j] * y_prev[j:j + 1, :]

    @pl.when(g % 2 == 0)
    def _():
        step(y_even, y_odd)

    @pl.when(g % 2 == 1)
    def _():
        step(y_odd, y_even)


def _moe(idx, gate, xg, w_gate, w_up, w_down, mod, x_new):
    b, n_e, xrows, lanes = xg.shape
    _, s, d = x_new.shape
    cap = xrows * lanes // d
    f = w_gate.shape[2]
    n_items = b * n_e
    cur = lambda g: jnp.minimum(g, n_items - 1)
    prev = lambda g: jnp.maximum(g - 1, 0)
    smem = lambda: pl.BlockSpec((1, 1, cap), lambda g: (prev(g), 0, 0), memory_space=pltpu.SMEM)
    wspec = lambda shape: pl.BlockSpec(shape, lambda g: (cur(g) % n_e, 0, 0))
    return pl.pallas_call(
        functools.partial(_moe_kernel, d=d, cap=cap),
        grid=(n_items + 1,),
        in_specs=[smem(), smem(),
                  pl.BlockSpec((1, 1, xrows, lanes), lambda g: (cur(g) // n_e, cur(g) % n_e, 0, 0)),
                  wspec((1, d, f)), wspec((1, d, f)), wspec((1, f, d)),
                  pl.BlockSpec((1, 1, N_MOD * d), lambda g: (2 * (cur(g) // n_e), 0, 0)),
                  pl.BlockSpec((1, s // n_e, d), lambda g: (prev(g) // n_e, prev(g) % n_e, 0))],
        out_specs=pl.BlockSpec((1, s, d), lambda g: (prev(g) // n_e, 0, 0), pipeline_mode=pl.Buffered(1)),
        out_shape=jax.ShapeDtypeStruct((b, s, d), F32),
        scratch_shapes=[pltpu.VMEM((cap, d), F32), pltpu.VMEM((cap, d), F32)],
        compiler_params=_cparams(("arbitrary",)),
        name="moe",
    )(idx, gate, xg, w_gate, w_up, w_down, mod, x_new)


def _pad_lanes(a, width):
    return jnp.pad(a, [(0, 0)] * (a.ndim - 1) + [(0, width - a.shape[-1])])


def _rope_tables(s, lc, g_q_nope, g_q_rope, g_k_nope, g_k_rope):
    rows = s // GRID_W
    row = jnp.repeat(jnp.arange(rows, dtype=F32), GRID_W)
    col = jnp.tile(jnp.arange(GRID_W, dtype=F32), rows)
    n_pairs = QK_ROPE // 4
    inv_freq = ROPE_BASE ** (-jnp.arange(n_pairs, dtype=F32) / n_pairs)
    ang = jnp.concatenate([row[:, None] * inv_freq, col[:, None] * inv_freq], axis=-1)
    cos = jnp.concatenate([jnp.cos(ang), jnp.ones((lc, HALF_ROPE), F32)], axis=0)
    sin = jnp.concatenate([jnp.sin(ang), jnp.zeros((lc, HALF_ROPE), F32)], axis=0)
    ltot = s + lc
    ones = jnp.ones((ltot, 1), F32)

    def tables(g_nope, g_rope, scale):
        g1, g2 = g_rope[:HALF_ROPE], g_rope[HALF_ROPE:]
        tc = jnp.concatenate([ones * g_nope[None, :], cos * g1, cos * g2], axis=-1) * scale
        ts = jnp.concatenate([jnp.zeros((ltot, QK_NOPE), F32), -sin * g2, sin * g1], axis=-1) * scale
        return _pad_lanes(tc, LANE), _pad_lanes(ts, LANE)

    tqc, tqs = tables(g_q_nope, g_q_rope, ATTN_SCALE * LOG2_E)
    tkc, tks = tables(jnp.zeros((QK_NOPE,), F32), g_k_rope, 1.0)
    gkn = _pad_lanes(g_k_nope[None, :], LANE)
    return tqc, tqs, tkc, tks, gkn


def _swap_halves(a):
    return jnp.concatenate([a[..., HALF_ROPE:], a[..., :HALF_ROPE]], axis=-1)


def _layout_weights(w_in, w_uq, w_uk, w_uv, d):
    o_q = 2 * d
    o_kv = o_q + Q_LORA
    o_kr = o_kv + KV_LORA
    o_g = o_kr + QK_ROPE
    kr = w_in[:, o_kr:o_g]
    zl = jnp.zeros((w_in.shape[0], QK_NOPE), F32)
    kr_a = _pad_lanes(jnp.concatenate([zl, kr], axis=-1), LANE)
    kr_s = _pad_lanes(jnp.concatenate([zl, _swap_halves(kr)], axis=-1), LANE)
    win_p = jnp.concatenate([w_in[:, :o_kr], kr_a, kr_s, w_in[:, o_g:]], axis=-1).astype(BF16)

    uq = w_uq.reshape(Q_LORA, MLA_HEADS, QK_NOPE + QK_ROPE)
    nope, rope = uq[..., :QK_NOPE], uq[..., QK_NOPE:]
    wuq_p = _pad_lanes(jnp.concatenate([nope, rope], axis=-1), HEAD_PAD).reshape(Q_LORA, -1).astype(BF16)
    wuq_s = _pad_lanes(jnp.concatenate([jnp.zeros_like(nope), _swap_halves(rope)], axis=-1),
                       HEAD_PAD).reshape(Q_LORA, -1).astype(BF16)
    wuk_p = _pad_lanes(w_uk.reshape(KV_LORA, MLA_HEADS, QK_NOPE), HEAD_PAD).reshape(KV_LORA, -1).astype(BF16)
    wuv_p = _pad_lanes(w_uv.reshape(KV_LORA, MLA_HEADS, V_HEAD), HEAD_PAD).reshape(KV_LORA, -1).astype(BF16)
    return win_p, wuq_p, wuq_s, wuk_p, wuv_p


def _pair_blocks(w):
    nb, n, _ = w.shape
    w = w.reshape(nb // 2, 2, n, n)
    z = jnp.zeros((nb // 2, n, n), w.dtype)
    top = jnp.concatenate([w[:, 0], z], axis=-1)
    bot = jnp.concatenate([z, w[:, 1]], axis=-1)
    return jnp.concatenate([top, bot], axis=-2).astype(BF16)


def kernel(x, c, ctx, c_ctx, w_ada, b_ada, g_norm1, g_norm2, w_in, conv_w, conv_b, rg_wa, rg_ba, rg_wx, rg_bx,
           rg_lambda, w_rnn_out, g_q_lora, w_uq, g_kv_lora, w_uk, w_uv, g_q_nope, g_q_rope, g_k_nope, g_k_rope,
           w_mla_out, w_o, w_router, w_e_gate, w_e_up, w_e_down):
    assert w_ada.shape[0] == 1, "single-layer kernel"
    b, s, d = x.shape
    lc = ctx.shape[1]
    tile = lc
    n_lat = s // tile
    cap = CAPACITY_FACTOR * s // N_EXPERTS

    pad_rows = (-(b + 1)) % SUBLANE
    cc = jnp.concatenate([c, c_ctx[None, :], jnp.zeros((pad_rows, d), F32)], axis=0)
    mod_all = _ada(cc, w_ada[0], b_ada[0])
    mod = jnp.stack([mod_all[:b], jnp.broadcast_to(mod_all[b], (b, N_MOD * d))], axis=1).reshape(2 * b, 1, N_MOD * d)

    win_p, wuq_p, wuq_s, wuk_p, wuv_p = _layout_weights(w_in[0], w_uq[0], w_uk[0], w_uv[0], d)
    tqc, tqs, tkc, tks, gkn = _rope_tables(s, lc, g_q_nope[0], g_q_rope[0], g_k_nope[0], g_k_rope[0])
    z0, gz, q, k, v, sg = _inproj(x, ctx, mod, g_norm1, win_p, g_q_lora, wuq_p, wuq_s, g_kv_lora, wuk_p,
                                  wuv_p, gkn, tqc, tqs, tkc, tks, tile)

    h_f, xr = _rnn(z0, _pair_blocks(rg_wa[0, 0]), _pair_blocks(rg_wx[0, 0]), rg_ba[0, 0:1], rg_bx[0, 0:1],
                   rg_lambda[0, 0:1], tile, n_lat, False, conv=(conv_w[0], conv_b))
    rnn = _rnn(xr, _pair_blocks(rg_wa[0, 1]), _pair_blocks(rg_wx[0, 1]), rg_ba[0, 1:2], rg_bx[0, 1:2],
               rg_lambda[0, 1:2], tile, n_lat, True, hf=h_f, gz=gz)

    attn = _attention(q, k, v, s, 2 * MXU_COLS)

    x_new, h2, aff = _merge(rnn, attn, sg, x, mod, g_norm2, w_rnn_out[0].astype(BF16), w_mla_out[0].astype(BF16),
                            w_o[0].astype(BF16), _pad_lanes(w_router[0], LANE).astype(BF16), 2 * tile)

    idx, gate = _topk(jnp.swapaxes(aff, 1, 2), cap)
    xg = _gather(idx, h2, cap)
    return _moe(idx, gate, xg, w_e_gate[0].astype(BF16), w_e_up[0].astype(BF16), w_e_down[0].astype(BF16), mod, x_new)
```

```python
import functools

import jax
import jax.numpy as jnp
from jax import lax
from jax.experimental import pallas as pl
from jax.experimental.pallas import tpu as pltpu

EPS = 1e-6
N_MOD = 6
GRID_W = 64
RG_BLOCKS = 8
CONV_W = 4
CONV_PAD_L = 2
RG_C = 8.0
MLA_HEADS = 8
QK_NOPE = 64
QK_ROPE = 32
V_HEAD = 64
Q_LORA = 256
KV_LORA = 128
ROPE_BASE = 10000.0
ATTN_SCALE = (QK_NOPE + QK_ROPE) ** -0.5
LOG2_E = 1.4426950408889634
N_EXPERTS = 16
CAPACITY_FACTOR = 2

LANE = 128
SUBLANE = 8
MXU_COLS = 256
HEAD_PAD = LANE
HALF_ROPE = QK_ROPE // 2
RNN_SUB = 64
SLOT_LO = 32
VMEM_LIMIT = 56 * 1024 * 1024

BF16 = jnp.bfloat16
F32 = jnp.float32


def _cparams(sem):
    return pltpu.CompilerParams(dimension_semantics=sem, vmem_limit_bytes=VMEM_LIMIT)


def _ada_kernel(c_ref, w_ref, b_ref, o_ref):
    cc = c_ref[...]
    s = cc * jax.nn.sigmoid(cc)
    o_ref[...] = jnp.dot(s, w_ref[...], preferred_element_type=F32) + b_ref[...]


def _ada(cc, w_ada, b_ada):
    rows, d = cc.shape
    n = w_ada.shape[1]
    bn = d
    return pl.pallas_call(
        _ada_kernel,
        grid=(n // bn,),
        in_specs=[pl.BlockSpec((rows, d), lambda j: (0, 0)),
                  pl.BlockSpec((d, bn), lambda j: (0, j)),
                  pl.BlockSpec((1, bn), lambda j: (0, j))],
        out_specs=pl.BlockSpec((rows, bn), lambda j: (0, j)),
        out_shape=jax.ShapeDtypeStruct((rows, n), F32),
        compiler_params=_cparams(("arbitrary",)),
        name="ada",
    )(cc, w_ada, b_ada.reshape(1, n))


def _inproj_kernel(x_ref, ctx_ref, mod_ref, g1_ref, win_ref, gq_ref, wuq_ref, wuqs_ref, gkv_ref, wuk_ref,
                   wuv_ref, gkn_ref, tqc_ref, tqs_ref, tkc_ref, tks_ref,
                   z0_ref, gz_ref, q_ref, k_ref, v_ref, sg_ref, *, d, n_lat_tiles):
    i = pl.program_id(1)
    is_ctx = i == n_lat_tiles
    xt = jnp.where(is_ctx, ctx_ref[0], x_ref[0])
    shift = mod_ref[0, :, 0:d]
    scale = mod_ref[0, :, d:2 * d]
    inv = lax.rsqrt(jnp.mean(xt * xt, axis=-1, keepdims=True) + EPS)
    xn = ((xt * inv * g1_ref[...]) * (1.0 + scale) + shift).astype(BF16)

    def seg(lo, hi):
        return jnp.dot(xn, win_ref[:, lo:hi], preferred_element_type=F32)

    o_q = 2 * d
    o_kv = o_q + Q_LORA
    o_kr = o_kv + KV_LORA
    o_g = o_kr + 2 * LANE
    zq = seg(o_q, o_kv)
    zkv = seg(o_kv, o_kr)
    kr_a = seg(o_kr, o_kr + LANE)
    kr_s = seg(o_kr + LANE, o_g)

    lane = lax.broadcasted_iota(jnp.int32, (1, LANE), 1)
    is_nope = lane < QK_NOPE
    is_rope = jnp.logical_and(lane >= QK_NOPE, lane < QK_NOPE + QK_ROPE)

    zqn = (zq * lax.rsqrt(jnp.mean(zq * zq, axis=-1, keepdims=True) + EPS) * gq_ref[...]).astype(BF16)
    qa = jnp.dot(zqn, wuq_ref[...], preferred_element_type=F32)
    qs = jnp.dot(zqn, wuqs_ref[...], preferred_element_type=F32)
    tqc = tqc_ref[...]
    tqs = tqs_ref[...]
    for h in range(MLA_HEADS):
        a = qa[:, h * HEAD_PAD:(h + 1) * HEAD_PAD]
        s = qs[:, h * HEAD_PAD:(h + 1) * HEAD_PAD]
        sq = a * a
        ssn = jnp.sum(jnp.where(is_nope, sq, 0.0), axis=-1, keepdims=True)
        ssr = jnp.sum(jnp.where(is_rope, sq, 0.0), axis=-1, keepdims=True)
        invh = jnp.where(is_nope, lax.rsqrt(ssn * (1.0 / QK_NOPE) + EPS), lax.rsqrt(ssr * (1.0 / QK_ROPE) + EPS))
        q_ref[0, :, h * HEAD_PAD:(h + 1) * HEAD_PAD] = (invh * (a * tqc + s * tqs)).astype(BF16)

    ckv = (zkv * lax.rsqrt(jnp.mean(zkv * zkv, axis=-1, keepdims=True) + EPS) * gkv_ref[...]).astype(BF16)
    lane_h = lax.broadcasted_iota(jnp.int32, (1, MLA_HEADS * HEAD_PAD), 1) % HEAD_PAD
    vp = jnp.dot(ckv, wuv_ref[...], preferred_element_type=F32) + jnp.where(lane_h == V_HEAD, 1.0, 0.0)
    v_ref[0] = vp.T.astype(BF16)
    ka = jnp.dot(ckv, wuk_ref[...], preferred_element_type=F32)
    inv_r = lax.rsqrt(jnp.sum(kr_a * kr_a, axis=-1, keepdims=True) * (1.0 / QK_ROPE) + EPS)
    kr = inv_r * (kr_a * tkc_ref[...] + kr_s * tks_ref[...])
    gkn = gkn_ref[...]
    for h in range(MLA_HEADS):
        a = ka[:, h * HEAD_PAD:(h + 1) * HEAD_PAD]
        invh = lax.rsqrt(jnp.sum(a * a, axis=-1, keepdims=True) * (1.0 / QK_NOPE) + EPS)
        k_ref[0, :, h * HEAD_PAD:(h + 1) * HEAD_PAD] = (a * invh * gkn + kr).astype(BF16)

    z0_ref[0] = seg(0, d)
    gz_ref[0] = jax.nn.gelu(seg(d, o_q))
    sg_ref[0] = jax.nn.sigmoid(seg(o_g, o_g + 2 * d))


def _inproj(x, ctx, mod, g1, win_p, gq, wuq_p, wuq_s, gkv, wuk_p, wuv, gkn, tqc, tqs, tkc, tks, tile):
    b, s, d = x.shape
    lc = ctx.shape[1]
    assert lc == tile and s % tile == 0
    nl = s // tile
    ltot = s + lc
    n_in = win_p.shape[1]
    hp = MLA_HEADS * HEAD_PAD
    const = lambda bb, i: (0, 0)
    row = lambda bb, i: (bb, i, 0)
    tab = lambda bb, i: (i, 0)
    out_shapes = (
        jax.ShapeDtypeStruct((b, ltot, d), F32),
        jax.ShapeDtypeStruct((b, ltot, d), F32),
        jax.ShapeDtypeStruct((b, ltot, hp), BF16),
        jax.ShapeDtypeStruct((b, ltot, hp), BF16),
        jax.ShapeDtypeStruct((b, hp, ltot), BF16),
        jax.ShapeDtypeStruct((b, ltot, 2 * d), F32),
    )
    return pl.pallas_call(
        functools.partial(_inproj_kernel, d=d, n_lat_tiles=nl),
        grid=(b, nl + 1),
        in_specs=[
            pl.BlockSpec((1, tile, d), lambda bb, i: (bb, jnp.minimum(i, nl - 1), 0)),
            pl.BlockSpec((1, tile, d), lambda bb, i: (bb, 0, 0)),
            pl.BlockSpec((1, 1, N_MOD * d), lambda bb, i: (2 * bb + i // nl, 0, 0)),
            pl.BlockSpec((1, d), const),
            pl.BlockSpec((d, n_in), const),
            pl.BlockSpec((1, Q_LORA), const),
            pl.BlockSpec((Q_LORA, hp), const),
            pl.BlockSpec((Q_LORA, hp), const),
            pl.BlockSpec((1, KV_LORA), const),
            pl.BlockSpec((KV_LORA, hp), const),
            pl.BlockSpec((KV_LORA, hp), const),
            pl.BlockSpec((1, LANE), const),
            pl.BlockSpec((tile, LANE), tab),
            pl.BlockSpec((tile, LANE), tab),
            pl.BlockSpec((tile, LANE), tab),
            pl.BlockSpec((tile, LANE), tab),
        ],
        out_specs=[
            pl.BlockSpec((1, tile, d), row),
            pl.BlockSpec((1, tile, d), row),
            pl.BlockSpec((1, tile, hp), row),
            pl.BlockSpec((1, tile, hp), row),
            pl.BlockSpec((1, hp, tile), lambda bb, i: (bb, 0, i)),
            pl.BlockSpec((1, tile, 2 * d), row),
        ],
        out_shape=out_shapes,
        compiler_params=_cparams(("arbitrary", "arbitrary")),
        name="inproj",
    )(x, ctx, mod, g1, win_p, gq, wuq_p, wuq_s, gkv, wuk_p, wuv, gkn, tqc, tqs, tkc, tks)


def _sigmoid(x):
    return 0.5 * jnp.tanh(0.5 * x) + 0.5


def _softplus(x):
    e = jnp.exp(-jnp.abs(x))
    u = 1.0 + e
    tiny = u == 1.0
    log1p_e = jnp.where(tiny, e, jnp.log(u) * (e / jnp.where(tiny, 1.0, u - 1.0)))
    return jnp.maximum(x, 0.0) + log1p_e


def _rnn_kernel(*refs, d, tile, n_lat, reverse):
    if reverse:
        (xr_ref, wa_ref, wx_ref, ba_ref, bx_ref, lam_ref, hf_ref, gz_ref, o_ref, a_s, b_s, hb_s, h_s) = refs
    else:
        (z_ref, zp_ref, zn_ref, cw_ref, cb_ref, wa_ref, wx_ref, ba_ref, bx_ref, lam_ref,
         o_ref, xr_ref, a_s, b_s, h_s) = refs
    j = pl.program_id(1)
    if not reverse:
        c = jnp.where(j == 0, n_lat, j - 1)
        prev_zero = jnp.logical_or(c == 0, c == n_lat)
        next_zero = c >= n_lat - 1
        ext = jnp.concatenate([jnp.where(prev_zero, 0.0, zp_ref[0]), z_ref[0], jnp.where(next_zero, 0.0, zn_ref[0])],
                              axis=0)
        n_ext = tile + 2 * SUBLANE
        taps = [ext if (CONV_PAD_L - k) % n_ext == 0 else pltpu.roll(ext, (CONV_PAD_L - k) % n_ext, 0)
                for k in range(CONV_W)]
    pair = 2 * (d // RG_BLOCKS)
    neg_c_softplus = -RG_C * _softplus(-lam_ref[...])
    dst = hb_s if reverse else o_ref.at[0]

    h = jnp.where(j == 0, 0.0, h_s[...])
    subs = range(tile // RNN_SUB)
    for sb in (reversed(subs) if reverse else subs):
        r0 = sb * RNN_SUB
        if reverse:
            xr = xr_ref[0, r0:r0 + RNN_SUB, :]
        else:
            xr = cb_ref[...]
            for k in range(CONV_W):
                xr = xr + taps[k][SUBLANE + r0:SUBLANE + r0 + RNN_SUB] * cw_ref[k:k + 1, :]
            xr_ref[0, r0:r0 + RNN_SUB, :] = xr
        xb = xr.astype(BF16)
        ya = jnp.concatenate([jnp.dot(xb[:, m * pair:(m + 1) * pair], wa_ref[m], preferred_element_type=F32)
                              for m in range(d // pair)], axis=-1)
        yx = jnp.concatenate([jnp.dot(xb[:, m * pair:(m + 1) * pair], wx_ref[m], preferred_element_type=F32)
                              for m in range(d // pair)], axis=-1)
        r = _sigmoid(ya + ba_ref[...])
        gi = _sigmoid(yx + bx_ref[...])
        log_a = neg_c_softplus * r
        a_s[r0:r0 + RNN_SUB, :] = jnp.exp(log_a)
        th = jnp.tanh(log_a)
        v = -2.0 * th / (1.0 - th)
        b_s[r0:r0 + RNN_SUB, :] = (v * lax.rsqrt(jnp.maximum(v, 1e-30))) * (gi * xr)
        rows = range(r0, r0 + RNN_SUB)
        for t in (reversed(rows) if reverse else rows):
            h = a_s[t:t + 1, :] * h + b_s[t:t + 1, :]
            dst[t:t + 1, :] = h
    h_s[...] = h
    if reverse:
        o_ref[0] = ((hf_ref[0] + hb_s[...]) * gz_ref[0]).astype(BF16)


def _rnn(z, wa_p, wx_p, ba, bx, lam, tile, n_lat, reverse, conv=None, hf=None, gz=None):
    b, ltot, d = z.shape
    n_halo = ltot // SUBLANE
    per = tile // SUBLANE
    if reverse:
        cidx = lambda j: jnp.where(j == 0, n_lat, n_lat - j)
        oidx = lambda j: jnp.where(j == 0, n_lat - 1, n_lat - j)
    else:
        cidx = lambda j: jnp.where(j == 0, n_lat, j - 1)
        oidx = lambda j: jnp.maximum(j - 1, 0)
    const2 = lambda bb, j: (0, 0)
    const3 = lambda bb, j: (0, 0, 0)
    chunk = pl.BlockSpec((1, tile, d), lambda bb, j: (bb, cidx(j), 0))
    out_chunk = pl.BlockSpec((1, tile, d), lambda bb, j: (bb, oidx(j), 0))
    gate_specs = [pl.BlockSpec(wa_p.shape, const3), pl.BlockSpec(wx_p.shape, const3),
                  pl.BlockSpec((1, d), const2), pl.BlockSpec((1, d), const2), pl.BlockSpec((1, d), const2)]
    gate_args = [wa_p, wx_p, ba, bx, lam]
    scratch = [pltpu.VMEM((tile, d), F32), pltpu.VMEM((tile, d), F32)]
    if reverse:
        in_specs = [chunk] + gate_specs + [out_chunk, chunk]
        args = [z] + gate_args + [hf, gz]
        out_specs = out_chunk
        out_shape = jax.ShapeDtypeStruct((b, n_lat * tile, d), BF16)
        scratch += [pltpu.VMEM((tile, d), F32)]
    else:
        in_specs = [chunk,
                    pl.BlockSpec((1, SUBLANE, d), lambda bb, j: (bb, jnp.maximum(cidx(j) * per - 1, 0), 0)),
                    pl.BlockSpec((1, SUBLANE, d), lambda bb, j: (bb, jnp.minimum((cidx(j) + 1) * per, n_halo - 1), 0)),
                    pl.BlockSpec((CONV_W, d), const2), pl.BlockSpec((1, d), const2)] + gate_specs
        args = [z, z, z, conv[0], conv[1]] + gate_args
        out_specs = [out_chunk, chunk]
        out_shape = (jax.ShapeDtypeStruct((b, n_lat * tile, d), F32), jax.ShapeDtypeStruct((b, ltot, d), F32))
    scratch += [pltpu.VMEM((1, d), F32)]
    return pl.pallas_call(
        functools.partial(_rnn_kernel, d=d, tile=tile, n_lat=n_lat, reverse=reverse),
        grid=(b, n_lat + 1),
        in_specs=in_specs,
        out_specs=out_specs,
        out_shape=out_shape,
        scratch_shapes=scratch,
        compiler_params=_cparams(("arbitrary", "arbitrary")),
        name="rnn_bwd" if reverse else "rnn_fwd",
    )(*args)


def _attn_kernel(q_ref, k_ref, vt_ref, o_ref, s_even, s_odd, m_even, m_odd):
    g = pl.program_id(0)

    @pl.when(g == 0)
    def _():
        s_odd[...] = jnp.zeros_like(s_odd)
        m_odd[...] = jnp.zeros_like(m_odd)

    n_keys = k_ref.shape[1]
    tq = q_ref.shape[1]

    def step(s_new, m_new, s_prev, m_prev):
        heads = range(2)
        lanes = [slice(hh * HEAD_PAD, (hh + 1) * HEAD_PAD) for hh in heads]
        qs = [q_ref[0, :, lanes[hh]] for hh in heads]
        m_run = [None, None]
        acc = [jnp.zeros((HEAD_PAD, tq), F32) for _ in heads]
        for c in range(n_keys // MXU_COLS):
            rows = slice(c * MXU_COLS, (c + 1) * MXU_COLS)
            for hh in heads:
                st = lax.dot_general(k_ref[0, rows, lanes[hh]], qs[hh], (((1,), (1,)), ((), ())),
                                     preferred_element_type=F32)
                s_new[hh, rows, :] = st
                cm = jnp.max(st, axis=0, keepdims=True)
                m_run[hh] = cm if c == 0 else jnp.maximum(m_run[hh], cm)
                pt = jnp.exp2(s_prev[hh, rows, :] - m_prev[hh]).astype(BF16)
                acc[hh] = acc[hh] + jnp.dot(vt_ref[0, lanes[hh], rows], pt, preferred_element_type=F32)
        for hh in heads:
            m_new[hh] = m_run[hh]
        outs = [acc[hh][0:V_HEAD] / acc[hh][V_HEAD:V_HEAD + 1] for hh in heads]
        o_ref[0] = jnp.concatenate(outs, axis=0).T.astype(BF16)

    @pl.when(g % 2 == 0)
    def _():
        step(s_even, m_even, s_odd, m_odd)

    @pl.when(g % 2 == 1)
    def _():
        step(s_odd, m_odd, s_even, m_even)


def _attention(q, k, vt, s, tq):
    b, ltot, _ = q.shape
    pairs = MLA_HEADS // 2
    nq = s // tq
    n_items = b * pairs * nq

    def item(g):
        return g // (pairs * nq), (g // nq) % pairs, g % nq

    def cur(g):
        return item(jnp.minimum(g, n_items - 1))

    def prev(g):
        return item(jnp.maximum(g - 1, 0))

    scores = pltpu.VMEM((2, ltot, tq), F32)
    colmax = pltpu.VMEM((2, 1, tq), F32)
    return pl.pallas_call(
        _attn_kernel,
        grid=(n_items + 1,),
        in_specs=[pl.BlockSpec((1, tq, 2 * HEAD_PAD), lambda g: (cur(g)[0], cur(g)[2], cur(g)[1])),
                  pl.BlockSpec((1, ltot, 2 * HEAD_PAD), lambda g: (cur(g)[0], 0, cur(g)[1])),
                  pl.BlockSpec((1, 2 * HEAD_PAD, ltot), lambda g: (prev(g)[0], prev(g)[1], 0))],
        out_specs=pl.BlockSpec((1, tq, 2 * V_HEAD), lambda g: (prev(g)[0], prev(g)[2], prev(g)[1])),
        out_shape=jax.ShapeDtypeStruct((b, s, MLA_HEADS * V_HEAD), BF16),
        scratch_shapes=[scores, scores, colmax, colmax],
        compiler_params=_cparams(("arbitrary",)),
        name="attn",
    )(q, k, vt)


def _merge_kernel(rnn_ref, attn_ref, sg_ref, x_ref, mod_ref, modp_ref, g2_ref, wr_ref, wm_ref, wo_ref, wrt_ref,
                  xn_ref, h2_ref, aff_ref, xn_even, xn_odd, *, d):
    g = pl.program_id(0)

    @pl.when(g == 0)
    def _():
        xn_odd[...] = jnp.zeros_like(xn_odd)

    def step(xn_new, xn_prev):
        pr = jnp.dot(rnn_ref[0], wr_ref[...], preferred_element_type=F32)
        pa = jnp.dot(attn_ref[0], wm_ref[...], preferred_element_type=F32)
        merged = (sg_ref[0, :, 0:d] * pr + sg_ref[0, :, d:2 * d] * pa).astype(BF16)

        xp = xn_prev[...]
        inv = lax.rsqrt(jnp.mean(xp * xp, axis=-1, keepdims=True) + EPS)
        h2 = (xp * inv * g2_ref[...]) * (1.0 + modp_ref[0, :, 4 * d:5 * d]) + modp_ref[0, :, 3 * d:4 * d]
        n_tok = h2.shape[0]
        for t in range(d // LANE):
            h2_ref[0, pl.ds(t, n_tok, stride=d // LANE), :] = h2[:, t * LANE:(t + 1) * LANE]
        logits = jnp.dot(h2.astype(BF16), wrt_ref[...], preferred_element_type=F32)[:, 0:N_EXPERTS]
        e = jnp.exp(logits - jnp.max(logits, axis=-1, keepdims=True))
        aff_ref[0] = e / jnp.sum(e, axis=-1, keepdims=True)

        xn = x_ref[0] + mod_ref[0, :, 2 * d:3 * d] * jnp.dot(merged, wo_ref[...], preferred_element_type=F32)
        xn_ref[0] = xn
        xn_new[...] = xn

    @pl.when(g % 2 == 0)
    def _():
        step(xn_even, xn_odd)

    @pl.when(g % 2 == 1)
    def _():
        step(xn_odd, xn_even)


def _merge(rnn, attn, sg, x, mod, g2, wr, wm, wo, wrt, tile):
    b, s, d = x.shape
    nt = s // tile
    n_items = b * nt
    const = lambda g: (0, 0)
    cur = lambda g: jnp.minimum(g, n_items - 1)
    prev = lambda g: jnp.maximum(g - 1, 0)
    row = lambda g: (cur(g) // nt, cur(g) % nt, 0)
    row_prev = lambda g: (prev(g) // nt, prev(g) % nt, 0)
    return pl.pallas_call(
        functools.partial(_merge_kernel, d=d),
        grid=(n_items + 1,),
        in_specs=[pl.BlockSpec((1, tile, d), row),
                  pl.BlockSpec((1, tile, attn.shape[2]), row),
                  pl.BlockSpec((1, tile, 2 * d), row),
                  pl.BlockSpec((1, tile, d), row),
                  pl.BlockSpec((1, 1, N_MOD * d), lambda g: (2 * (cur(g) // nt), 0, 0)),
                  pl.BlockSpec((1, 1, N_MOD * d), lambda g: (2 * (prev(g) // nt), 0, 0)),
                  pl.BlockSpec((1, d), const),
                  pl.BlockSpec(wr.shape, const),
                  pl.BlockSpec(wm.shape, const),
                  pl.BlockSpec(wo.shape, const),
                  pl.BlockSpec(wrt.shape, const)],
        out_specs=[pl.BlockSpec((1, tile, d), row),
                   pl.BlockSpec((1, tile * (d // LANE), LANE), row_prev),
                   pl.BlockSpec((1, tile, N_EXPERTS), row_prev)],
        out_shape=(jax.ShapeDtypeStruct((b, s, d), F32),
                   jax.ShapeDtypeStruct((b, s * (d // LANE), LANE), F32),
                   jax.ShapeDtypeStruct((b, s, N_EXPERTS), F32)),
        scratch_shapes=[pltpu.VMEM((tile, d), F32), pltpu.VMEM((tile, d), F32)],
        compiler_params=_cparams(("arbitrary",)),
        name="merge",
    )(rnn, attn, sg, x, mod, mod, g2, wr, wm, wo, wrt)


def _excl_cumsum_lanes(x01, tri):
    rows, length = x01.shape
    offset = jnp.zeros((rows, 1), F32)
    pieces = []
    for blk in range(length // LANE):
        xb = x01[:, blk * LANE:(blk + 1) * LANE]
        inc = jnp.dot(xb.astype(BF16), tri, preferred_element_type=F32)
        pieces.append(inc - xb + offset)
        offset = offset + jnp.sum(xb, axis=-1, keepdims=True)
    return jnp.concatenate(pieces, axis=-1)


def _topk_kernel(aff_ref, idx_ref, gate_ref, pos_s, sel_s, *, cap):
    a = aff_ref[0]
    n_e, length = a.shape

    def as_float(bits):
        return lax.bitcast_convert_type(bits, F32)

    def bisect(i, cand):
        t = cand | (jnp.int32(1) << (30 - i))
        cnt = jnp.sum((a >= as_float(t)).astype(jnp.int32), axis=-1, keepdims=True)
        return jnp.where(cnt >= cap, t, cand)

    thr_bits = lax.fori_loop(0, 31, bisect, jnp.zeros((n_e, 1), jnp.int32))
    thr = as_float(thr_bits)
    above = as_float(thr_bits + 1)
    gt = (a >= above).astype(F32)
    eq = jnp.logical_and(a >= thr, a < above).astype(F32)
    need = cap - jnp.sum(gt, axis=-1, keepdims=True)
    r_i = lax.broadcasted_iota(jnp.int32, (LANE, LANE), 0)
    c_i = lax.broadcasted_iota(jnp.int32, (LANE, LANE), 1)
    tri = (r_i <= c_i).astype(BF16)
    tie_rank = _excl_cumsum_lanes(eq, tri)
    sel = gt + eq * (tie_rank < need).astype(F32)
    pos_s[...] = _excl_cumsum_lanes(sel, tri)
    sel_s[...] = sel

    n_hi = idx_ref.shape[2]
    tok = lax.broadcasted_iota(jnp.int32, (1, length), 1)
    tok_hi = (tok >> 6).astype(F32)
    tok_lo = (tok & 63).astype(F32)
    a_iota = lax.broadcasted_iota(jnp.int32, (n_hi, 1), 0)
    c_iota = lax.broadcasted_iota(jnp.int32, (SLOT_LO, 1), 0)

    def compact(e, carry):
        pos_row = pos_s[pl.ds(e, 1), :].astype(jnp.int32)
        sel_row = sel_s[pl.ds(e, 1), :] > 0.5
        hot_a = jnp.where(jnp.logical_and((pos_row // SLOT_LO) == a_iota, sel_row), 1.0, 0.0)
        hot_c = jnp.where((pos_row % SLOT_LO) == c_iota, 1.0, 0.0).astype(BF16)
        g = aff_ref[0, pl.ds(e, 1), :]
        g1 = g.astype(BF16).astype(F32)
        g2 = (g - g1).astype(BF16).astype(F32)
        g3 = g - g1 - g2
        lhs = jnp.concatenate([hot_a * v for v in (tok_hi, tok_lo, g1, g2, g3)], axis=0).astype(BF16)
        res = lax.dot_general(lhs, hot_c, (((1,), (1,)), ((), ())), preferred_element_type=F32)
        part = [res[v * n_hi:(v + 1) * n_hi] for v in range(5)]
        idx_ref[0, e] = (part[0] * 64.0 + part[1]).astype(jnp.int32)
        gate_ref[0, e] = part[2] + part[3] + part[4]
        return carry

    lax.fori_loop(0, n_e, compact, 0)


def _topk(aff_t, cap):
    b, n_e, length = aff_t.shape
    n_hi = max(cap // SLOT_LO, SUBLANE)
    blk = lambda bb: (bb, 0, 0)
    blk4 = lambda bb: (bb, 0, 0, 0)
    idx, gate = pl.pallas_call(
        functools.partial(_topk_kernel, cap=cap),
        grid=(b,),
        in_specs=[pl.BlockSpec((1, n_e, length), blk)],
        out_specs=[pl.BlockSpec((1, n_e, n_hi, SLOT_LO), blk4), pl.BlockSpec((1, n_e, n_hi, SLOT_LO), blk4)],
        out_shape=(jax.ShapeDtypeStruct((b, n_e, n_hi, SLOT_LO), jnp.int32),
                   jax.ShapeDtypeStruct((b, n_e, n_hi, SLOT_LO), F32)),
        scratch_shapes=[pltpu.VMEM((n_e, length), F32), pltpu.VMEM((n_e, length), F32)],
        compiler_params=_cparams(("arbitrary",)),
        name="topk",
    )(aff_t)
    flat = lambda a: a.reshape(b * n_e, 1, n_hi * SLOT_LO)[:, :, :cap]
    return flat(idx), flat(gate)


def _gather_kernel(idx_ref, h_ref, o_ref, rows_s, *, cap, d):
    tiles = d // LANE

    def body(j, carry):
        src = pl.multiple_of(idx_ref[0, 0, j] * tiles, tiles)
        dst = pl.multiple_of(j * tiles, tiles)
        rows_s[pl.ds(dst, tiles), :] = h_ref[0, pl.ds(src, tiles), :]
        return carry

    lax.fori_loop(0, cap, body, 0, unroll=8)
    o_ref[0, 0] = jnp.concatenate([rows_s[pl.ds(t, cap, stride=tiles), :] for t in range(tiles)],
                                  axis=-1).astype(BF16)


def _gather(idx, h2t, cap, d):
    b, rows, lanes = h2t.shape
    assert d // lanes == SUBLANE and lanes == LANE
    return pl.pallas_call(
        functools.partial(_gather_kernel, cap=cap, d=d),
        grid=(b, N_EXPERTS),
        in_specs=[pl.BlockSpec((1, 1, cap), lambda bb, e: (bb * N_EXPERTS + e, 0, 0), memory_space=pltpu.SMEM),
                  pl.BlockSpec((1, rows, lanes), lambda bb, e: (bb, 0, 0))],
        out_specs=pl.BlockSpec((1, 1, cap, d), lambda bb, e: (bb, e, 0, 0)),
        out_shape=jax.ShapeDtypeStruct((b, N_EXPERTS, cap, d), BF16),
        scratch_shapes=[pltpu.VMEM((cap * (d // LANE), LANE), F32)],
        compiler_params=_cparams(("arbitrary", "arbitrary")),
        name="gather",
    )(idx, h2t)


def _moe_kernel(idx_ref, gate_ref, x_ref, wg_ref, wu_ref, wd_ref, mod_ref, res_ref, o_ref, y_even, y_odd, *, d, cap):
    g = pl.program_id(0)
    e_prev = jnp.maximum(g - 1, 0) % N_EXPERTS
    piece = res_ref.shape[1]

    @pl.when(g == 0)
    def _():
        y_odd[...] = jnp.zeros_like(y_odd)

    @pl.when(e_prev == 0)
    def _():
        o_ref[...] = jnp.zeros_like(o_ref)

    def step(y_new, y_prev):
        xb = x_ref[0, 0]
        gg = jnp.dot(xb, wg_ref[0], preferred_element_type=F32)
        uu = jnp.dot(xb, wu_ref[0], preferred_element_type=F32)
        hid = (gg * jax.nn.sigmoid(gg) * uu).astype(BF16)
        y_new[...] = jnp.dot(hid, wd_ref[0], preferred_element_type=F32) * mod_ref[0, :, 5 * d:6 * d]

        rows = pl.ds(pl.multiple_of(e_prev * piece, piece), piece)
        o_ref[0, rows, :] = o_ref[0, rows, :] + res_ref[0]
        for j in range(cap):
            r = idx_ref[0, 0, j]
            o_ref[0, pl.ds(r, 1), :] = o_ref[0, pl.ds(r, 1), :] + gate_ref[0, 0, j] * y_prev[j:j + 1, :]

    @pl.when(g % 2 == 0)
    def _():
        step(y_even, y_odd)

    @pl.when(g % 2 == 1)
    def _():
        step(y_odd, y_even)


def _moe(idx, gate, xg, w_gate, w_up, w_down, mod, x_new):
    b, n_e, cap, d = xg.shape
    s = x_new.shape[1]
    f = w_gate.shape[2]
    n_items = b * n_e
    cur = lambda g: jnp.minimum(g, n_items - 1)
    prev = lambda g: jnp.maximum(g - 1, 0)
    smem = lambda: pl.BlockSpec((1, 1, cap), lambda g: (prev(g), 0, 0), memory_space=pltpu.SMEM)
    wspec = lambda shape: pl.BlockSpec(shape, lambda g: (cur(g) % n_e, 0, 0))
    return pl.pallas_call(
        functools.partial(_moe_kernel, d=d, cap=cap),
        grid=(n_items + 1,),
        in_specs=[smem(), smem(),
                  pl.BlockSpec((1, 1, cap, d), lambda g: (cur(g) // n_e, cur(g) % n_e, 0, 0)),
                  wspec((1, d, f)), wspec((1, d, f)), wspec((1, f, d)),
                  pl.BlockSpec((1, 1, N_MOD * d), lambda g: (2 * (cur(g) // n_e), 0, 0)),
                  pl.BlockSpec((1, s // n_e, d), lambda g: (prev(g) // n_e, prev(g) % n_e, 0))],
        out_specs=pl.BlockSpec((1, s, d), lambda g: (prev(g) // n_e, 0, 0), pipeline_mode=pl.Buffered(1)),
        out_shape=jax.ShapeDtypeStruct((b, s, d), F32),
        scratch_shapes=[pltpu.VMEM((cap, d), F32), pltpu.VMEM((cap, d), F32)],
        compiler_params=_cparams(("arbitrary",)),
        name="moe",
    )(idx, gate, xg, w_gate, w_up, w_down, mod, x_new)


def _pad_lanes(a, width):
    return jnp.pad(a, [(0, 0)] * (a.ndim - 1) + [(0, width - a.shape[-1])])


def _rope_tables(s, lc, g_q_nope, g_q_rope, g_k_nope, g_k_rope):
    rows = s // GRID_W
    row = jnp.repeat(jnp.arange(rows, dtype=F32), GRID_W)
    col = jnp.tile(jnp.arange(GRID_W, dtype=F32), rows)
    n_pairs = QK_ROPE // 4
    inv_freq = ROPE_BASE ** (-jnp.arange(n_pairs, dtype=F32) / n_pairs)
    ang = jnp.concatenate([row[:, None] * inv_freq, col[:, None] * inv_freq], axis=-1)
    cos = jnp.concatenate([jnp.cos(ang), jnp.ones((lc, HALF_ROPE), F32)], axis=0)
    sin = jnp.concatenate([jnp.sin(ang), jnp.zeros((lc, HALF_ROPE), F32)], axis=0)
    ltot = s + lc
    ones = jnp.ones((ltot, 1), F32)

    def tables(g_nope, g_rope, scale):
        g1, g2 = g_rope[:HALF_ROPE], g_rope[HALF_ROPE:]
        tc = jnp.concatenate([ones * g_nope[None, :], cos * g1, cos * g2], axis=-1) * scale
        ts = jnp.concatenate([jnp.zeros((ltot, QK_NOPE), F32), -sin * g2, sin * g1], axis=-1) * scale
        return _pad_lanes(tc, LANE), _pad_lanes(ts, LANE)

    tqc, tqs = tables(g_q_nope, g_q_rope, ATTN_SCALE * LOG2_E)
    tkc, tks = tables(jnp.zeros((QK_NOPE,), F32), g_k_rope, 1.0)
    gkn = _pad_lanes(g_k_nope[None, :], LANE)
    return tqc, tqs, tkc, tks, gkn


def _swap_halves(a):
    return jnp.concatenate([a[..., HALF_ROPE:], a[..., :HALF_ROPE]], axis=-1)


def _layout_weights(w_in, w_uq, w_uk, w_uv, d):
    o_q = 2 * d
    o_kv = o_q + Q_LORA
    o_kr = o_kv + KV_LORA
    o_g = o_kr + QK_ROPE
    kr = w_in[:, o_kr:o_g]
    zl = jnp.zeros((w_in.shape[0], QK_NOPE), F32)
    kr_a = _pad_lanes(jnp.concatenate([zl, kr], axis=-1), LANE)
    kr_s = _pad_lanes(jnp.concatenate([zl, _swap_halves(kr)], axis=-1), LANE)
    win_p = jnp.concatenate([w_in[:, :o_kr], kr_a, kr_s, w_in[:, o_g:]], axis=-1).astype(BF16)

    uq = w_uq.reshape(Q_LORA, MLA_HEADS, QK_NOPE + QK_ROPE)
    nope, rope = uq[..., :QK_NOPE], uq[..., QK_NOPE:]
    wuq_p = _pad_lanes(jnp.concatenate([nope, rope], axis=-1), HEAD_PAD).reshape(Q_LORA, -1).astype(BF16)
    wuq_s = _pad_lanes(jnp.concatenate([jnp.zeros_like(nope), _swap_halves(rope)], axis=-1),
                       HEAD_PAD).reshape(Q_LORA, -1).astype(BF16)
    wuk_p = _pad_lanes(w_uk.reshape(KV_LORA, MLA_HEADS, QK_NOPE), HEAD_PAD).reshape(KV_LORA, -1).astype(BF16)
    wuv_p = _pad_lanes(w_uv.reshape(KV_LORA, MLA_HEADS, V_HEAD), HEAD_PAD).reshape(KV_LORA, -1).astype(BF16)
    return win_p, wuq_p, wuq_s, wuk_p, wuv_p


def _pair_blocks(w):
    nb, n, _ = w.shape
    w = w.reshape(nb // 2, 2, n, n)
    z = jnp.zeros((nb // 2, n, n), w.dtype)
    top = jnp.concatenate([w[:, 0], z], axis=-1)
    bot = jnp.concatenate([z, w[:, 1]], axis=-1)
    return jnp.concatenate([top, bot], axis=-2).astype(BF16)


def kernel(x, c, ctx, c_ctx, w_ada, b_ada, g_norm1, g_norm2, w_in, conv_w, conv_b, rg_wa, rg_ba, rg_wx, rg_bx,
           rg_lambda, w_rnn_out, g_q_lora, w_uq, g_kv_lora, w_uk, w_uv, g_q_nope, g_q_rope, g_k_nope, g_k_rope,
           w_mla_out, w_o, w_router, w_e_gate, w_e_up, w_e_down):
    assert w_ada.shape[0] == 1, "single-layer kernel"
    b, s, d = x.shape
    lc = ctx.shape[1]
    tile = lc
    n_lat = s // tile
    cap = CAPACITY_FACTOR * s // N_EXPERTS

    pad_rows = (-(b + 1)) % SUBLANE
    cc = jnp.concatenate([c, c_ctx[None, :], jnp.zeros((pad_rows, d), F32)], axis=0)
    mod_all = _ada(cc, w_ada[0], b_ada[0])
    mod = jnp.stack([mod_all[:b], jnp.broadcast_to(mod_all[b], (b, N_MOD * d))], axis=1).reshape(2 * b, 1, N_MOD * d)

    win_p, wuq_p, wuq_s, wuk_p, wuv_p = _layout_weights(w_in[0], w_uq[0], w_uk[0], w_uv[0], d)
    tqc, tqs, tkc, tks, gkn = _rope_tables(s, lc, g_q_nope[0], g_q_rope[0], g_k_nope[0], g_k_rope[0])
    z0, gz, q, k, v, sg = _inproj(x, ctx, mod, g_norm1, win_p, g_q_lora, wuq_p, wuq_s, g_kv_lora, wuk_p,
                                  wuv_p, gkn, tqc, tqs, tkc, tks, tile)

    h_f, xr = _rnn(z0, _pair_blocks(rg_wa[0, 0]), _pair_blocks(rg_wx[0, 0]), rg_ba[0, 0:1], rg_bx[0, 0:1],
                   rg_lambda[0, 0:1], tile, n_lat, False, conv=(conv_w[0], conv_b))
    rnn = _rnn(xr, _pair_blocks(rg_wa[0, 1]), _pair_blocks(rg_wx[0, 1]), rg_ba[0, 1:2], rg_bx[0, 1:2],
               rg_lambda[0, 1:2], tile, n_lat, True, hf=h_f, gz=gz)

    attn = _attention(q, k, v, s, 2 * MXU_COLS)

    x_new, h2, aff = _merge(rnn, attn, sg, x, mod, g_norm2, w_rnn_out[0].astype(BF16), w_mla_out[0].astype(BF16),
                            w_o[0].astype(BF16), _pad_lanes(w_router[0], LANE).astype(BF16), 2 * tile)

    idx, gate = _topk(jnp.swapaxes(aff, 1, 2), cap)
    xg = _gather(idx, h2, cap, d)
    return _moe(idx, gate, xg, w_e_gate[0].astype(BF16), w_e_up[0].astype(BF16), w_e_down[0].astype(BF16), mod, x_new)
```

```python
import functools

import jax
import jax.numpy as jnp
from jax import lax
from jax.experimental import pallas as pl
from jax.experimental.pallas import tpu as pltpu

EPS = 1e-6
N_MOD = 6
GRID_W = 64
RG_BLOCKS = 8
CONV_W = 4
CONV_PAD_L = 2
RG_C = 8.0
MLA_HEADS = 8
QK_NOPE = 64
QK_ROPE = 32
V_HEAD = 64
Q_LORA = 256
KV_LORA = 128
ROPE_BASE = 10000.0
ATTN_SCALE = (QK_NOPE + QK_ROPE) ** -0.5
LOG2_E = 1.4426950408889634
N_EXPERTS = 16
CAPACITY_FACTOR = 2

LANE = 128
SUBLANE = 8
MXU_COLS = 256
HEAD_PAD = LANE
HALF_ROPE = QK_ROPE // 2
RNN_SUB = 64
SLOT_LO = 32
VMEM_LIMIT = 56 * 1024 * 1024

BF16 = jnp.bfloat16
F32 = jnp.float32


def _cparams(sem):
    return pltpu.CompilerParams(dimension_semantics=sem, vmem_limit_bytes=VMEM_LIMIT)


def _ada_kernel(c_ref, w_ref, b_ref, o_ref):
    cc = c_ref[...]
    s = cc * jax.nn.sigmoid(cc)
    o_ref[...] = jnp.dot(s, w_ref[...], preferred_element_type=F32) + b_ref[...]


def _ada(cc, w_ada, b_ada):
    rows, d = cc.shape
    n = w_ada.shape[1]
    bn = d
    return pl.pallas_call(
        _ada_kernel,
        grid=(n // bn,),
        in_specs=[pl.BlockSpec((rows, d), lambda j: (0, 0)),
                  pl.BlockSpec((d, bn), lambda j: (0, j)),
                  pl.BlockSpec((1, bn), lambda j: (0, j))],
        out_specs=pl.BlockSpec((rows, bn), lambda j: (0, j)),
        out_shape=jax.ShapeDtypeStruct((rows, n), F32),
        compiler_params=_cparams(("arbitrary",)),
        name="ada",
    )(cc, w_ada, b_ada.reshape(1, n))


def _inproj_kernel(x_ref, ctx_ref, mod_ref, g1_ref, win_ref, gq_ref, wuq_ref, wuqs_ref, gkv_ref, wuk_ref,
                   wuv_ref, gkn_ref, tqc_ref, tqs_ref, tkc_ref, tks_ref,
                   z0_ref, gz_ref, q_ref, k_ref, v_ref, sg_ref, *, d, n_lat_tiles):
    i = pl.program_id(1)
    is_ctx = i == n_lat_tiles
    xt = jnp.where(is_ctx, ctx_ref[0], x_ref[0])
    shift = mod_ref[0, :, 0:d]
    scale = mod_ref[0, :, d:2 * d]
    inv = lax.rsqrt(jnp.mean(xt * xt, axis=-1, keepdims=True) + EPS)
    xn = ((xt * inv * g1_ref[...]) * (1.0 + scale) + shift).astype(BF16)

    def seg(lo, hi):
        return jnp.dot(xn, win_ref[:, lo:hi], preferred_element_type=F32)

    o_q = 2 * d
    o_kv = o_q + Q_LORA
    o_kr = o_kv + KV_LORA
    o_g = o_kr + 2 * LANE
    zq = seg(o_q, o_kv)
    zkv = seg(o_kv, o_kr)
    kr_a = seg(o_kr, o_kr + LANE)
    kr_s = seg(o_kr + LANE, o_g)

    lane = lax.broadcasted_iota(jnp.int32, (1, LANE), 1)
    is_nope = lane < QK_NOPE
    is_rope = jnp.logical_and(lane >= QK_NOPE, lane < QK_NOPE + QK_ROPE)

    zqn = (zq * lax.rsqrt(jnp.mean(zq * zq, axis=-1, keepdims=True) + EPS) * gq_ref[...]).astype(BF16)
    qa = jnp.dot(zqn, wuq_ref[...], preferred_element_type=F32)
    qs = jnp.dot(zqn, wuqs_ref[...], preferred_element_type=F32)
    tqc = tqc_ref[...]
    tqs = tqs_ref[...]
    for h in range(MLA_HEADS):
        a = qa[:, h * HEAD_PAD:(h + 1) * HEAD_PAD]
        s = qs[:, h * HEAD_PAD:(h + 1) * HEAD_PAD]
        sq = a * a
        ssn = jnp.sum(jnp.where(is_nope, sq, 0.0), axis=-1, keepdims=True)
        ssr = jnp.sum(jnp.where(is_rope, sq, 0.0), axis=-1, keepdims=True)
        invh = jnp.where(is_nope, lax.rsqrt(ssn * (1.0 / QK_NOPE) + EPS), lax.rsqrt(ssr * (1.0 / QK_ROPE) + EPS))
        q_ref[0, :, h * HEAD_PAD:(h + 1) * HEAD_PAD] = (invh * (a * tqc + s * tqs)).astype(BF16)

    ckv = (zkv * lax.rsqrt(jnp.mean(zkv * zkv, axis=-1, keepdims=True) + EPS) * gkv_ref[...]).astype(BF16)
    lane_h = lax.broadcasted_iota(jnp.int32, (1, MLA_HEADS * HEAD_PAD), 1) % HEAD_PAD
    vp = jnp.dot(ckv, wuv_ref[...], preferred_element_type=F32) + jnp.where(lane_h == V_HEAD, 1.0, 0.0)
    v_ref[0] = vp.T.astype(BF16)
    ka = jnp.dot(ckv, wuk_ref[...], preferred_element_type=F32)
    inv_r = lax.rsqrt(jnp.sum(kr_a * kr_a, axis=-1, keepdims=True) * (1.0 / QK_ROPE) + EPS)
    kr = inv_r * (kr_a * tkc_ref[...] + kr_s * tks_ref[...])
    gkn = gkn_ref[...]
    for h in range(MLA_HEADS):
        a = ka[:, h * HEAD_PAD:(h + 1) * HEAD_PAD]
        invh = lax.rsqrt(jnp.sum(a * a, axis=-1, keepdims=True) * (1.0 / QK_NOPE) + EPS)
        k_ref[0, :, h * HEAD_PAD:(h + 1) * HEAD_PAD] = (a * invh * gkn + kr).astype(BF16)

    z0_ref[0] = seg(0, d)
    gz_ref[0] = jax.nn.gelu(seg(d, o_q)).astype(BF16)
    sg_ref[0] = jax.nn.sigmoid(seg(o_g, o_g + 2 * d)).astype(BF16)


def _inproj(x, ctx, mod, g1, win_p, gq, wuq_p, wuq_s, gkv, wuk_p, wuv, gkn, tqc, tqs, tkc, tks, tile):
    b, s, d = x.shape
    lc = ctx.shape[1]
    assert lc == tile and s % tile == 0
    nl = s // tile
    ltot = s + lc
    n_in = win_p.shape[1]
    hp = MLA_HEADS * HEAD_PAD
    const = lambda bb, i: (0, 0)
    row = lambda bb, i: (bb, i, 0)
    tab = lambda bb, i: (i, 0)
    out_shapes = (
        jax.ShapeDtypeStruct((b, ltot, d), F32),
        jax.ShapeDtypeStruct((b, ltot, d), BF16),
        jax.ShapeDtypeStruct((b, ltot, hp), BF16),
        jax.ShapeDtypeStruct((b, ltot, hp), BF16),
        jax.ShapeDtypeStruct((b, hp, ltot), BF16),
        jax.ShapeDtypeStruct((b, ltot, 2 * d), BF16),
    )
    return pl.pallas_call(
        functools.partial(_inproj_kernel, d=d, n_lat_tiles=nl),
        grid=(b, nl + 1),
        in_specs=[
            pl.BlockSpec((1, tile, d), lambda bb, i: (bb, jnp.minimum(i, nl - 1), 0)),
            pl.BlockSpec((1, tile, d), lambda bb, i: (bb, 0, 0)),
            pl.BlockSpec((1, 1, N_MOD * d), lambda bb, i: (2 * bb + i // nl, 0, 0)),
            pl.BlockSpec((1, d), const),
            pl.BlockSpec((d, n_in), const),
            pl.BlockSpec((1, Q_LORA), const),
            pl.BlockSpec((Q_LORA, hp), const),
            pl.BlockSpec((Q_LORA, hp), const),
            pl.BlockSpec((1, KV_LORA), const),
            pl.BlockSpec((KV_LORA, hp), const),
            pl.BlockSpec((KV_LORA, hp), const),
            pl.BlockSpec((1, LANE), const),
            pl.BlockSpec((tile, LANE), tab),
            pl.BlockSpec((tile, LANE), tab),
            pl.BlockSpec((tile, LANE), tab),
            pl.BlockSpec((tile, LANE), tab),
        ],
        out_specs=[
            pl.BlockSpec((1, tile, d), row),
            pl.BlockSpec((1, tile, d), row),
            pl.BlockSpec((1, tile, hp), row),
            pl.BlockSpec((1, tile, hp), row),
            pl.BlockSpec((1, hp, tile), lambda bb, i: (bb, 0, i)),
            pl.BlockSpec((1, tile, 2 * d), row),
        ],
        out_shape=out_shapes,
        compiler_params=_cparams(("arbitrary", "arbitrary")),
        name="inproj",
    )(x, ctx, mod, g1, win_p, gq, wuq_p, wuq_s, gkv, wuk_p, wuv, gkn, tqc, tqs, tkc, tks)


def _sigmoid(x):
    return 0.5 * jnp.tanh(0.5 * x) + 0.5


def _softplus(x):
    e = jnp.exp(-jnp.abs(x))
    u = 1.0 + e
    tiny = u == 1.0
    log1p_e = jnp.where(tiny, e, jnp.log(u) * (e / jnp.where(tiny, 1.0, u - 1.0)))
    return jnp.maximum(x, 0.0) + log1p_e


def _rnn_kernel(*refs, d, tile, n_lat, reverse):
    if reverse:
        (xr_ref, wa_ref, wx_ref, ba_ref, bx_ref, lam_ref, hf_ref, gz_ref, o_ref, a_s, b_s, hb_s, h_s) = refs
    else:
        (z_ref, zp_ref, zn_ref, cw_ref, cb_ref, wa_ref, wx_ref, ba_ref, bx_ref, lam_ref,
         o_ref, xr_ref, a_s, b_s, h_s) = refs
    j = pl.program_id(1)
    if not reverse:
        c = jnp.where(j == 0, n_lat, j - 1)
        prev_zero = jnp.logical_or(c == 0, c == n_lat)
        next_zero = c >= n_lat - 1
        ext = jnp.concatenate([jnp.where(prev_zero, 0.0, zp_ref[0]), z_ref[0], jnp.where(next_zero, 0.0, zn_ref[0])],
                              axis=0)
        n_ext = tile + 2 * SUBLANE
        taps = [ext if (CONV_PAD_L - k) % n_ext == 0 else pltpu.roll(ext, (CONV_PAD_L - k) % n_ext, 0)
                for k in range(CONV_W)]
    pair = 2 * (d // RG_BLOCKS)
    neg_c_softplus = -RG_C * _softplus(-lam_ref[...])
    dst = hb_s if reverse else o_ref.at[0]

    h = jnp.where(j == 0, 0.0, h_s[...])
    subs = range(tile // RNN_SUB)
    for sb in (reversed(subs) if reverse else subs):
        r0 = sb * RNN_SUB
        if reverse:
            xr = xr_ref[0, r0:r0 + RNN_SUB, :]
        else:
            xr = cb_ref[...]
            for k in range(CONV_W):
                xr = xr + taps[k][SUBLANE + r0:SUBLANE + r0 + RNN_SUB] * cw_ref[k:k + 1, :]
            xr_ref[0, r0:r0 + RNN_SUB, :] = xr
        xb = xr.astype(BF16)
        ya = jnp.concatenate([jnp.dot(xb[:, m * pair:(m + 1) * pair], wa_ref[m], preferred_element_type=F32)
                              for m in range(d // pair)], axis=-1)
        yx = jnp.concatenate([jnp.dot(xb[:, m * pair:(m + 1) * pair], wx_ref[m], preferred_element_type=F32)
                              for m in range(d // pair)], axis=-1)
        r = _sigmoid(ya + ba_ref[...])
        gi = _sigmoid(yx + bx_ref[...])
        log_a = neg_c_softplus * r
        a_s[r0:r0 + RNN_SUB, :] = jnp.exp(log_a)
        th = jnp.tanh(log_a)
        v = -2.0 * th / (1.0 - th)
        b_s[r0:r0 + RNN_SUB, :] = (v * lax.rsqrt(jnp.maximum(v, 1e-30))) * (gi * xr)
        rows = range(r0, r0 + RNN_SUB)
        for t in (reversed(rows) if reverse else rows):
            h = a_s[t:t + 1, :] * h + b_s[t:t + 1, :]
            dst[t:t + 1, :] = h
    h_s[...] = h
    if reverse:
        o_ref[0] = ((hf_ref[0] + hb_s[...]) * gz_ref[0]).astype(BF16)


def _rnn(z, wa_p, wx_p, ba, bx, lam, tile, n_lat, reverse, conv=None, hf=None, gz=None):
    b, ltot, d = z.shape
    n_halo = ltot // SUBLANE
    per = tile // SUBLANE
    if reverse:
        cidx = lambda j: jnp.where(j == 0, n_lat, n_lat - j)
        oidx = lambda j: jnp.where(j == 0, n_lat - 1, n_lat - j)
    else:
        cidx = lambda j: jnp.where(j == 0, n_lat, j - 1)
        oidx = lambda j: jnp.maximum(j - 1, 0)
    const2 = lambda bb, j: (0, 0)
    const3 = lambda bb, j: (0, 0, 0)
    chunk = pl.BlockSpec((1, tile, d), lambda bb, j: (bb, cidx(j), 0))
    out_chunk = pl.BlockSpec((1, tile, d), lambda bb, j: (bb, oidx(j), 0))
    gate_specs = [pl.BlockSpec(wa_p.shape, const3), pl.BlockSpec(wx_p.shape, const3),
                  pl.BlockSpec((1, d), const2), pl.BlockSpec((1, d), const2), pl.BlockSpec((1, d), const2)]
    gate_args = [wa_p, wx_p, ba, bx, lam]
    scratch = [pltpu.VMEM((tile, d), F32), pltpu.VMEM((tile, d), F32)]
    if reverse:
        in_specs = [chunk] + gate_specs + [out_chunk, chunk]
        args = [z] + gate_args + [hf, gz]
        out_specs = out_chunk
        out_shape = jax.ShapeDtypeStruct((b, n_lat * tile, d), BF16)
        scratch += [pltpu.VMEM((tile, d), F32)]
    else:
        in_specs = [chunk,
                    pl.BlockSpec((1, SUBLANE, d), lambda bb, j: (bb, jnp.maximum(cidx(j) * per - 1, 0), 0)),
                    pl.BlockSpec((1, SUBLANE, d), lambda bb, j: (bb, jnp.minimum((cidx(j) + 1) * per, n_halo - 1), 0)),
                    pl.BlockSpec((CONV_W, d), const2), pl.BlockSpec((1, d), const2)] + gate_specs
        args = [z, z, z, conv[0], conv[1]] + gate_args
        out_specs = [out_chunk, chunk]
        out_shape = (jax.ShapeDtypeStruct((b, n_lat * tile, d), F32), jax.ShapeDtypeStruct((b, ltot, d), F32))
    scratch += [pltpu.VMEM((1, d), F32)]
    return pl.pallas_call(
        functools.partial(_rnn_kernel, d=d, tile=tile, n_lat=n_lat, reverse=reverse),
        grid=(b, n_lat + 1),
        in_specs=in_specs,
        out_specs=out_specs,
        out_shape=out_shape,
        scratch_shapes=scratch,
        compiler_params=_cparams(("arbitrary", "arbitrary")),
        name="rnn_bwd" if reverse else "rnn_fwd",
    )(*args)


def _attn_kernel(q_ref, k_ref, vt_ref, o_ref, s_even, s_odd, m_even, m_odd):
    g = pl.program_id(0)

    @pl.when(g == 0)
    def _():
        s_odd[...] = jnp.zeros_like(s_odd)
        m_odd[...] = jnp.zeros_like(m_odd)

    n_keys = k_ref.shape[1]
    tq = q_ref.shape[1]

    def step(s_new, m_new, s_prev, m_prev):
        heads = range(2)
        lanes = [slice(hh * HEAD_PAD, (hh + 1) * HEAD_PAD) for hh in heads]
        qs = [q_ref[0, :, lanes[hh]] for hh in heads]
        m_run = [None, None]
        acc = [jnp.zeros((HEAD_PAD, tq), F32) for _ in heads]
        for c in range(n_keys // MXU_COLS):
            rows = slice(c * MXU_COLS, (c + 1) * MXU_COLS)
            for hh in heads:
                st = lax.dot_general(k_ref[0, rows, lanes[hh]], qs[hh], (((1,), (1,)), ((), ())),
                                     preferred_element_type=F32)
                s_new[hh, rows, :] = st
                cm = jnp.max(st, axis=0, keepdims=True)
                m_run[hh] = cm if c == 0 else jnp.maximum(m_run[hh], cm)
                pt = jnp.exp2(s_prev[hh, rows, :] - m_prev[hh]).astype(BF16)
                acc[hh] = acc[hh] + jnp.dot(vt_ref[0, lanes[hh], rows], pt, preferred_element_type=F32)
        for hh in heads:
            m_new[hh] = m_run[hh]
        outs = [acc[hh][0:V_HEAD] / acc[hh][V_HEAD:V_HEAD + 1] for hh in heads]
        o_ref[0] = jnp.concatenate(outs, axis=0).T.astype(BF16)

    @pl.when(g % 2 == 0)
    def _():
        step(s_even, m_even, s_odd, m_odd)

    @pl.when(g % 2 == 1)
    def _():
        step(s_odd, m_odd, s_even, m_even)


def _attention(q, k, vt, s, tq):
    b, ltot, _ = q.shape
    pairs = MLA_HEADS // 2
    nq = s // tq
    n_items = b * pairs * nq

    def item(g):
        return g // (pairs * nq), (g // nq) % pairs, g % nq

    def cur(g):
        return item(jnp.minimum(g, n_items - 1))

    def prev(g):
        return item(jnp.maximum(g - 1, 0))

    scores = pltpu.VMEM((2, ltot, tq), F32)
    colmax = pltpu.VMEM((2, 1, tq), F32)
    return pl.pallas_call(
        _attn_kernel,
        grid=(n_items + 1,),
        in_specs=[pl.BlockSpec((1, tq, 2 * HEAD_PAD), lambda g: (cur(g)[0], cur(g)[2], cur(g)[1])),
                  pl.BlockSpec((1, ltot, 2 * HEAD_PAD), lambda g: (cur(g)[0], 0, cur(g)[1])),
                  pl.BlockSpec((1, 2 * HEAD_PAD, ltot), lambda g: (prev(g)[0], prev(g)[1], 0))],
        out_specs=pl.BlockSpec((1, tq, 2 * V_HEAD), lambda g: (prev(g)[0], prev(g)[2], prev(g)[1])),
        out_shape=jax.ShapeDtypeStruct((b, s, MLA_HEADS * V_HEAD), BF16),
        scratch_shapes=[scores, scores, colmax, colmax],
        compiler_params=_cparams(("arbitrary",)),
        name="attn",
    )(q, k, vt)


def _merge_kernel(rnn_ref, attn_ref, sg_ref, x_ref, mod_ref, modp_ref, g2_ref, wr_ref, wm_ref, wo_ref, wrt_ref,
                  xn_ref, h2_ref, aff_ref, xn_even, xn_odd, *, d):
    g = pl.program_id(0)

    @pl.when(g == 0)
    def _():
        xn_odd[...] = jnp.zeros_like(xn_odd)

    def step(xn_new, xn_prev):
        pr = jnp.dot(rnn_ref[0], wr_ref[...], preferred_element_type=F32)
        pa = jnp.dot(attn_ref[0], wm_ref[...], preferred_element_type=F32)
        merged = (sg_ref[0, :, 0:d] * pr + sg_ref[0, :, d:2 * d] * pa).astype(BF16)

        xp = xn_prev[...]
        inv = lax.rsqrt(jnp.mean(xp * xp, axis=-1, keepdims=True) + EPS)
        h2 = (xp * inv * g2_ref[...]) * (1.0 + modp_ref[0, :, 4 * d:5 * d]) + modp_ref[0, :, 3 * d:4 * d]
        n_tok = h2.shape[0]
        for t in range(d // LANE):
            h2_ref[0, pl.ds(t, n_tok, stride=d // LANE), :] = h2[:, t * LANE:(t + 1) * LANE]
        logits = jnp.dot(h2.astype(BF16), wrt_ref[...], preferred_element_type=F32)[:, 0:N_EXPERTS]
        e = jnp.exp(logits - jnp.max(logits, axis=-1, keepdims=True))
        aff_ref[0] = e / jnp.sum(e, axis=-1, keepdims=True)

        xn = x_ref[0] + mod_ref[0, :, 2 * d:3 * d] * jnp.dot(merged, wo_ref[...], preferred_element_type=F32)
        xn_ref[0] = xn
        xn_new[...] = xn

    @pl.when(g % 2 == 0)
    def _():
        step(xn_even, xn_odd)

    @pl.when(g % 2 == 1)
    def _():
        step(xn_odd, xn_even)


def _merge(rnn, attn, sg, x, mod, g2, wr, wm, wo, wrt, tile):
    b, s, d = x.shape
    nt = s // tile
    n_items = b * nt
    const = lambda g: (0, 0)
    cur = lambda g: jnp.minimum(g, n_items - 1)
    prev = lambda g: jnp.maximum(g - 1, 0)
    row = lambda g: (cur(g) // nt, cur(g) % nt, 0)
    row_prev = lambda g: (prev(g) // nt, prev(g) % nt, 0)
    return pl.pallas_call(
        functools.partial(_merge_kernel, d=d),
        grid=(n_items + 1,),
        in_specs=[pl.BlockSpec((1, tile, d), row),
                  pl.BlockSpec((1, tile, attn.shape[2]), row),
                  pl.BlockSpec((1, tile, 2 * d), row),
                  pl.BlockSpec((1, tile, d), row),
                  pl.BlockSpec((1, 1, N_MOD * d), lambda g: (2 * (cur(g) // nt), 0, 0)),
                  pl.BlockSpec((1, 1, N_MOD * d), lambda g: (2 * (prev(g) // nt), 0, 0)),
                  pl.BlockSpec((1, d), const),
                  pl.BlockSpec(wr.shape, const),
                  pl.BlockSpec(wm.shape, const),
                  pl.BlockSpec(wo.shape, const),
                  pl.BlockSpec(wrt.shape, const)],
        out_specs=[pl.BlockSpec((1, tile, d), row),
                   pl.BlockSpec((1, tile * (d // LANE), LANE), row_prev),
                   pl.BlockSpec((1, tile, N_EXPERTS), row_prev)],
        out_shape=(jax.ShapeDtypeStruct((b, s, d), F32),
                   jax.ShapeDtypeStruct((b, s * (d // LANE), LANE), F32),
                   jax.ShapeDtypeStruct((b, s, N_EXPERTS), F32)),
        scratch_shapes=[pltpu.VMEM((tile, d), F32), pltpu.VMEM((tile, d), F32)],
        compiler_params=_cparams(("arbitrary",)),
        name="merge",
    )(rnn, attn, sg, x, mod, mod, g2, wr, wm, wo, wrt)


def _excl_cumsum_lanes(x01, tri):
    rows, length = x01.shape
    offset = jnp.zeros((rows, 1), F32)
    pieces = []
    for blk in range(length // LANE):
        xb = x01[:, blk * LANE:(blk + 1) * LANE]
        inc = jnp.dot(xb.astype(BF16), tri, preferred_element_type=F32)
        pieces.append(inc - xb + offset)
        offset = offset + jnp.sum(xb, axis=-1, keepdims=True)
    return jnp.concatenate(pieces, axis=-1)


def _topk_kernel(aff_ref, idx_ref, gate_ref, pos_s, sel_s, *, cap):
    a = aff_ref[0]
    n_e, length = a.shape

    def as_float(bits):
        return lax.bitcast_convert_type(bits, F32)

    def bisect(i, cand):
        t = cand | (jnp.int32(1) << (30 - i))
        cnt = jnp.sum((a >= as_float(t)).astype(jnp.int32), axis=-1, keepdims=True)
        return jnp.where(cnt >= cap, t, cand)

    thr_bits = lax.fori_loop(0, 31, bisect, jnp.zeros((n_e, 1), jnp.int32))
    thr = as_float(thr_bits)
    above = as_float(thr_bits + 1)
    gt = (a >= above).astype(F32)
    eq = jnp.logical_and(a >= thr, a < above).astype(F32)
    need = cap - jnp.sum(gt, axis=-1, keepdims=True)
    r_i = lax.broadcasted_iota(jnp.int32, (LANE, LANE), 0)
    c_i = lax.broadcasted_iota(jnp.int32, (LANE, LANE), 1)
    tri = (r_i <= c_i).astype(BF16)
    tie_rank = _excl_cumsum_lanes(eq, tri)
    sel = gt + eq * (tie_rank < need).astype(F32)
    pos_s[...] = _excl_cumsum_lanes(sel, tri)
    sel_s[...] = sel

    n_hi = idx_ref.shape[2]
    tok = lax.broadcasted_iota(jnp.int32, (1, length), 1)
    tok_hi = (tok >> 6).astype(F32)
    tok_lo = (tok & 63).astype(F32)
    a_iota = lax.broadcasted_iota(jnp.int32, (n_hi, 1), 0)
    c_iota = lax.broadcasted_iota(jnp.int32, (SLOT_LO, 1), 0)

    def compact(e, carry):
        pos_row = pos_s[pl.ds(e, 1), :].astype(jnp.int32)
        sel_row = sel_s[pl.ds(e, 1), :] > 0.5
        hot_a = jnp.where(jnp.logical_and((pos_row // SLOT_LO) == a_iota, sel_row), 1.0, 0.0)
        hot_c = jnp.where((pos_row % SLOT_LO) == c_iota, 1.0, 0.0).astype(BF16)
        g = aff_ref[0, pl.ds(e, 1), :]
        g1 = g.astype(BF16).astype(F32)
        g2 = (g - g1).astype(BF16).astype(F32)
        g3 = g - g1 - g2
        lhs = jnp.concatenate([hot_a * v for v in (tok_hi, tok_lo, g1, g2, g3)], axis=0).astype(BF16)
        res = lax.dot_general(lhs, hot_c, (((1,), (1,)), ((), ())), preferred_element_type=F32)
        part = [res[v * n_hi:(v + 1) * n_hi] for v in range(5)]
        idx_ref[0, e] = (part[0] * 64.0 + part[1]).astype(jnp.int32)
        gate_ref[0, e] = part[2] + part[3] + part[4]
        return carry

    lax.fori_loop(0, n_e, compact, 0)


def _topk(aff_t, cap):
    b, n_e, length = aff_t.shape
    n_hi = max(cap // SLOT_LO, SUBLANE)
    blk = lambda bb: (bb, 0, 0)
    blk4 = lambda bb: (bb, 0, 0, 0)
    idx, gate = pl.pallas_call(
        functools.partial(_topk_kernel, cap=cap),
        grid=(b,),
        in_specs=[pl.BlockSpec((1, n_e, length), blk)],
        out_specs=[pl.BlockSpec((1, n_e, n_hi, SLOT_LO), blk4), pl.BlockSpec((1, n_e, n_hi, SLOT_LO), blk4)],
        out_shape=(jax.ShapeDtypeStruct((b, n_e, n_hi, SLOT_LO), jnp.int32),
                   jax.ShapeDtypeStruct((b, n_e, n_hi, SLOT_LO), F32)),
        scratch_shapes=[pltpu.VMEM((n_e, length), F32), pltpu.VMEM((n_e, length), F32)],
        compiler_params=_cparams(("arbitrary",)),
        name="topk",
    )(aff_t)
    flat = lambda a: a.reshape(b * n_e, 1, n_hi * SLOT_LO)[:, :, :cap]
    return flat(idx), flat(gate)


def _gather_kernel(idx_ref, h_ref, o_ref, rows_s, *, cap, d):
    tiles = d // LANE

    def body(j, carry):
        src = pl.multiple_of(idx_ref[0, 0, j] * tiles, tiles)
        dst = pl.multiple_of(j * tiles, tiles)
        rows_s[pl.ds(dst, tiles), :] = h_ref[0, pl.ds(src, tiles), :]
        return carry

    lax.fori_loop(0, cap, body, 0, unroll=8)
    o_ref[0, 0] = jnp.concatenate([rows_s[pl.ds(t, cap, stride=tiles), :] for t in range(tiles)],
                                  axis=-1).astype(BF16)


def _gather(idx, h2t, cap, d):
    b, rows, lanes = h2t.shape
    assert d // lanes == SUBLANE and lanes == LANE
    return pl.pallas_call(
        functools.partial(_gather_kernel, cap=cap, d=d),
        grid=(b, N_EXPERTS),
        in_specs=[pl.BlockSpec((1, 1, cap), lambda bb, e: (bb * N_EXPERTS + e, 0, 0), memory_space=pltpu.SMEM),
                  pl.BlockSpec((1, rows, lanes), lambda bb, e: (bb, 0, 0))],
        out_specs=pl.BlockSpec((1, 1, cap, d), lambda bb, e: (bb, e, 0, 0)),
        out_shape=jax.ShapeDtypeStruct((b, N_EXPERTS, cap, d), BF16),
        scratch_shapes=[pltpu.VMEM((cap * (d // LANE), LANE), F32)],
        compiler_params=_cparams(("arbitrary", "arbitrary")),
        name="gather",
    )(idx, h2t)


def _moe_kernel(idx_ref, gate_ref, x_ref, wg_ref, wu_ref, wd_ref, mod_ref, res_ref, o_ref, y_even, y_odd, *, d, cap):
    g = pl.program_id(0)
    e_prev = jnp.maximum(g - 1, 0) % N_EXPERTS
    piece = res_ref.shape[1]

    @pl.when(g == 0)
    def _():
        y_odd[...] = jnp.zeros_like(y_odd)

    @pl.when(e_prev == 0)
    def _():
        o_ref[...] = jnp.zeros_like(o_ref)

    def step(y_new, y_prev):
        xb = x_ref[0, 0]
        gg = jnp.dot(xb, wg_ref[0], preferred_element_type=F32)
        uu = jnp.dot(xb, wu_ref[0], preferred_element_type=F32)
        hid = (gg * jax.nn.sigmoid(gg) * uu).astype(BF16)
        y_new[...] = jnp.dot(hid, wd_ref[0], preferred_element_type=F32) * mod_ref[0, :, 5 * d:6 * d]

        rows = pl.ds(pl.multiple_of(e_prev * piece, piece), piece)
        o_ref[0, rows, :] = o_ref[0, rows, :] + res_ref[0]
        for j in range(cap):
            r = idx_ref[0, 0, j]
            o_ref[0, pl.ds(r, 1), :] = o_ref[0, pl.ds(r, 1), :] + gate_ref[0, 0, j] * y_prev[j:j + 1, :]

    @pl.when(g % 2 == 0)
    def _():
        step(y_even, y_odd)

    @pl.when(g % 2 == 1)
    def _():
        step(y_odd, y_even)


def _moe(idx, gate, xg, w_gate, w_up, w_down, mod, x_new):
    b, n_e, cap, d = xg.shape
    s = x_new.shape[1]
    f = w_gate.shape[2]
    n_items = b * n_e
    cur = lambda g: jnp.minimum(g, n_items - 1)
    prev = lambda g: jnp.maximum(g - 1, 0)
    smem = lambda: pl.BlockSpec((1, 1, cap), lambda g: (prev(g), 0, 0), memory_space=pltpu.SMEM)
    wspec = lambda shape: pl.BlockSpec(shape, lambda g: (cur(g) % n_e, 0, 0))
    return pl.pallas_call(
        functools.partial(_moe_kernel, d=d, cap=cap),
        grid=(n_items + 1,),
        in_specs=[smem(), smem(),
                  pl.BlockSpec((1, 1, cap, d), lambda g: (cur(g) // n_e, cur(g) % n_e, 0, 0)),
                  wspec((1, d, f)), wspec((1, d, f)), wspec((1, f, d)),
                  pl.BlockSpec((1, 1, N_MOD * d), lambda g: (2 * (cur(g) // n_e), 0, 0)),
                  pl.BlockSpec((1, s // n_e, d), lambda g: (prev(g) // n_e, prev(g) % n_e, 0))],
        out_specs=pl.BlockSpec((1, s, d), lambda g: (prev(g) // n_e, 0, 0), pipeline_mode=pl.Buffered(1)),
        out_shape=jax.ShapeDtypeStruct((b, s, d), F32),
        scratch_shapes=[pltpu.VMEM((cap, d), F32), pltpu.VMEM((cap, d), F32)],
        compiler_params=_cparams(("arbitrary",)),
        name="moe",
    )(idx, gate, xg, w_gate, w_up, w_down, mod, x_new)


def _pad_lanes(a, width):
    return jnp.pad(a, [(0, 0)] * (a.ndim - 1) + [(0, width - a.shape[-1])])


def _rope_tables(s, lc, g_q_nope, g_q_rope, g_k_nope, g_k_rope):
    rows = s // GRID_W
    row = jnp.repeat(jnp.arange(rows, dtype=F32), GRID_W)
    col = jnp.tile(jnp.arange(GRID_W, dtype=F32), rows)
    n_pairs = QK_ROPE // 4
    inv_freq = ROPE_BASE ** (-jnp.arange(n_pairs, dtype=F32) / n_pairs)
    ang = jnp.concatenate([row[:, None] * inv_freq, col[:, None] * inv_freq], axis=-1)
    cos = jnp.concatenate([jnp.cos(ang), jnp.ones((lc, HALF_ROPE), F32)], axis=0)
    sin = jnp.concatenate([jnp.sin(ang), jnp.zeros((lc, HALF_ROPE), F32)], axis=0)
    ltot = s + lc
    ones = jnp.ones((ltot, 1), F32)

    def tables(g_nope, g_rope, scale):
        g1, g2 = g_rope[:HALF_ROPE], g_rope[HALF_ROPE:]
        tc = jnp.concatenate([ones * g_nope[None, :], cos * g1, cos * g2], axis=-1) * scale
        ts = jnp.concatenate([jnp.zeros((ltot, QK_NOPE), F32), -sin * g2, sin * g1], axis=-1) * scale
        return _pad_lanes(tc, LANE), _pad_lanes(ts, LANE)

    tqc, tqs = tables(g_q_nope, g_q_rope, ATTN_SCALE * LOG2_E)
    tkc, tks = tables(jnp.zeros((QK_NOPE,), F32), g_k_rope, 1.0)
    gkn = _pad_lanes(g_k_nope[None, :], LANE)
    return tqc, tqs, tkc, tks, gkn


def _swap_halves(a):
    return jnp.concatenate([a[..., HALF_ROPE:], a[..., :HALF_ROPE]], axis=-1)


def _layout_weights(w_in, w_uq, w_uk, w_uv, d):
    o_q = 2 * d
    o_kv = o_q + Q_LORA
    o_kr = o_kv + KV_LORA
    o_g = o_kr + QK_ROPE
    kr = w_in[:, o_kr:o_g]
    zl = jnp.zeros((w_in.shape[0], QK_NOPE), F32)
    kr_a = _pad_lanes(jnp.concatenate([zl, kr], axis=-1), LANE)
    kr_s = _pad_lanes(jnp.concatenate([zl, _swap_halves(kr)], axis=-1), LANE)
    win_p = jnp.concatenate([w_in[:, :o_kr], kr_a, kr_s, w_in[:, o_g:]], axis=-1).astype(BF16)

    uq = w_uq.reshape(Q_LORA, MLA_HEADS, QK_NOPE + QK_ROPE)
    nope, rope = uq[..., :QK_NOPE], uq[..., QK_NOPE:]
    wuq_p = _pad_lanes(jnp.concatenate([nope, rope], axis=-1), HEAD_PAD).reshape(Q_LORA, -1).astype(BF16)
    wuq_s = _pad_lanes(jnp.concatenate([jnp.zeros_like(nope), _swap_halves(rope)], axis=-1),
                       HEAD_PAD).reshape(Q_LORA, -1).astype(BF16)
    wuk_p = _pad_lanes(w_uk.reshape(KV_LORA, MLA_HEADS, QK_NOPE), HEAD_PAD).reshape(KV_LORA, -1).astype(BF16)
    wuv_p = _pad_lanes(w_uv.reshape(KV_LORA, MLA_HEADS, V_HEAD), HEAD_PAD).reshape(KV_LORA, -1).astype(BF16)
    return win_p, wuq_p, wuq_s, wuk_p, wuv_p


def _pair_blocks(w):
    nb, n, _ = w.shape
    w = w.reshape(nb // 2, 2, n, n)
    z = jnp.zeros((nb // 2, n, n), w.dtype)
    top = jnp.concatenate([w[:, 0], z], axis=-1)
    bot = jnp.concatenate([z, w[:, 1]], axis=-1)
    return jnp.concatenate([top, bot], axis=-2).astype(BF16)


def kernel(x, c, ctx, c_ctx, w_ada, b_ada, g_norm1, g_norm2, w_in, conv_w, conv_b, rg_wa, rg_ba, rg_wx, rg_bx,
           rg_lambda, w_rnn_out, g_q_lora, w_uq, g_kv_lora, w_uk, w_uv, g_q_nope, g_q_rope, g_k_nope, g_k_rope,
           w_mla_out, w_o, w_router, w_e_gate, w_e_up, w_e_down):
    assert w_ada.shape[0] == 1, "single-layer kernel"
    b, s, d = x.shape
    lc = ctx.shape[1]
    tile = lc
    n_lat = s // tile
    cap = CAPACITY_FACTOR * s // N_EXPERTS

    pad_rows = (-(b + 1)) % SUBLANE
    cc = jnp.concatenate([c, c_ctx[None, :], jnp.zeros((pad_rows, d), F32)], axis=0)
    mod_all = _ada(cc, w_ada[0], b_ada[0])
    mod = jnp.stack([mod_all[:b], jnp.broadcast_to(mod_all[b], (b, N_MOD * d))], axis=1).reshape(2 * b, 1, N_MOD * d)

    win_p, wuq_p, wuq_s, wuk_p, wuv_p = _layout_weights(w_in[0], w_uq[0], w_uk[0], w_uv[0], d)
    tqc, tqs, tkc, tks, gkn = _rope_tables(s, lc, g_q_nope[0], g_q_rope[0], g_k_nope[0], g_k_rope[0])
    z0, gz, q, k, v, sg = _inproj(x, ctx, mod, g_norm1, win_p, g_q_lora, wuq_p, wuq_s, g_kv_lora, wuk_p,
                                  wuv_p, gkn, tqc, tqs, tkc, tks, tile)

    h_f, xr = _rnn(z0, _pair_blocks(rg_wa[0, 0]), _pair_blocks(rg_wx[0, 0]), rg_ba[0, 0:1], rg_bx[0, 0:1],
                   rg_lambda[0, 0:1], tile, n_lat, False, conv=(conv_w[0], conv_b))
    rnn = _rnn(xr, _pair_blocks(rg_wa[0, 1]), _pair_blocks(rg_wx[0, 1]), rg_ba[0, 1:2], rg_bx[0, 1:2],
               rg_lambda[0, 1:2], tile, n_lat, True, hf=h_f, gz=gz)

    attn = _attention(q, k, v, s, 2 * MXU_COLS)

    x_new, h2, aff = _merge(rnn, attn, sg, x, mod, g_norm2, w_rnn_out[0].astype(BF16), w_mla_out[0].astype(BF16),
                            w_o[0].astype(BF16), _pad_lanes(w_router[0], LANE).astype(BF16), 2 * tile)

    idx, gate = _topk(jnp.swapaxes(aff, 1, 2), cap)
    xg = _gather(idx, h2, cap, d)
    return _moe(idx, gate, xg, w_e_gate[0].astype(BF16), w_e_up[0].astype(BF16), w_e_down[0].astype(BF16), mod, x_new)
```

```python
import functools

import jax
import jax.numpy as jnp
from jax import lax
from jax.experimental import pallas as pl
from jax.experimental.pallas import tpu as pltpu

EPS = 1e-6
N_MOD = 6
GRID_W = 64
RG_BLOCKS = 8
CONV_W = 4
CONV_PAD_L = 2
RG_C = 8.0
MLA_HEADS = 8
QK_NOPE = 64
QK_ROPE = 32
V_HEAD = 64
Q_LORA = 256
KV_LORA = 128
ROPE_BASE = 10000.0
ATTN_SCALE = (QK_NOPE + QK_ROPE) ** -0.5
LOG2_E = 1.4426950408889634
N_EXPERTS = 16
CAPACITY_FACTOR = 2

LANE = 128
SUBLANE = 8
MXU_COLS = 256
HEAD_PAD = LANE
HALF_ROPE = QK_ROPE // 2
RNN_SUB = 64
SLOT_LO = 32
VMEM_LIMIT = 56 * 1024 * 1024

BF16 = jnp.bfloat16
F32 = jnp.float32


def _cparams(sem):
    return pltpu.CompilerParams(dimension_semantics=sem, vmem_limit_bytes=VMEM_LIMIT)


def _ada_kernel(c_ref, w_ref, b_ref, o_ref):
    cc = c_ref[...]
    s = cc * jax.nn.sigmoid(cc)
    o_ref[...] = jnp.dot(s, w_ref[...], preferred_element_type=F32) + b_ref[...]


def _ada(cc, w_ada, b_ada):
    rows, d = cc.shape
    n = w_ada.shape[1]
    bn = d
    return pl.pallas_call(
        _ada_kernel,
        grid=(n // bn,),
        in_specs=[pl.BlockSpec((rows, d), lambda j: (0, 0)),
                  pl.BlockSpec((d, bn), lambda j: (0, j)),
                  pl.BlockSpec((1, bn), lambda j: (0, j))],
        out_specs=pl.BlockSpec((rows, bn), lambda j: (0, j)),
        out_shape=jax.ShapeDtypeStruct((rows, n), F32),
        compiler_params=_cparams(("arbitrary",)),
        name="ada",
    )(cc, w_ada, b_ada.reshape(1, n))


def _inproj_kernel(x_ref, ctx_ref, mod_ref, g1_ref, win_ref, gq_ref, wuq_ref, wuqs_ref, gkv_ref, wuk_ref,
                   wuv_ref, gkn_ref, tqc_ref, tqs_ref, tkc_ref, tks_ref,
                   z0_ref, gz_ref, q_ref, k_ref, v_ref, sg_ref, *, d, n_lat_tiles):
    i = pl.program_id(1)
    is_ctx = i == n_lat_tiles
    xt = jnp.where(is_ctx, ctx_ref[0], x_ref[0])
    shift = mod_ref[0, :, 0:d]
    scale = mod_ref[0, :, d:2 * d]
    inv = lax.rsqrt(jnp.mean(xt * xt, axis=-1, keepdims=True) + EPS)
    xn = ((xt * inv * g1_ref[...]) * (1.0 + scale) + shift).astype(BF16)

    def seg(lo, hi):
        return jnp.dot(xn, win_ref[:, lo:hi], preferred_element_type=F32)

    o_q = 2 * d
    o_kv = o_q + Q_LORA
    o_kr = o_kv + KV_LORA
    o_g = o_kr + 2 * LANE
    zq = seg(o_q, o_kv)
    zkv = seg(o_kv, o_kr)
    kr_a = seg(o_kr, o_kr + LANE)
    kr_s = seg(o_kr + LANE, o_g)

    lane = lax.broadcasted_iota(jnp.int32, (1, LANE), 1)
    is_nope = lane < QK_NOPE
    is_rope = jnp.logical_and(lane >= QK_NOPE, lane < QK_NOPE + QK_ROPE)

    zqn = (zq * lax.rsqrt(jnp.mean(zq * zq, axis=-1, keepdims=True) + EPS) * gq_ref[...]).astype(BF16)
    qa = jnp.dot(zqn, wuq_ref[...], preferred_element_type=F32)
    qs = jnp.dot(zqn, wuqs_ref[...], preferred_element_type=F32)
    tqc = tqc_ref[...]
    tqs = tqs_ref[...]
    for h in range(MLA_HEADS):
        a = qa[:, h * HEAD_PAD:(h + 1) * HEAD_PAD]
        s = qs[:, h * HEAD_PAD:(h + 1) * HEAD_PAD]
        sq = a * a
        ssn = jnp.sum(jnp.where(is_nope, sq, 0.0), axis=-1, keepdims=True)
        ssr = jnp.sum(jnp.where(is_rope, sq, 0.0), axis=-1, keepdims=True)
        invh = jnp.where(is_nope, lax.rsqrt(ssn * (1.0 / QK_NOPE) + EPS), lax.rsqrt(ssr * (1.0 / QK_ROPE) + EPS))
        q_ref[0, :, h * HEAD_PAD:(h + 1) * HEAD_PAD] = (invh * (a * tqc + s * tqs)).astype(BF16)

    ckv = (zkv * lax.rsqrt(jnp.mean(zkv * zkv, axis=-1, keepdims=True) + EPS) * gkv_ref[...]).astype(BF16)
    lane_h = lax.broadcasted_iota(jnp.int32, (1, MLA_HEADS * HEAD_PAD), 1) % HEAD_PAD
    vp = jnp.dot(ckv, wuv_ref[...], preferred_element_type=F32) + jnp.where(lane_h == V_HEAD, 1.0, 0.0)
    v_ref[0] = vp.T.astype(BF16)
    ka = jnp.dot(ckv, wuk_ref[...], preferred_element_type=F32)
    inv_r = lax.rsqrt(jnp.sum(kr_a * kr_a, axis=-1, keepdims=True) * (1.0 / QK_ROPE) + EPS)
    kr = inv_r * (kr_a * tkc_ref[...] + kr_s * tks_ref[...])
    gkn = gkn_ref[...]
    for h in range(MLA_HEADS):
        a = ka[:, h * HEAD_PAD:(h + 1) * HEAD_PAD]
        invh = lax.rsqrt(jnp.sum(a * a, axis=-1, keepdims=True) * (1.0 / QK_NOPE) + EPS)
        k_ref[0, :, h * HEAD_PAD:(h + 1) * HEAD_PAD] = (a * invh * gkn + kr).astype(BF16)

    z0_ref[0] = seg(0, d)
    gz_ref[0] = jax.nn.gelu(seg(d, o_q)).astype(BF16)
    sg_ref[0] = jax.nn.sigmoid(seg(o_g, o_g + 2 * d)).astype(BF16)


def _inproj(x, ctx, mod, g1, win_p, gq, wuq_p, wuq_s, gkv, wuk_p, wuv, gkn, tqc, tqs, tkc, tks, tile):
    b, s, d = x.shape
    lc = ctx.shape[1]
    assert lc == tile and s % tile == 0
    nl = s // tile
    ltot = s + lc
    n_in = win_p.shape[1]
    hp = MLA_HEADS * HEAD_PAD
    const = lambda bb, i: (0, 0)
    row = lambda bb, i: (bb, i, 0)
    tab = lambda bb, i: (i, 0)
    out_shapes = (
        jax.ShapeDtypeStruct((b, ltot, d), F32),
        jax.ShapeDtypeStruct((b, ltot, d), BF16),
        jax.ShapeDtypeStruct((b, ltot, hp), BF16),
        jax.ShapeDtypeStruct((b, ltot, hp), BF16),
        jax.ShapeDtypeStruct((b, hp, ltot), BF16),
        jax.ShapeDtypeStruct((b, ltot, 2 * d), BF16),
    )
    return pl.pallas_call(
        functools.partial(_inproj_kernel, d=d, n_lat_tiles=nl),
        grid=(b, nl + 1),
        in_specs=[
            pl.BlockSpec((1, tile, d), lambda bb, i: (bb, jnp.minimum(i, nl - 1), 0)),
            pl.BlockSpec((1, tile, d), lambda bb, i: (bb, 0, 0)),
            pl.BlockSpec((1, 1, N_MOD * d), lambda bb, i: (2 * bb + i // nl, 0, 0)),
            pl.BlockSpec((1, d), const),
            pl.BlockSpec((d, n_in), const),
            pl.BlockSpec((1, Q_LORA), const),
            pl.BlockSpec((Q_LORA, hp), const),
            pl.BlockSpec((Q_LORA, hp), const),
            pl.BlockSpec((1, KV_LORA), const),
            pl.BlockSpec((KV_LORA, hp), const),
            pl.BlockSpec((KV_LORA, hp), const),
            pl.BlockSpec((1, LANE), const),
            pl.BlockSpec((tile, LANE), tab),
            pl.BlockSpec((tile, LANE), tab),
            pl.BlockSpec((tile, LANE), tab),
            pl.BlockSpec((tile, LANE), tab),
        ],
        out_specs=[
            pl.BlockSpec((1, tile, d), row),
            pl.BlockSpec((1, tile, d), row),
            pl.BlockSpec((1, tile, hp), row),
            pl.BlockSpec((1, tile, hp), row),
            pl.BlockSpec((1, hp, tile), lambda bb, i: (bb, 0, i)),
            pl.BlockSpec((1, tile, 2 * d), row),
        ],
        out_shape=out_shapes,
        compiler_params=_cparams(("arbitrary", "arbitrary")),
        name="inproj",
    )(x, ctx, mod, g1, win_p, gq, wuq_p, wuq_s, gkv, wuk_p, wuv, gkn, tqc, tqs, tkc, tks)


def _sigmoid(x):
    return 0.5 * jnp.tanh(0.5 * x) + 0.5


def _softplus(x):
    e = jnp.exp(-jnp.abs(x))
    u = 1.0 + e
    tiny = u == 1.0
    log1p_e = jnp.where(tiny, e, jnp.log(u) * (e / jnp.where(tiny, 1.0, u - 1.0)))
    return jnp.maximum(x, 0.0) + log1p_e


def _rnn_kernel(*refs, d, tile, n_lat, reverse):
    if reverse:
        (xr_ref, wa_ref, wx_ref, ba_ref, bx_ref, lam_ref, hf_ref, gz_ref, o_ref, a_s, b_s, hb_s, h_s) = refs
    else:
        (z_ref, zp_ref, zn_ref, cw_ref, cb_ref, wa_ref, wx_ref, ba_ref, bx_ref, lam_ref,
         o_ref, xr_ref, a_s, b_s, h_s) = refs
    j = pl.program_id(1)
    if not reverse:
        c = jnp.where(j == 0, n_lat, j - 1)
        prev_zero = jnp.logical_or(c == 0, c == n_lat)
        next_zero = c >= n_lat - 1
        ext = jnp.concatenate([jnp.where(prev_zero, 0.0, zp_ref[0]), z_ref[0], jnp.where(next_zero, 0.0, zn_ref[0])],
                              axis=0)
        n_ext = tile + 2 * SUBLANE
        taps = [ext if (CONV_PAD_L - k) % n_ext == 0 else pltpu.roll(ext, (CONV_PAD_L - k) % n_ext, 0)
                for k in range(CONV_W)]
    pair = 2 * (d // RG_BLOCKS)
    neg_c_softplus = -RG_C * _softplus(-lam_ref[...])
    dst = hb_s if reverse else o_ref.at[0]

    h = jnp.where(j == 0, 0.0, h_s[...])
    subs = range(tile // RNN_SUB)
    for sb in (reversed(subs) if reverse else subs):
        r0 = sb * RNN_SUB
        if reverse:
            xr = xr_ref[0, r0:r0 + RNN_SUB, :]
        else:
            xr = cb_ref[...]
            for k in range(CONV_W):
                xr = xr + taps[k][SUBLANE + r0:SUBLANE + r0 + RNN_SUB] * cw_ref[k:k + 1, :]
            xr_ref[0, r0:r0 + RNN_SUB, :] = xr
        xb = xr.astype(BF16)
        ya = jnp.concatenate([jnp.dot(xb[:, m * pair:(m + 1) * pair], wa_ref[m], preferred_element_type=F32)
                              for m in range(d // pair)], axis=-1)
        yx = jnp.concatenate([jnp.dot(xb[:, m * pair:(m + 1) * pair], wx_ref[m], preferred_element_type=F32)
                              for m in range(d // pair)], axis=-1)
        r = _sigmoid(ya + ba_ref[...])
        gi = _sigmoid(yx + bx_ref[...])
        log_a = neg_c_softplus * r
        a_s[r0:r0 + RNN_SUB, :] = jnp.exp(log_a)
        th = jnp.tanh(log_a)
        v = -2.0 * th / (1.0 - th)
        b_s[r0:r0 + RNN_SUB, :] = (v * lax.rsqrt(jnp.maximum(v, 1e-30))) * (gi * xr)
        rows = range(r0, r0 + RNN_SUB)
        for t in (reversed(rows) if reverse else rows):
            h = a_s[t:t + 1, :] * h + b_s[t:t + 1, :]
            dst[t:t + 1, :] = h
    h_s[...] = h
    if reverse:
        o_ref[0] = ((hf_ref[0] + hb_s[...]) * gz_ref[0]).astype(BF16)


def _rnn(z, wa_p, wx_p, ba, bx, lam, tile, n_lat, reverse, conv=None, hf=None, gz=None):
    b, ltot, d = z.shape
    n_halo = ltot // SUBLANE
    per = tile // SUBLANE
    if reverse:
        cidx = lambda j: jnp.where(j == 0, n_lat, n_lat - j)
        oidx = lambda j: jnp.where(j == 0, n_lat - 1, n_lat - j)
    else:
        cidx = lambda j: jnp.where(j == 0, n_lat, j - 1)
        oidx = lambda j: jnp.maximum(j - 1, 0)
    const2 = lambda bb, j: (0, 0)
    const3 = lambda bb, j: (0, 0, 0)
    chunk = pl.BlockSpec((1, tile, d), lambda bb, j: (bb, cidx(j), 0))
    out_chunk = pl.BlockSpec((1, tile, d), lambda bb, j: (bb, oidx(j), 0))
    gate_specs = [pl.BlockSpec(wa_p.shape, const3), pl.BlockSpec(wx_p.shape, const3),
                  pl.BlockSpec((1, d), const2), pl.BlockSpec((1, d), const2), pl.BlockSpec((1, d), const2)]
    gate_args = [wa_p, wx_p, ba, bx, lam]
    scratch = [pltpu.VMEM((tile, d), F32), pltpu.VMEM((tile, d), F32)]
    if reverse:
        in_specs = [chunk] + gate_specs + [out_chunk, chunk]
        args = [z] + gate_args + [hf, gz]
        out_specs = out_chunk
        out_shape = jax.ShapeDtypeStruct((b, n_lat * tile, d), BF16)
        scratch += [pltpu.VMEM((tile, d), F32)]
    else:
        in_specs = [chunk,
                    pl.BlockSpec((1, SUBLANE, d), lambda bb, j: (bb, jnp.maximum(cidx(j) * per - 1, 0), 0)),
                    pl.BlockSpec((1, SUBLANE, d), lambda bb, j: (bb, jnp.minimum((cidx(j) + 1) * per, n_halo - 1), 0)),
                    pl.BlockSpec((CONV_W, d), const2), pl.BlockSpec((1, d), const2)] + gate_specs
        args = [z, z, z, conv[0], conv[1]] + gate_args
        out_specs = [out_chunk, chunk]
        out_shape = (jax.ShapeDtypeStruct((b, n_lat * tile, d), F32), jax.ShapeDtypeStruct((b, ltot, d), F32))
    scratch += [pltpu.VMEM((1, d), F32)]
    return pl.pallas_call(
        functools.partial(_rnn_kernel, d=d, tile=tile, n_lat=n_lat, reverse=reverse),
        grid=(b, n_lat + 1),
        in_specs=in_specs,
        out_specs=out_specs,
        out_shape=out_shape,
        scratch_shapes=scratch,
        compiler_params=_cparams(("arbitrary", "arbitrary")),
        name="rnn_bwd" if reverse else "rnn_fwd",
    )(*args)


def _attn_kernel(q_ref, k_ref, vt_ref, *rest, n_cast):
    w_refs = rest[:n_cast]
    o_ref = rest[n_cast]
    wb_refs = rest[n_cast + 1:2 * n_cast + 1]
    s_even, s_odd, m_even, m_odd = rest[2 * n_cast + 1:]
    g = pl.program_id(0)

    @pl.when(g == 0)
    def _():
        s_odd[...] = jnp.zeros_like(s_odd)
        m_odd[...] = jnp.zeros_like(m_odd)

    n_keys = k_ref.shape[1]
    tq = q_ref.shape[1]

    def step(s_new, m_new, s_prev, m_prev):
        heads = range(2)
        lanes = [slice(hh * HEAD_PAD, (hh + 1) * HEAD_PAD) for hh in heads]
        qs = [q_ref[0, :, lanes[hh]] for hh in heads]
        m_run = [None, None]
        acc = [jnp.zeros((HEAD_PAD, tq), F32) for _ in heads]
        for c in range(n_keys // MXU_COLS):
            rows = slice(c * MXU_COLS, (c + 1) * MXU_COLS)
            for hh in heads:
                st = lax.dot_general(k_ref[0, rows, lanes[hh]], qs[hh], (((1,), (1,)), ((), ())),
                                     preferred_element_type=F32)
                s_new[hh, rows, :] = st
                cm = jnp.max(st, axis=0, keepdims=True)
                m_run[hh] = cm if c == 0 else jnp.maximum(m_run[hh], cm)
                pt = jnp.exp2(s_prev[hh, rows, :] - m_prev[hh]).astype(BF16)
                acc[hh] = acc[hh] + jnp.dot(vt_ref[0, lanes[hh], rows], pt, preferred_element_type=F32)
        for hh in heads:
            m_new[hh] = m_run[hh]
        outs = [acc[hh][0:V_HEAD] / acc[hh][V_HEAD:V_HEAD + 1] for hh in heads]
        o_ref[0] = jnp.concatenate(outs, axis=0).T.astype(BF16)
        for w_ref, wb_ref in zip(w_refs, wb_refs):
            wb_ref[...] = w_ref[...].astype(BF16)

    @pl.when(g % 2 == 0)
    def _():
        step(s_even, m_even, s_odd, m_odd)

    @pl.when(g % 2 == 1)
    def _():
        step(s_odd, m_odd, s_even, m_even)


def _attention(q, k, vt, s, tq, to_bf16):
    b, ltot, _ = q.shape
    pairs = MLA_HEADS // 2
    nq = s // tq
    n_items = b * pairs * nq
    flat = [w.reshape(-1, w.shape[-1]) for w in to_bf16]
    slabs = [w.shape[0] // n_items for w in flat]
    assert all(w.shape[0] == sl * n_items and sl % (2 * SUBLANE) == 0 for w, sl in zip(flat, slabs))
    slab_specs = [pl.BlockSpec((sl, w.shape[1]), lambda g: (jnp.minimum(g, n_items - 1), 0))
                  for w, sl in zip(flat, slabs)]

    def item(g):
        return g // (pairs * nq), (g // nq) % pairs, g % nq

    def cur(g):
        return item(jnp.minimum(g, n_items - 1))

    def prev(g):
        return item(jnp.maximum(g - 1, 0))

    scores = pltpu.VMEM((2, ltot, tq), F32)
    colmax = pltpu.VMEM((2, 1, tq), F32)
    out = pl.pallas_call(
        functools.partial(_attn_kernel, n_cast=len(flat)),
        grid=(n_items + 1,),
        in_specs=[pl.BlockSpec((1, tq, 2 * HEAD_PAD), lambda g: (cur(g)[0], cur(g)[2], cur(g)[1])),
                  pl.BlockSpec((1, ltot, 2 * HEAD_PAD), lambda g: (cur(g)[0], 0, cur(g)[1])),
                  pl.BlockSpec((1, 2 * HEAD_PAD, ltot), lambda g: (prev(g)[0], prev(g)[1], 0))] + slab_specs,
        out_specs=[pl.BlockSpec((1, tq, 2 * V_HEAD), lambda g: (prev(g)[0], prev(g)[2], prev(g)[1]))] + slab_specs,
        out_shape=[jax.ShapeDtypeStruct((b, s, MLA_HEADS * V_HEAD), BF16)]
        + [jax.ShapeDtypeStruct(w.shape, BF16) for w in flat],
        scratch_shapes=[scores, scores, colmax, colmax],
        compiler_params=_cparams(("arbitrary",)),
        name="attn",
    )(q, k, vt, *flat)
    return out[0], [wb.reshape(w.shape) for wb, w in zip(out[1:], to_bf16)]


def _merge_kernel(rnn_ref, attn_ref, sg_ref, x_ref, mod_ref, modp_ref, g2_ref, wr_ref, wm_ref, wo_ref, wrt_ref,
                  xn_ref, h2_ref, aff_ref, xn_even, xn_odd, *, d):
    g = pl.program_id(0)

    @pl.when(g == 0)
    def _():
        xn_odd[...] = jnp.zeros_like(xn_odd)

    def step(xn_new, xn_prev):
        pr = jnp.dot(rnn_ref[0], wr_ref[...], preferred_element_type=F32)
        pa = jnp.dot(attn_ref[0], wm_ref[...], preferred_element_type=F32)
        merged = (sg_ref[0, :, 0:d] * pr + sg_ref[0, :, d:2 * d] * pa).astype(BF16)

        xp = xn_prev[...]
        inv = lax.rsqrt(jnp.mean(xp * xp, axis=-1, keepdims=True) + EPS)
        h2 = (xp * inv * g2_ref[...]) * (1.0 + modp_ref[0, :, 4 * d:5 * d]) + modp_ref[0, :, 3 * d:4 * d]
        n_tok = h2.shape[0]
        for t in range(d // LANE):
            h2_ref[0, pl.ds(t, n_tok, stride=d // LANE), :] = h2[:, t * LANE:(t + 1) * LANE]
        logits = jnp.dot(h2.astype(BF16), wrt_ref[...], preferred_element_type=F32)
        lane = lax.broadcasted_iota(jnp.int32, (1, LANE), 1)
        logits = jnp.where(lane < N_EXPERTS, logits, -jnp.inf)
        e = jnp.exp(logits - jnp.max(logits, axis=-1, keepdims=True))
        aff_ref[0] = (e / jnp.sum(e, axis=-1, keepdims=True)).T[0:N_EXPERTS]

        xn = x_ref[0] + mod_ref[0, :, 2 * d:3 * d] * jnp.dot(merged, wo_ref[...], preferred_element_type=F32)
        xn_ref[0] = xn
        xn_new[...] = xn

    @pl.when(g % 2 == 0)
    def _():
        step(xn_even, xn_odd)

    @pl.when(g % 2 == 1)
    def _():
        step(xn_odd, xn_even)


def _merge(rnn, attn, sg, x, mod, g2, wr, wm, wo, wrt, tile):
    b, s, d = x.shape
    nt = s // tile
    n_items = b * nt
    const = lambda g: (0, 0)
    cur = lambda g: jnp.minimum(g, n_items - 1)
    prev = lambda g: jnp.maximum(g - 1, 0)
    row = lambda g: (cur(g) // nt, cur(g) % nt, 0)
    row_prev = lambda g: (prev(g) // nt, prev(g) % nt, 0)
    return pl.pallas_call(
        functools.partial(_merge_kernel, d=d),
        grid=(n_items + 1,),
        in_specs=[pl.BlockSpec((1, tile, d), row),
                  pl.BlockSpec((1, tile, attn.shape[2]), row),
                  pl.BlockSpec((1, tile, 2 * d), row),
                  pl.BlockSpec((1, tile, d), row),
                  pl.BlockSpec((1, 1, N_MOD * d), lambda g: (2 * (cur(g) // nt), 0, 0)),
                  pl.BlockSpec((1, 1, N_MOD * d), lambda g: (2 * (prev(g) // nt), 0, 0)),
                  pl.BlockSpec((1, d), const),
                  pl.BlockSpec(wr.shape, const),
                  pl.BlockSpec(wm.shape, const),
                  pl.BlockSpec(wo.shape, const),
                  pl.BlockSpec(wrt.shape, const)],
        out_specs=[pl.BlockSpec((1, tile, d), row),
                   pl.BlockSpec((1, tile * (d // LANE), LANE), row_prev),
                   pl.BlockSpec((1, N_EXPERTS, tile), lambda g: (prev(g) // nt, 0, prev(g) % nt))],
        out_shape=(jax.ShapeDtypeStruct((b, s, d), F32),
                   jax.ShapeDtypeStruct((b, s * (d // LANE), LANE), F32),
                   jax.ShapeDtypeStruct((b, N_EXPERTS, s), F32)),
        scratch_shapes=[pltpu.VMEM((tile, d), F32), pltpu.VMEM((tile, d), F32)],
        compiler_params=_cparams(("arbitrary",)),
        name="merge",
    )(rnn, attn, sg, x, mod, mod, g2, wr, wm, wo, wrt)


def _excl_cumsum_lanes(x01, tri):
    rows, length = x01.shape
    offset = jnp.zeros((rows, 1), F32)
    pieces = []
    for blk in range(length // LANE):
        xb = x01[:, blk * LANE:(blk + 1) * LANE]
        inc = jnp.dot(xb.astype(BF16), tri, preferred_element_type=F32)
        pieces.append(inc - xb + offset)
        offset = offset + jnp.sum(xb, axis=-1, keepdims=True)
    return jnp.concatenate(pieces, axis=-1)


def _topk_kernel(aff_ref, idx_ref, gate_ref, pos_s, sel_s, *, cap):
    a = aff_ref[0]
    n_e, length = a.shape

    def as_float(bits):
        return lax.bitcast_convert_type(bits, F32)

    def bisect(i, cand):
        t = cand | (jnp.int32(1) << (30 - i))
        cnt = jnp.sum((a >= as_float(t)).astype(jnp.int32), axis=-1, keepdims=True)
        return jnp.where(cnt >= cap, t, cand)

    thr_bits = lax.fori_loop(0, 31, bisect, jnp.zeros((n_e, 1), jnp.int32))
    thr = as_float(thr_bits)
    above = as_float(thr_bits + 1)
    gt = (a >= above).astype(F32)
    eq = jnp.logical_and(a >= thr, a < above).astype(F32)
    need = cap - jnp.sum(gt, axis=-1, keepdims=True)
    r_i = lax.broadcasted_iota(jnp.int32, (LANE, LANE), 0)
    c_i = lax.broadcasted_iota(jnp.int32, (LANE, LANE), 1)
    tri = (r_i <= c_i).astype(BF16)
    tie_rank = _excl_cumsum_lanes(eq, tri)
    sel = gt + eq * (tie_rank < need).astype(F32)
    pos_s[...] = _excl_cumsum_lanes(sel, tri)
    sel_s[...] = sel

    n_hi = idx_ref.shape[2]
    tok = lax.broadcasted_iota(jnp.int32, (1, length), 1)
    tok_hi = (tok >> 6).astype(F32)
    tok_lo = (tok & 63).astype(F32)
    a_iota = lax.broadcasted_iota(jnp.int32, (n_hi, 1), 0)
    c_iota = lax.broadcasted_iota(jnp.int32, (SLOT_LO, 1), 0)

    def compact(e, carry):
        pos_row = pos_s[pl.ds(e, 1), :].astype(jnp.int32)
        sel_row = sel_s[pl.ds(e, 1), :] > 0.5
        hot_a = jnp.where(jnp.logical_and((pos_row // SLOT_LO) == a_iota, sel_row), 1.0, 0.0)
        hot_c = jnp.where((pos_row % SLOT_LO) == c_iota, 1.0, 0.0).astype(BF16)
        g = aff_ref[0, pl.ds(e, 1), :]
        g1 = g.astype(BF16).astype(F32)
        g2 = (g - g1).astype(BF16).astype(F32)
        g3 = g - g1 - g2
        lhs = jnp.concatenate([hot_a * v for v in (tok_hi, tok_lo, g1, g2, g3)], axis=0).astype(BF16)
        res = lax.dot_general(lhs, hot_c, (((1,), (1,)), ((), ())), preferred_element_type=F32)
        part = [res[v * n_hi:(v + 1) * n_hi] for v in range(5)]
        idx_ref[0, e] = (part[0] * 64.0 + part[1]).astype(jnp.int32)
        gate_ref[0, e] = part[2] + part[3] + part[4]
        return carry

    lax.fori_loop(0, n_e, compact, 0)


def _topk(aff_t, cap):
    b, n_e, length = aff_t.shape
    n_hi = max(cap // SLOT_LO, SUBLANE)
    blk = lambda bb: (bb, 0, 0)
    blk4 = lambda bb: (bb, 0, 0, 0)
    idx, gate = pl.pallas_call(
        functools.partial(_topk_kernel, cap=cap),
        grid=(b,),
        in_specs=[pl.BlockSpec((1, n_e, length), blk)],
        out_specs=[pl.BlockSpec((1, n_e, n_hi, SLOT_LO), blk4), pl.BlockSpec((1, n_e, n_hi, SLOT_LO), blk4)],
        out_shape=(jax.ShapeDtypeStruct((b, n_e, n_hi, SLOT_LO), jnp.int32),
                   jax.ShapeDtypeStruct((b, n_e, n_hi, SLOT_LO), F32)),
        scratch_shapes=[pltpu.VMEM((n_e, length), F32), pltpu.VMEM((n_e, length), F32)],
        compiler_params=_cparams(("arbitrary",)),
        name="topk",
    )(aff_t)
    flat = lambda a: a.reshape(b * n_e, 1, n_hi * SLOT_LO)[:, :, :cap]
    return flat(idx), flat(gate)


def _gather_kernel(idx_ref, h_ref, o_ref, rows_s, *, cap, d):
    tiles = d // LANE

    def body(j, carry):
        src = pl.multiple_of(idx_ref[0, 0, j] * tiles, tiles)
        dst = pl.multiple_of(j * tiles, tiles)
        rows_s[pl.ds(dst, tiles), :] = h_ref[0, pl.ds(src, tiles), :]
        return carry

    lax.fori_loop(0, cap, body, 0, unroll=8)
    o_ref[0, 0] = jnp.concatenate([rows_s[pl.ds(t, cap, stride=tiles), :] for t in range(tiles)],
                                  axis=-1).astype(BF16)


def _gather(idx, h2t, cap, d):
    b, rows, lanes = h2t.shape
    assert d // lanes == SUBLANE and lanes == LANE
    return pl.pallas_call(
        functools.partial(_gather_kernel, cap=cap, d=d),
        grid=(b, N_EXPERTS),
        in_specs=[pl.BlockSpec((1, 1, cap), lambda bb, e: (bb * N_EXPERTS + e, 0, 0), memory_space=pltpu.SMEM),
                  pl.BlockSpec((1, rows, lanes), lambda bb, e: (bb, 0, 0))],
        out_specs=pl.BlockSpec((1, 1, cap, d), lambda bb, e: (bb, e, 0, 0)),
        out_shape=jax.ShapeDtypeStruct((b, N_EXPERTS, cap, d), BF16),
        scratch_shapes=[pltpu.VMEM((cap * (d // LANE), LANE), F32)],
        compiler_params=_cparams(("arbitrary", "arbitrary")),
        name="gather",
    )(idx, h2t)


def _moe_kernel(idx_ref, gate_ref, x_ref, wg_ref, wu_ref, wd_ref, mod_ref, res_ref, o_ref, y_even, y_odd, *, d, cap):
    g = pl.program_id(0)
    e_prev = jnp.maximum(g - 1, 0) % N_EXPERTS
    piece = res_ref.shape[1]

    @pl.when(g == 0)
    def _():
        y_odd[...] = jnp.zeros_like(y_odd)

    @pl.when(e_prev == 0)
    def _():
        o_ref[...] = jnp.zeros_like(o_ref)

    def step(y_new, y_prev):
        xb = x_ref[0, 0]
        gg = jnp.dot(xb, wg_ref[0], preferred_element_type=F32)
        uu = jnp.dot(xb, wu_ref[0], preferred_element_type=F32)
        hid = (gg * jax.nn.sigmoid(gg) * uu).astype(BF16)
        y_new[...] = jnp.dot(hid, wd_ref[0], preferred_element_type=F32) * mod_ref[0, :, 5 * d:6 * d]

        rows = pl.ds(pl.multiple_of(e_prev * piece, piece), piece)
        o_ref[0, rows, :] = o_ref[0, rows, :] + res_ref[0]
        for j in range(cap):
            r = idx_ref[0, 0, j]
            o_ref[0, pl.ds(r, 1), :] = o_ref[0, pl.ds(r, 1), :] + gate_ref[0, 0, j] * y_prev[j:j + 1, :]

    @pl.when(g % 2 == 0)
    def _():
        step(y_even, y_odd)

    @pl.when(g % 2 == 1)
    def _():
        step(y_odd, y_even)


def _moe(idx, gate, xg, w_gate, w_up, w_down, mod, x_new):
    b, n_e, cap, d = xg.shape
    s = x_new.shape[1]
    f = w_gate.shape[2]
    n_items = b * n_e
    cur = lambda g: jnp.minimum(g, n_items - 1)
    prev = lambda g: jnp.maximum(g - 1, 0)
    smem = lambda: pl.BlockSpec((1, 1, cap), lambda g: (prev(g), 0, 0), memory_space=pltpu.SMEM)
    wspec = lambda shape: pl.BlockSpec(shape, lambda g: (cur(g) % n_e, 0, 0))
    return pl.pallas_call(
        functools.partial(_moe_kernel, d=d, cap=cap),
        grid=(n_items + 1,),
        in_specs=[smem(), smem(),
                  pl.BlockSpec((1, 1, cap, d), lambda g: (cur(g) // n_e, cur(g) % n_e, 0, 0)),
                  wspec((1, d, f)), wspec((1, d, f)), wspec((1, f, d)),
                  pl.BlockSpec((1, 1, N_MOD * d), lambda g: (2 * (cur(g) // n_e), 0, 0)),
                  pl.BlockSpec((1, s // n_e, d), lambda g: (prev(g) // n_e, prev(g) % n_e, 0))],
        out_specs=pl.BlockSpec((1, s, d), lambda g: (prev(g) // n_e, 0, 0), pipeline_mode=pl.Buffered(1)),
        out_shape=jax.ShapeDtypeStruct((b, s, d), F32),
        scratch_shapes=[pltpu.VMEM((cap, d), F32), pltpu.VMEM((cap, d), F32)],
        compiler_params=_cparams(("arbitrary",)),
        name="moe",
    )(idx, gate, xg, w_gate, w_up, w_down, mod, x_new)


def _pad_lanes(a, width):
    return jnp.pad(a, [(0, 0)] * (a.ndim - 1) + [(0, width - a.shape[-1])])


def _rope_tables(s, lc, g_q_nope, g_q_rope, g_k_nope, g_k_rope):
    rows = s // GRID_W
    row = jnp.repeat(jnp.arange(rows, dtype=F32), GRID_W)
    col = jnp.tile(jnp.arange(GRID_W, dtype=F32), rows)
    n_pairs = QK_ROPE // 4
    inv_freq = ROPE_BASE ** (-jnp.arange(n_pairs, dtype=F32) / n_pairs)
    ang = jnp.concatenate([row[:, None] * inv_freq, col[:, None] * inv_freq], axis=-1)
    cos = jnp.concatenate([jnp.cos(ang), jnp.ones((lc, HALF_ROPE), F32)], axis=0)
    sin = jnp.concatenate([jnp.sin(ang), jnp.zeros((lc, HALF_ROPE), F32)], axis=0)
    ltot = s + lc
    ones = jnp.ones((ltot, 1), F32)

    def tables(g_nope, g_rope, scale):
        g1, g2 = g_rope[:HALF_ROPE], g_rope[HALF_ROPE:]
        tc = jnp.concatenate([ones * g_nope[None, :], cos * g1, cos * g2], axis=-1) * scale
        ts = jnp.concatenate([jnp.zeros((ltot, QK_NOPE), F32), -sin * g2, sin * g1], axis=-1) * scale
        return _pad_lanes(tc, LANE), _pad_lanes(ts, LANE)

    tqc, tqs = tables(g_q_nope, g_q_rope, ATTN_SCALE * LOG2_E)
    tkc, tks = tables(jnp.zeros((QK_NOPE,), F32), g_k_rope, 1.0)
    gkn = _pad_lanes(g_k_nope[None, :], LANE)
    return tqc, tqs, tkc, tks, gkn


def _swap_halves(a):
    return jnp.concatenate([a[..., HALF_ROPE:], a[..., :HALF_ROPE]], axis=-1)


def _layout_weights(w_in, w_uq, w_uk, w_uv, d):
    o_q = 2 * d
    o_kv = o_q + Q_LORA
    o_kr = o_kv + KV_LORA
    o_g = o_kr + QK_ROPE
    kr = w_in[:, o_kr:o_g]
    zl = jnp.zeros((w_in.shape[0], QK_NOPE), F32)
    kr_a = _pad_lanes(jnp.concatenate([zl, kr], axis=-1), LANE)
    kr_s = _pad_lanes(jnp.concatenate([zl, _swap_halves(kr)], axis=-1), LANE)
    win_p = jnp.concatenate([w_in[:, :o_kr], kr_a, kr_s, w_in[:, o_g:]], axis=-1).astype(BF16)

    uq = w_uq.reshape(Q_LORA, MLA_HEADS, QK_NOPE + QK_ROPE)
    nope, rope = uq[..., :QK_NOPE], uq[..., QK_NOPE:]
    wuq_p = _pad_lanes(jnp.concatenate([nope, rope], axis=-1), HEAD_PAD).reshape(Q_LORA, -1).astype(BF16)
    wuq_s = _pad_lanes(jnp.concatenate([jnp.zeros_like(nope), _swap_halves(rope)], axis=-1),
                       HEAD_PAD).reshape(Q_LORA, -1).astype(BF16)
    wuk_p = _pad_lanes(w_uk.reshape(KV_LORA, MLA_HEADS, QK_NOPE), HEAD_PAD).reshape(KV_LORA, -1).astype(BF16)
    wuv_p = _pad_lanes(w_uv.reshape(KV_LORA, MLA_HEADS, V_HEAD), HEAD_PAD).reshape(KV_LORA, -1).astype(BF16)
    return win_p, wuq_p, wuq_s, wuk_p, wuv_p


def _pair_blocks(w):
    nb, n, _ = w.shape
    w = w.reshape(nb // 2, 2, n, n)
    z = jnp.zeros((nb // 2, n, n), w.dtype)
    top = jnp.concatenate([w[:, 0], z], axis=-1)
    bot = jnp.concatenate([z, w[:, 1]], axis=-1)
    return jnp.concatenate([top, bot], axis=-2).astype(BF16)


def kernel(x, c, ctx, c_ctx, w_ada, b_ada, g_norm1, g_norm2, w_in, conv_w, conv_b, rg_wa, rg_ba, rg_wx, rg_bx,
           rg_lambda, w_rnn_out, g_q_lora, w_uq, g_kv_lora, w_uk, w_uv, g_q_nope, g_q_rope, g_k_nope, g_k_rope,
           w_mla_out, w_o, w_router, w_e_gate, w_e_up, w_e_down):
    assert w_ada.shape[0] == 1, "single-layer kernel"
    b, s, d = x.shape
    lc = ctx.shape[1]
    tile = lc
    n_lat = s // tile
    cap = CAPACITY_FACTOR * s // N_EXPERTS

    pad_rows = (-(b + 1)) % SUBLANE
    cc = jnp.concatenate([c, c_ctx[None, :], jnp.zeros((pad_rows, d), F32)], axis=0)
    mod_all = _ada(cc, w_ada[0], b_ada[0])
    mod = jnp.stack([mod_all[:b], jnp.broadcast_to(mod_all[b], (b, N_MOD * d))], axis=1).reshape(2 * b, 1, N_MOD * d)

    win_p, wuq_p, wuq_s, wuk_p, wuv_p = _layout_weights(w_in[0], w_uq[0], w_uk[0], w_uv[0], d)
    tqc, tqs, tkc, tks, gkn = _rope_tables(s, lc, g_q_nope[0], g_q_rope[0], g_k_nope[0], g_k_rope[0])
    z0, gz, q, k, v, sg = _inproj(x, ctx, mod, g_norm1, win_p, g_q_lora, wuq_p, wuq_s, g_kv_lora, wuk_p,
                                  wuv_p, gkn, tqc, tqs, tkc, tks, tile)

    h_f, xr = _rnn(z0, _pair_blocks(rg_wa[0, 0]), _pair_blocks(rg_wx[0, 0]), rg_ba[0, 0:1], rg_bx[0, 0:1],
                   rg_lambda[0, 0:1], tile, n_lat, False, conv=(conv_w[0], conv_b))
    rnn = _rnn(xr, _pair_blocks(rg_wa[0, 1]), _pair_blocks(rg_wx[0, 1]), rg_ba[0, 1:2], rg_bx[0, 1:2],
               rg_lambda[0, 1:2], tile, n_lat, True, hf=h_f, gz=gz)

    attn, (wg_b, wu_b, wd_b) = _attention(q, k, v, s, 2 * MXU_COLS, (w_e_gate[0], w_e_up[0], w_e_down[0]))

    x_new, h2, aff = _merge(rnn, attn, sg, x, mod, g_norm2, w_rnn_out[0].astype(BF16), w_mla_out[0].astype(BF16),
                            w_o[0].astype(BF16), _pad_lanes(w_router[0], LANE).astype(BF16), 2 * tile)

    idx, gate = _topk(aff, cap)
    xg = _gather(idx, h2, cap, d)
    return _moe(idx, gate, xg, wg_b, wu_b, wd_b, mod, x_new)
```

```python
import functools

import jax
import jax.numpy as jnp
from jax import lax
from jax.experimental import pallas as pl
from jax.experimental.pallas import tpu as pltpu

EPS = 1e-6
N_MOD = 6
GRID_W = 64
RG_BLOCKS = 8
CONV_W = 4
CONV_PAD_L = 2
RG_C = 8.0
MLA_HEADS = 8
QK_NOPE = 64
QK_ROPE = 32
V_HEAD = 64
Q_LORA = 256
KV_LORA = 128
ROPE_BASE = 10000.0
ATTN_SCALE = (QK_NOPE + QK_ROPE) ** -0.5
LOG2_E = 1.4426950408889634
N_EXPERTS = 16
CAPACITY_FACTOR = 2

LANE = 128
SUBLANE = 8
MXU_COLS = 256
HEAD_PAD = LANE
HALF_ROPE = QK_ROPE // 2
RNN_SUB = 64
SLOT_LO = 32
VMEM_LIMIT = 56 * 1024 * 1024

BF16 = jnp.bfloat16
F32 = jnp.float32


def _cparams(sem):
    return pltpu.CompilerParams(dimension_semantics=sem, vmem_limit_bytes=VMEM_LIMIT)


def _ada_kernel(c_ref, w_ref, b_ref, o_ref):
    cc = c_ref[...]
    s = cc * jax.nn.sigmoid(cc)
    o_ref[...] = jnp.dot(s, w_ref[...], preferred_element_type=F32) + b_ref[...]


def _ada(cc, w_ada, b_ada):
    rows, d = cc.shape
    n = w_ada.shape[1]
    bn = d
    return pl.pallas_call(
        _ada_kernel,
        grid=(n // bn,),
        in_specs=[pl.BlockSpec((rows, d), lambda j: (0, 0)),
                  pl.BlockSpec((d, bn), lambda j: (0, j)),
                  pl.BlockSpec((1, bn), lambda j: (0, j))],
        out_specs=pl.BlockSpec((rows, bn), lambda j: (0, j)),
        out_shape=jax.ShapeDtypeStruct((rows, n), F32),
        compiler_params=_cparams(("arbitrary",)),
        name="ada",
    )(cc, w_ada, b_ada.reshape(1, n))


def _inproj_kernel(x_ref, ctx_ref, mod_ref, g1_ref, win_ref, gq_ref, wuq_ref, wuqs_ref, gkv_ref, wuk_ref,
                   wuv_ref, gkn_ref, tqc_ref, tqs_ref, tkc_ref, tks_ref,
                   z0_ref, gz_ref, q_ref, k_ref, v_ref, sg_ref, *, d, n_lat_tiles):
    i = pl.program_id(1)
    is_ctx = i == n_lat_tiles
    xt = jnp.where(is_ctx, ctx_ref[0], x_ref[0])
    shift = mod_ref[0, :, 0:d]
    scale = mod_ref[0, :, d:2 * d]
    inv = lax.rsqrt(jnp.mean(xt * xt, axis=-1, keepdims=True) + EPS)
    xn = ((xt * inv * g1_ref[...]) * (1.0 + scale) + shift).astype(BF16)

    def seg(lo, hi):
        return jnp.dot(xn, win_ref[:, lo:hi], preferred_element_type=F32)

    o_q = 2 * d
    o_kv = o_q + Q_LORA
    o_kr = o_kv + KV_LORA
    o_g = o_kr + 2 * LANE
    zq = seg(o_q, o_kv)
    zkv = seg(o_kv, o_kr)
    kr_a = seg(o_kr, o_kr + LANE)
    kr_s = seg(o_kr + LANE, o_g)

    lane = lax.broadcasted_iota(jnp.int32, (1, LANE), 1)
    is_nope = lane < QK_NOPE
    is_rope = jnp.logical_and(lane >= QK_NOPE, lane < QK_NOPE + QK_ROPE)

    zqn = (zq * lax.rsqrt(jnp.mean(zq * zq, axis=-1, keepdims=True) + EPS) * gq_ref[...]).astype(BF16)
    qa = jnp.dot(zqn, wuq_ref[...], preferred_element_type=F32)
    qs = jnp.dot(zqn, wuqs_ref[...], preferred_element_type=F32)
    tqc = tqc_ref[...]
    tqs = tqs_ref[...]
    for h in range(MLA_HEADS):
        a = qa[:, h * HEAD_PAD:(h + 1) * HEAD_PAD]
        s = qs[:, h * HEAD_PAD:(h + 1) * HEAD_PAD]
        sq = a * a
        ssn = jnp.sum(jnp.where(is_nope, sq, 0.0), axis=-1, keepdims=True)
        ssr = jnp.sum(jnp.where(is_rope, sq, 0.0), axis=-1, keepdims=True)
        invh = jnp.where(is_nope, lax.rsqrt(ssn * (1.0 / QK_NOPE) + EPS), lax.rsqrt(ssr * (1.0 / QK_ROPE) + EPS))
        q_ref[0, :, h * HEAD_PAD:(h + 1) * HEAD_PAD] = (invh * (a * tqc + s * tqs)).astype(BF16)

    ckv = (zkv * lax.rsqrt(jnp.mean(zkv * zkv, axis=-1, keepdims=True) + EPS) * gkv_ref[...]).astype(BF16)
    lane_h = lax.broadcasted_iota(jnp.int32, (1, MLA_HEADS * HEAD_PAD), 1) % HEAD_PAD
    vp = jnp.dot(ckv, wuv_ref[...], preferred_element_type=F32) + jnp.where(lane_h == V_HEAD, 1.0, 0.0)
    v_ref[0] = vp.T.astype(BF16)
    ka = jnp.dot(ckv, wuk_ref[...], preferred_element_type=F32)
    inv_r = lax.rsqrt(jnp.sum(kr_a * kr_a, axis=-1, keepdims=True) * (1.0 / QK_ROPE) + EPS)
    kr = inv_r * (kr_a * tkc_ref[...] + kr_s * tks_ref[...])
    gkn = gkn_ref[...]
    for h in range(MLA_HEADS):
        a = ka[:, h * HEAD_PAD:(h + 1) * HEAD_PAD]
        invh = lax.rsqrt(jnp.sum(a * a, axis=-1, keepdims=True) * (1.0 / QK_NOPE) + EPS)
        k_ref[0, :, h * HEAD_PAD:(h + 1) * HEAD_PAD] = (a * invh * gkn + kr).astype(BF16)

    z0_ref[0] = seg(0, d)
    gz_ref[0] = jax.nn.gelu(seg(d, o_q)).astype(BF16)
    sg_ref[0] = jax.nn.sigmoid(seg(o_g, o_g + 2 * d)).astype(BF16)


def _inproj(x, ctx, mod, g1, win_p, gq, wuq_p, wuq_s, gkv, wuk_p, wuv, gkn, tqc, tqs, tkc, tks, tile):
    b, s, d = x.shape
    lc = ctx.shape[1]
    assert lc == tile and s % tile == 0
    nl = s // tile
    ltot = s + lc
    n_in = win_p.shape[1]
    hp = MLA_HEADS * HEAD_PAD
    const = lambda bb, i: (0, 0)
    row = lambda bb, i: (bb, i, 0)
    tab = lambda bb, i: (i, 0)
    out_shapes = (
        jax.ShapeDtypeStruct((b, ltot, d), F32),
        jax.ShapeDtypeStruct((b, ltot, d), BF16),
        jax.ShapeDtypeStruct((b, ltot, hp), BF16),
        jax.ShapeDtypeStruct((b, ltot, hp), BF16),
        jax.ShapeDtypeStruct((b, hp, ltot), BF16),
        jax.ShapeDtypeStruct((b, ltot, 2 * d), BF16),
    )
    return pl.pallas_call(
        functools.partial(_inproj_kernel, d=d, n_lat_tiles=nl),
        grid=(b, nl + 1),
        in_specs=[
            pl.BlockSpec((1, tile, d), lambda bb, i: (bb, jnp.minimum(i, nl - 1), 0)),
            pl.BlockSpec((1, tile, d), lambda bb, i: (bb, 0, 0)),
            pl.BlockSpec((1, 1, N_MOD * d), lambda bb, i: (2 * bb + i // nl, 0, 0)),
            pl.BlockSpec((1, d), const),
            pl.BlockSpec((d, n_in), const),
            pl.BlockSpec((1, Q_LORA), const),
            pl.BlockSpec((Q_LORA, hp), const),
            pl.BlockSpec((Q_LORA, hp), const),
            pl.BlockSpec((1, KV_LORA), const),
            pl.BlockSpec((KV_LORA, hp), const),
            pl.BlockSpec((KV_LORA, hp), const),
            pl.BlockSpec((1, LANE), const),
            pl.BlockSpec((tile, LANE), tab),
            pl.BlockSpec((tile, LANE), tab),
            pl.BlockSpec((tile, LANE), tab),
            pl.BlockSpec((tile, LANE), tab),
        ],
        out_specs=[
            pl.BlockSpec((1, tile, d), row),
            pl.BlockSpec((1, tile, d), row),
            pl.BlockSpec((1, tile, hp), row),
            pl.BlockSpec((1, tile, hp), row),
            pl.BlockSpec((1, hp, tile), lambda bb, i: (bb, 0, i)),
            pl.BlockSpec((1, tile, 2 * d), row),
        ],
        out_shape=out_shapes,
        compiler_params=_cparams(("arbitrary", "arbitrary")),
        name="inproj",
    )(x, ctx, mod, g1, win_p, gq, wuq_p, wuq_s, gkv, wuk_p, wuv, gkn, tqc, tqs, tkc, tks)


def _sigmoid(x):
    return 0.5 * jnp.tanh(0.5 * x) + 0.5


def _softplus(x):
    e = jnp.exp(-jnp.abs(x))
    u = 1.0 + e
    tiny = u == 1.0
    log1p_e = jnp.where(tiny, e, jnp.log(u) * (e / jnp.where(tiny, 1.0, u - 1.0)))
    return jnp.maximum(x, 0.0) + log1p_e


def _rnn_kernel(*refs, d, tile, n_lat, reverse):
    if reverse:
        (xr_ref, wa_ref, wx_ref, ba_ref, bx_ref, lam_ref, hf_ref, gz_ref, o_ref, a_s, b_s, hb_s, h_s) = refs
    else:
        (z_ref, zp_ref, zn_ref, cw_ref, cb_ref, wa_ref, wx_ref, ba_ref, bx_ref, lam_ref,
         o_ref, xr_ref, a_s, b_s, h_s) = refs
    j = pl.program_id(1)
    if not reverse:
        c = jnp.where(j == 0, n_lat, j - 1)
        prev_zero = jnp.logical_or(c == 0, c == n_lat)
        next_zero = c >= n_lat - 1
        ext = jnp.concatenate([jnp.where(prev_zero, 0.0, zp_ref[0]), z_ref[0], jnp.where(next_zero, 0.0, zn_ref[0])],
                              axis=0)
        n_ext = tile + 2 * SUBLANE
        taps = [ext if (CONV_PAD_L - k) % n_ext == 0 else pltpu.roll(ext, (CONV_PAD_L - k) % n_ext, 0)
                for k in range(CONV_W)]
    pair = 2 * (d // RG_BLOCKS)
    neg_c_softplus = -RG_C * _softplus(-lam_ref[...])
    dst = hb_s if reverse else o_ref.at[0]

    h = jnp.where(j == 0, 0.0, h_s[...])
    subs = range(tile // RNN_SUB)
    for sb in (reversed(subs) if reverse else subs):
        r0 = sb * RNN_SUB
        if reverse:
            xr = xr_ref[0, r0:r0 + RNN_SUB, :]
        else:
            xr = cb_ref[...]
            for k in range(CONV_W):
                xr = xr + taps[k][SUBLANE + r0:SUBLANE + r0 + RNN_SUB] * cw_ref[k:k + 1, :]
            xr_ref[0, r0:r0 + RNN_SUB, :] = xr
        xb = xr.astype(BF16)
        ya = jnp.concatenate([jnp.dot(xb[:, m * pair:(m + 1) * pair], wa_ref[m], preferred_element_type=F32)
                              for m in range(d // pair)], axis=-1)
        yx = jnp.concatenate([jnp.dot(xb[:, m * pair:(m + 1) * pair], wx_ref[m], preferred_element_type=F32)
                              for m in range(d // pair)], axis=-1)
        r = _sigmoid(ya + ba_ref[...])
        gi = _sigmoid(yx + bx_ref[...])
        log_a = neg_c_softplus * r
        a_s[r0:r0 + RNN_SUB, :] = jnp.exp(log_a)
        th = jnp.tanh(log_a)
        v = -2.0 * th / (1.0 - th)
        b_s[r0:r0 + RNN_SUB, :] = (v * lax.rsqrt(jnp.maximum(v, 1e-30))) * (gi * xr)
        rows = range(r0, r0 + RNN_SUB)
        for t in (reversed(rows) if reverse else rows):
            h = a_s[t:t + 1, :] * h + b_s[t:t + 1, :]
            dst[t:t + 1, :] = h
    h_s[...] = h
    if reverse:
        o_ref[0] = ((hf_ref[0] + hb_s[...]) * gz_ref[0]).astype(BF16)


def _rnn(z, wa_p, wx_p, ba, bx, lam, tile, n_lat, reverse, conv=None, hf=None, gz=None):
    b, ltot, d = z.shape
    n_halo = ltot // SUBLANE
    per = tile // SUBLANE
    if reverse:
        cidx = lambda j: jnp.where(j == 0, n_lat, n_lat - j)
        oidx = lambda j: jnp.where(j == 0, n_lat - 1, n_lat - j)
    else:
        cidx = lambda j: jnp.where(j == 0, n_lat, j - 1)
        oidx = lambda j: jnp.maximum(j - 1, 0)
    const2 = lambda bb, j: (0, 0)
    const3 = lambda bb, j: (0, 0, 0)
    chunk = pl.BlockSpec((1, tile, d), lambda bb, j: (bb, cidx(j), 0))
    out_chunk = pl.BlockSpec((1, tile, d), lambda bb, j: (bb, oidx(j), 0))
    gate_specs = [pl.BlockSpec(wa_p.shape, const3), pl.BlockSpec(wx_p.shape, const3),
                  pl.BlockSpec((1, d), const2), pl.BlockSpec((1, d), const2), pl.BlockSpec((1, d), const2)]
    gate_args = [wa_p, wx_p, ba, bx, lam]
    scratch = [pltpu.VMEM((tile, d), F32), pltpu.VMEM((tile, d), F32)]
    if reverse:
        in_specs = [chunk] + gate_specs + [out_chunk, chunk]
        args = [z] + gate_args + [hf, gz]
        out_specs = out_chunk
        out_shape = jax.ShapeDtypeStruct((b, n_lat * tile, d), BF16)
        scratch += [pltpu.VMEM((tile, d), F32)]
    else:
        in_specs = [chunk,
                    pl.BlockSpec((1, SUBLANE, d), lambda bb, j: (bb, jnp.maximum(cidx(j) * per - 1, 0), 0)),
                    pl.BlockSpec((1, SUBLANE, d), lambda bb, j: (bb, jnp.minimum((cidx(j) + 1) * per, n_halo - 1), 0)),
                    pl.BlockSpec((CONV_W, d), const2), pl.BlockSpec((1, d), const2)] + gate_specs
        args = [z, z, z, conv[0], conv[1]] + gate_args
        out_specs = [out_chunk, chunk]
        out_shape = (jax.ShapeDtypeStruct((b, n_lat * tile, d), F32), jax.ShapeDtypeStruct((b, ltot, d), F32))
    scratch += [pltpu.VMEM((1, d), F32)]
    return pl.pallas_call(
        functools.partial(_rnn_kernel, d=d, tile=tile, n_lat=n_lat, reverse=reverse),
        grid=(b, n_lat + 1),
        in_specs=in_specs,
        out_specs=out_specs,
        out_shape=out_shape,
        scratch_shapes=scratch,
        compiler_params=_cparams(("arbitrary", "arbitrary")),
        name="rnn_bwd" if reverse else "rnn_fwd",
    )(*args)


def _attn_kernel(q_ref, k_ref, vt_ref, *rest, n_cast):
    w_refs = rest[:n_cast]
    o_ref = rest[n_cast]
    wb_refs = rest[n_cast + 1:2 * n_cast + 1]
    s_even, s_odd, m_even, m_odd = rest[2 * n_cast + 1:]
    g = pl.program_id(0)

    @pl.when(g == 0)
    def _():
        s_odd[...] = jnp.zeros_like(s_odd)
        m_odd[...] = jnp.zeros_like(m_odd)

    n_keys = k_ref.shape[1]
    tq = q_ref.shape[1]

    def step(s_new, m_new, s_prev, m_prev):
        heads = range(2)
        lanes = [slice(hh * HEAD_PAD, (hh + 1) * HEAD_PAD) for hh in heads]
        qs = [q_ref[0, :, lanes[hh]] for hh in heads]
        m_run = [None, None]
        acc = [jnp.zeros((HEAD_PAD, tq), F32) for _ in heads]
        for c in range(n_keys // MXU_COLS):
            rows = slice(c * MXU_COLS, (c + 1) * MXU_COLS)
            for hh in heads:
                st = lax.dot_general(k_ref[0, rows, lanes[hh]], qs[hh], (((1,), (1,)), ((), ())),
                                     preferred_element_type=F32)
                s_new[hh, rows, :] = st
                cm = jnp.max(st, axis=0, keepdims=True)
                m_run[hh] = cm if c == 0 else jnp.maximum(m_run[hh], cm)
                pt = jnp.exp2(s_prev[hh, rows, :] - m_prev[hh]).astype(BF16)
                acc[hh] = acc[hh] + jnp.dot(vt_ref[0, lanes[hh], rows], pt, preferred_element_type=F32)
        for hh in heads:
            m_new[hh] = m_run[hh]
        outs = [acc[hh][0:V_HEAD] / acc[hh][V_HEAD:V_HEAD + 1] for hh in heads]
        o_ref[0] = jnp.concatenate(outs, axis=0).T.astype(BF16)
        for w_ref, wb_ref in zip(w_refs, wb_refs):
            wb_ref[...] = w_ref[...].astype(BF16)

    @pl.when(g % 2 == 0)
    def _():
        step(s_even, m_even, s_odd, m_odd)

    @pl.when(g % 2 == 1)
    def _():
        step(s_odd, m_odd, s_even, m_even)


def _attention(q, k, vt, s, tq, to_bf16):
    b, ltot, _ = q.shape
    pairs = MLA_HEADS // 2
    nq = s // tq
    n_items = b * pairs * nq
    flat = [w.reshape(-1, w.shape[-1]) for w in to_bf16]
    slabs = [w.shape[0] // n_items for w in flat]
    assert all(w.shape[0] == sl * n_items and sl % (2 * SUBLANE) == 0 for w, sl in zip(flat, slabs))
    slab_specs = [pl.BlockSpec((sl, w.shape[1]), lambda g: (jnp.minimum(g, n_items - 1), 0))
                  for w, sl in zip(flat, slabs)]

    def item(g):
        return g // (pairs * nq), (g // nq) % pairs, g % nq

    def cur(g):
        return item(jnp.minimum(g, n_items - 1))

    def prev(g):
        return item(jnp.maximum(g - 1, 0))

    scores = pltpu.VMEM((2, ltot, tq), F32)
    colmax = pltpu.VMEM((2, 1, tq), F32)
    out = pl.pallas_call(
        functools.partial(_attn_kernel, n_cast=len(flat)),
        grid=(n_items + 1,),
        in_specs=[pl.BlockSpec((1, tq, 2 * HEAD_PAD), lambda g: (cur(g)[0], cur(g)[2], cur(g)[1])),
                  pl.BlockSpec((1, ltot, 2 * HEAD_PAD), lambda g: (cur(g)[0], 0, cur(g)[1])),
                  pl.BlockSpec((1, 2 * HEAD_PAD, ltot), lambda g: (prev(g)[0], prev(g)[1], 0))] + slab_specs,
        out_specs=[pl.BlockSpec((1, tq, 2 * V_HEAD), lambda g: (prev(g)[0], prev(g)[2], prev(g)[1]))] + slab_specs,
        out_shape=[jax.ShapeDtypeStruct((b, s, MLA_HEADS * V_HEAD), BF16)]
        + [jax.ShapeDtypeStruct(w.shape, BF16) for w in flat],
        scratch_shapes=[scores, scores, colmax, colmax],
        compiler_params=_cparams(("arbitrary",)),
        name="attn",
    )(q, k, vt, *flat)
    return out[0], [wb.reshape(w.shape) for wb, w in zip(out[1:], to_bf16)]


def _merge_kernel(rnn_ref, attn_ref, sg_ref, x_ref, mod_ref, modp_ref, g2_ref, wr_ref, wm_ref, wo_ref, wrt_ref,
                  xn_ref, h2_ref, aff_ref, xn_even, xn_odd, *, d):
    g = pl.program_id(0)

    @pl.when(g == 0)
    def _():
        xn_odd[...] = jnp.zeros_like(xn_odd)

    def step(xn_new, xn_prev):
        pr = jnp.dot(rnn_ref[0], wr_ref[...], preferred_element_type=F32)
        pa = jnp.dot(attn_ref[0], wm_ref[...], preferred_element_type=F32)
        merged = (sg_ref[0, :, 0:d] * pr + sg_ref[0, :, d:2 * d] * pa).astype(BF16)

        xp = xn_prev[...]
        inv = lax.rsqrt(jnp.mean(xp * xp, axis=-1, keepdims=True) + EPS)
        h2 = (xp * inv * g2_ref[...]) * (1.0 + modp_ref[0, :, 4 * d:5 * d]) + modp_ref[0, :, 3 * d:4 * d]
        n_tok = h2.shape[0]
        for t in range(d // LANE):
            h2_ref[0, pl.ds(t, n_tok, stride=d // LANE), :] = h2[:, t * LANE:(t + 1) * LANE]
        logits = jnp.dot(h2.astype(BF16), wrt_ref[...], preferred_element_type=F32)
        lane = lax.broadcasted_iota(jnp.int32, (1, LANE), 1)
        logits = jnp.where(lane < N_EXPERTS, logits, -jnp.inf)
        e = jnp.exp(logits - jnp.max(logits, axis=-1, keepdims=True))
        aff_ref[0] = (e / jnp.sum(e, axis=-1, keepdims=True)).T[0:N_EXPERTS]

        xn = x_ref[0] + mod_ref[0, :, 2 * d:3 * d] * jnp.dot(merged, wo_ref[...], preferred_element_type=F32)
        xn_ref[0] = xn
        xn_new[...] = xn

    @pl.when(g % 2 == 0)
    def _():
        step(xn_even, xn_odd)

    @pl.when(g % 2 == 1)
    def _():
        step(xn_odd, xn_even)


def _merge(rnn, attn, sg, x, mod, g2, wr, wm, wo, wrt, tile):
    b, s, d = x.shape
    nt = s // tile
    n_items = b * nt
    const = lambda g: (0, 0)
    cur = lambda g: jnp.minimum(g, n_items - 1)
    prev = lambda g: jnp.maximum(g - 1, 0)
    row = lambda g: (cur(g) // nt, cur(g) % nt, 0)
    row_prev = lambda g: (prev(g) // nt, prev(g) % nt, 0)
    return pl.pallas_call(
        functools.partial(_merge_kernel, d=d),
        grid=(n_items + 1,),
        in_specs=[pl.BlockSpec((1, tile, d), row),
                  pl.BlockSpec((1, tile, attn.shape[2]), row),
                  pl.BlockSpec((1, tile, 2 * d), row),
                  pl.BlockSpec((1, tile, d), row),
                  pl.BlockSpec((1, 1, N_MOD * d), lambda g: (2 * (cur(g) // nt), 0, 0)),
                  pl.BlockSpec((1, 1, N_MOD * d), lambda g: (2 * (prev(g) // nt), 0, 0)),
                  pl.BlockSpec((1, d), const),
                  pl.BlockSpec(wr.shape, const),
                  pl.BlockSpec(wm.shape, const),
                  pl.BlockSpec(wo.shape, const),
                  pl.BlockSpec(wrt.shape, const)],
        out_specs=[pl.BlockSpec((1, tile, d), row),
                   pl.BlockSpec((1, tile * (d // LANE), LANE), row_prev),
                   pl.BlockSpec((1, N_EXPERTS, tile), lambda g: (prev(g) // nt, 0, prev(g) % nt))],
        out_shape=(jax.ShapeDtypeStruct((b, s, d), F32),
                   jax.ShapeDtypeStruct((b, s * (d // LANE), LANE), F32),
                   jax.ShapeDtypeStruct((b, N_EXPERTS, s), F32)),
        scratch_shapes=[pltpu.VMEM((tile, d), F32), pltpu.VMEM((tile, d), F32)],
        compiler_params=_cparams(("arbitrary",)),
        name="merge",
    )(rnn, attn, sg, x, mod, mod, g2, wr, wm, wo, wrt)


def _excl_cumsum_lanes(x01, tri):
    rows, length = x01.shape
    offset = jnp.zeros((rows, 1), F32)
    pieces = []
    for blk in range(length // LANE):
        xb = x01[:, blk * LANE:(blk + 1) * LANE]
        inc = jnp.dot(xb.astype(BF16), tri, preferred_element_type=F32)
        pieces.append(inc - xb + offset)
        offset = offset + jnp.sum(xb, axis=-1, keepdims=True)
    return jnp.concatenate(pieces, axis=-1)


def _topk_kernel(aff_ref, idx_ref, gate_ref, pos_s, sel_s, *, cap):
    a = aff_ref[0]
    n_e, length = a.shape

    def as_float(bits):
        return lax.bitcast_convert_type(bits, F32)

    def bisect(i, cand):
        t = cand | (jnp.int32(1) << (30 - i))
        cnt = jnp.sum((a >= as_float(t)).astype(jnp.int32), axis=-1, keepdims=True)
        return jnp.where(cnt >= cap, t, cand)

    thr_bits = lax.fori_loop(0, 31, bisect, jnp.zeros((n_e, 1), jnp.int32))
    thr = as_float(thr_bits)
    above = as_float(thr_bits + 1)
    gt = (a >= above).astype(F32)
    eq = jnp.logical_and(a >= thr, a < above).astype(F32)
    need = cap - jnp.sum(gt, axis=-1, keepdims=True)
    r_i = lax.broadcasted_iota(jnp.int32, (LANE, LANE), 0)
    c_i = lax.broadcasted_iota(jnp.int32, (LANE, LANE), 1)
    tri = (r_i <= c_i).astype(BF16)
    tie_rank = _excl_cumsum_lanes(eq, tri)
    sel = gt + eq * (tie_rank < need).astype(F32)
    pos_s[...] = _excl_cumsum_lanes(sel, tri)
    sel_s[...] = sel

    n_hi = idx_ref.shape[2]
    tok = lax.broadcasted_iota(jnp.int32, (1, length), 1)
    tok_hi = (tok >> 6).astype(F32)
    tok_lo = (tok & 63).astype(F32)
    a_iota = lax.broadcasted_iota(jnp.int32, (n_hi, 1), 0)
    c_iota = lax.broadcasted_iota(jnp.int32, (SLOT_LO, 1), 0)

    def compact(e, carry):
        pos_row = pos_s[pl.ds(e, 1), :].astype(jnp.int32)
        sel_row = sel_s[pl.ds(e, 1), :] > 0.5
        hot_a = jnp.where(jnp.logical_and((pos_row // SLOT_LO) == a_iota, sel_row), 1.0, 0.0)
        hot_c = jnp.where((pos_row % SLOT_LO) == c_iota, 1.0, 0.0).astype(BF16)
        g = aff_ref[0, pl.ds(e, 1), :]
        g1 = g.astype(BF16).astype(F32)
        g2 = (g - g1).astype(BF16).astype(F32)
        g3 = g - g1 - g2
        lhs = jnp.concatenate([hot_a * v for v in (tok_hi, tok_lo, g1, g2, g3)], axis=0).astype(BF16)
        res = lax.dot_general(lhs, hot_c, (((1,), (1,)), ((), ())), preferred_element_type=F32)
        part = [res[v * n_hi:(v + 1) * n_hi] for v in range(5)]
        idx_ref[0, e] = (part[0] * 64.0 + part[1]).astype(jnp.int32)
        gate_ref[0, e] = part[2] + part[3] + part[4]
        return carry

    lax.fori_loop(0, n_e, compact, 0)


def _topk(aff_t, cap):
    b, n_e, length = aff_t.shape
    n_hi = max(cap // SLOT_LO, SUBLANE)
    blk = lambda bb: (bb, 0, 0)
    blk4 = lambda bb: (bb, 0, 0, 0)
    idx, gate = pl.pallas_call(
        functools.partial(_topk_kernel, cap=cap),
        grid=(b,),
        in_specs=[pl.BlockSpec((1, n_e, length), blk)],
        out_specs=[pl.BlockSpec((1, n_e, n_hi, SLOT_LO), blk4), pl.BlockSpec((1, n_e, n_hi, SLOT_LO), blk4)],
        out_shape=(jax.ShapeDtypeStruct((b, n_e, n_hi, SLOT_LO), jnp.int32),
                   jax.ShapeDtypeStruct((b, n_e, n_hi, SLOT_LO), F32)),
        scratch_shapes=[pltpu.VMEM((n_e, length), F32), pltpu.VMEM((n_e, length), F32)],
        compiler_params=_cparams(("arbitrary",)),
        name="topk",
    )(aff_t)
    flat = lambda a: a.reshape(b * n_e, 1, n_hi * SLOT_LO)[:, :, :cap]
    return flat(idx), flat(gate)


def _gather_kernel(idx_ref, h_ref, o_ref, rows_s, *, cap, d):
    tiles = d // LANE

    def body(j, carry):
        src = pl.multiple_of(idx_ref[0, 0, j] * tiles, tiles)
        dst = pl.multiple_of(j * tiles, tiles)
        rows_s[pl.ds(dst, tiles), :] = h_ref[0, pl.ds(src, tiles), :]
        return carry

    lax.fori_loop(0, cap, body, 0, unroll=8)
    o_ref[0, 0] = jnp.concatenate([rows_s[pl.ds(t, cap, stride=tiles), :] for t in range(tiles)],
                                  axis=-1).astype(BF16)


def _gather(idx, h2t, cap, d):
    b, rows, lanes = h2t.shape
    assert d // lanes == SUBLANE and lanes == LANE
    return pl.pallas_call(
        functools.partial(_gather_kernel, cap=cap, d=d),
        grid=(b, N_EXPERTS),
        in_specs=[pl.BlockSpec((1, 1, cap), lambda bb, e: (bb * N_EXPERTS + e, 0, 0), memory_space=pltpu.SMEM),
                  pl.BlockSpec((1, rows, lanes), lambda bb, e: (bb, 0, 0))],
        out_specs=pl.BlockSpec((1, 1, cap, d), lambda bb, e: (bb, e, 0, 0)),
        out_shape=jax.ShapeDtypeStruct((b, N_EXPERTS, cap, d), BF16),
        scratch_shapes=[pltpu.VMEM((cap * (d // LANE), LANE), F32)],
        compiler_params=_cparams(("arbitrary", "arbitrary")),
        name="gather",
    )(idx, h2t)


def _moe_kernel(idx_ref, gate_ref, x_ref, wg_ref, wu_ref, wd_ref, mod_ref, res_ref, o_ref, y_even, y_odd, *, d, cap):
    g = pl.program_id(0)
    e_prev = jnp.maximum(g - 1, 0) % N_EXPERTS
    piece = res_ref.shape[1]

    @pl.when(g == 0)
    def _():
        y_odd[...] = jnp.zeros_like(y_odd)

    @pl.when(e_prev == 0)
    def _():
        o_ref[...] = jnp.zeros_like(o_ref)

    def step(y_new, y_prev):
        xb = x_ref[0, 0]
        gg = jnp.dot(xb, wg_ref[0], preferred_element_type=F32)
        uu = jnp.dot(xb, wu_ref[0], preferred_element_type=F32)
        hid = (gg * jax.nn.sigmoid(gg) * uu).astype(BF16)
        y_new[...] = jnp.dot(hid, wd_ref[0], preferred_element_type=F32) * mod_ref[0, :, 5 * d:6 * d]

        rows = pl.ds(pl.multiple_of(e_prev * piece, piece), piece)
        o_ref[0, rows, :] = o_ref[0, rows, :] + res_ref[0]
        for j in range(cap):
            r = idx_ref[0, 0, j]
            o_ref[0, pl.ds(r, 1), :] = o_ref[0, pl.ds(r, 1), :] + gate_ref[0, 0, j] * y_prev[j:j + 1, :]

    @pl.when(g % 2 == 0)
    def _():
        step(y_even, y_odd)

    @pl.when(g % 2 == 1)
    def _():
        step(y_odd, y_even)


def _moe(idx, gate, xg, w_gate, w_up, w_down, mod, x_new):
    b, n_e, cap, d = xg.shape
    s = x_new.shape[1]
    f = w_gate.shape[2]
    n_items = b * n_e
    cur = lambda g: jnp.minimum(g, n_items - 1)
    prev = lambda g: jnp.maximum(g - 1, 0)
    smem = lambda: pl.BlockSpec((1, 1, cap), lambda g: (prev(g), 0, 0), memory_space=pltpu.SMEM)
    wspec = lambda shape: pl.BlockSpec(shape, lambda g: (cur(g) % n_e, 0, 0))
    return pl.pallas_call(
        functools.partial(_moe_kernel, d=d, cap=cap),
        grid=(n_items + 1,),
        in_specs=[smem(), smem(),
                  pl.BlockSpec((1, 1, cap, d), lambda g: (cur(g) // n_e, cur(g) % n_e, 0, 0)),
                  wspec((1, d, f)), wspec((1, d, f)), wspec((1, f, d)),
                  pl.BlockSpec((1, 1, N_MOD * d), lambda g: (2 * (cur(g) // n_e), 0, 0)),
                  pl.BlockSpec((1, s // n_e, d), lambda g: (prev(g) // n_e, prev(g) % n_e, 0))],
        out_specs=pl.BlockSpec((1, s, d), lambda g: (prev(g) // n_e, 0, 0), pipeline_mode=pl.Buffered(1)),
        out_shape=jax.ShapeDtypeStruct((b, s, d), F32),
        scratch_shapes=[pltpu.VMEM((cap, d), F32), pltpu.VMEM((cap, d), F32)],
        compiler_params=_cparams(("arbitrary",)),
        name="moe",
    )(idx, gate, xg, w_gate, w_up, w_down, mod, x_new)


def _pad_lanes(a, width):
    return jnp.pad(a, [(0, 0)] * (a.ndim - 1) + [(0, width - a.shape[-1])])


def _rope_tables(s, lc, g_q_nope, g_q_rope, g_k_nope, g_k_rope):
    ltot = s + lc
    n_pairs = QK_ROPE // 4
    lane = jnp.arange(LANE)
    in_rope = jnp.logical_and(lane >= QK_NOPE, lane < QK_NOPE + QK_ROPE)
    angle_id = jnp.where(in_rope, (lane - QK_NOPE) % HALF_ROPE, 0)
    inv_freq = ROPE_BASE ** (-jnp.arange(n_pairs, dtype=F32) / n_pairs)
    freq = jnp.where(in_rope, inv_freq[angle_id % n_pairs], 0.0)
    t = jnp.arange(ltot)
    pos = jnp.where((angle_id >= n_pairs)[None, :], (t % GRID_W)[:, None], (t // GRID_W)[:, None]).astype(F32)
    ang = jnp.where((t < s)[:, None], pos * freq[None, :], 0.0)
    cos, sin = jnp.cos(ang), jnp.sin(ang)
    z_nope = jnp.zeros((QK_NOPE,), F32)
    z_tail = jnp.zeros((LANE - QK_NOPE - QK_ROPE,), F32)

    def tables(g_nope, g_rope, scale):
        g_cos = jnp.concatenate([g_nope, g_rope, z_tail]) * scale
        g_sin = jnp.concatenate([z_nope, -g_rope[HALF_ROPE:], g_rope[:HALF_ROPE], z_tail]) * scale
        return cos * g_cos[None, :], sin * g_sin[None, :]

    tqc, tqs = tables(g_q_nope, g_q_rope, ATTN_SCALE * LOG2_E)
    tkc, tks = tables(z_nope, g_k_rope, 1.0)
    gkn = _pad_lanes(g_k_nope[None, :], LANE)
    return tqc, tqs, tkc, tks, gkn


def _swap_halves(a):
    return jnp.concatenate([a[..., HALF_ROPE:], a[..., :HALF_ROPE]], axis=-1)


def _layout_weights(w_in, w_uq, w_uk, w_uv, d):
    o_q = 2 * d
    o_kv = o_q + Q_LORA
    o_kr = o_kv + KV_LORA
    o_g = o_kr + QK_ROPE
    kr = w_in[:, o_kr:o_g]
    zl = jnp.zeros((w_in.shape[0], QK_NOPE), F32)
    kr_a = _pad_lanes(jnp.concatenate([zl, kr], axis=-1), LANE)
    kr_s = _pad_lanes(jnp.concatenate([zl, _swap_halves(kr)], axis=-1), LANE)
    win_p = jnp.concatenate([w_in[:, :o_kr], kr_a, kr_s, w_in[:, o_g:]], axis=-1).astype(BF16)

    uq = w_uq.reshape(Q_LORA, MLA_HEADS, QK_NOPE + QK_ROPE)
    nope, rope = uq[..., :QK_NOPE], uq[..., QK_NOPE:]
    wuq_p = _pad_lanes(jnp.concatenate([nope, rope], axis=-1), HEAD_PAD).reshape(Q_LORA, -1).astype(BF16)
    wuq_s = _pad_lanes(jnp.concatenate([jnp.zeros_like(nope), _swap_halves(rope)], axis=-1),
                       HEAD_PAD).reshape(Q_LORA, -1).astype(BF16)
    wuk_p = _pad_lanes(w_uk.reshape(KV_LORA, MLA_HEADS, QK_NOPE), HEAD_PAD).reshape(KV_LORA, -1).astype(BF16)
    wuv_p = _pad_lanes(w_uv.reshape(KV_LORA, MLA_HEADS, V_HEAD), HEAD_PAD).reshape(KV_LORA, -1).astype(BF16)
    return win_p, wuq_p, wuq_s, wuk_p, wuv_p


def _pair_blocks(w):
    *lead, nb, n, _ = w.shape
    w = w.reshape(*lead, nb // 2, 2, n, n)
    paired = jnp.einsum('...irc,ij->...irjc', w, jnp.eye(2, dtype=w.dtype))
    return paired.reshape(*lead, nb // 2, 2 * n, 2 * n).astype(BF16)


def kernel(x, c, ctx, c_ctx, w_ada, b_ada, g_norm1, g_norm2, w_in, conv_w, conv_b, rg_wa, rg_ba, rg_wx, rg_bx,
           rg_lambda, w_rnn_out, g_q_lora, w_uq, g_kv_lora, w_uk, w_uv, g_q_nope, g_q_rope, g_k_nope, g_k_rope,
           w_mla_out, w_o, w_router, w_e_gate, w_e_up, w_e_down):
    assert w_ada.shape[0] == 1, "single-layer kernel"
    b, s, d = x.shape
    lc = ctx.shape[1]
    tile = lc
    n_lat = s // tile
    cap = CAPACITY_FACTOR * s // N_EXPERTS

    pad_rows = (-(b + 1)) % SUBLANE
    cc = jnp.concatenate([c, c_ctx[None, :], jnp.zeros((pad_rows, d), F32)], axis=0)
    mod_all = _ada(cc, w_ada[0], b_ada[0])
    mod = jnp.stack([mod_all[:b], jnp.broadcast_to(mod_all[b], (b, N_MOD * d))], axis=1).reshape(2 * b, 1, N_MOD * d)

    win_p, wuq_p, wuq_s, wuk_p, wuv_p = _layout_weights(w_in[0], w_uq[0], w_uk[0], w_uv[0], d)
    tqc, tqs, tkc, tks, gkn = _rope_tables(s, lc, g_q_nope[0], g_q_rope[0], g_k_nope[0], g_k_rope[0])
    z0, gz, q, k, v, sg = _inproj(x, ctx, mod, g_norm1, win_p, g_q_lora, wuq_p, wuq_s, g_kv_lora, wuk_p,
                                  wuv_p, gkn, tqc, tqs, tkc, tks, tile)

    gate_w = _pair_blocks(jnp.stack([rg_wa[0], rg_wx[0]]))
    h_f, xr = _rnn(z0, gate_w[0, 0], gate_w[1, 0], rg_ba[0, 0:1], rg_bx[0, 0:1],
                   rg_lambda[0, 0:1], tile, n_lat, False, conv=(conv_w[0], conv_b))
    rnn = _rnn(xr, gate_w[0, 1], gate_w[1, 1], rg_ba[0, 1:2], rg_bx[0, 1:2],
               rg_lambda[0, 1:2], tile, n_lat, True, hf=h_f, gz=gz)

    attn, (wg_b, wu_b, wd_b) = _attention(q, k, v, s, 2 * MXU_COLS, (w_e_gate[0], w_e_up[0], w_e_down[0]))

    x_new, h2, aff = _merge(rnn, attn, sg, x, mod, g_norm2, w_rnn_out[0].astype(BF16), w_mla_out[0].astype(BF16),
                            w_o[0].astype(BF16), _pad_lanes(w_router[0], LANE).astype(BF16), 2 * tile)

    idx, gate = _topk(aff, cap)
    xg = _gather(idx, h2, cap, d)
    return _moe(idx, gate, xg, wg_b, wu_b, wd_b, mod, x_new)
```

```python
import functools

import jax
import jax.numpy as jnp
from jax import lax
from jax.experimental import pallas as pl
from jax.experimental.pallas import tpu as pltpu

EPS = 1e-6
N_MOD = 6
GRID_W = 64
RG_BLOCKS = 8
CONV_W = 4
CONV_PAD_L = 2
RG_C = 8.0
MLA_HEADS = 8
QK_NOPE = 64
QK_ROPE = 32
V_HEAD = 64
Q_LORA = 256
KV_LORA = 128
ROPE_BASE = 10000.0
ATTN_SCALE = (QK_NOPE + QK_ROPE) ** -0.5
LOG2_E = 1.4426950408889634
N_EXPERTS = 16
CAPACITY_FACTOR = 2

LANE = 128
SUBLANE = 8
MXU_COLS = 256
HEAD_PAD = LANE
HALF_ROPE = QK_ROPE // 2
RNN_SUB = 64
RNN_SAMPLES = 4
SLOT_LO = 32
VMEM_LIMIT = 56 * 1024 * 1024

BF16 = jnp.bfloat16
F32 = jnp.float32


def _cparams(sem):
    return pltpu.CompilerParams(dimension_semantics=sem, vmem_limit_bytes=VMEM_LIMIT)


def _ada_kernel(c_ref, w_ref, b_ref, o_ref):
    cc = c_ref[...]
    s = cc * jax.nn.sigmoid(cc)
    o_ref[...] = jnp.dot(s, w_ref[...], preferred_element_type=F32) + b_ref[...]


def _ada(cc, w_ada, b_ada):
    rows, d = cc.shape
    n = w_ada.shape[1]
    bn = d
    return pl.pallas_call(
        _ada_kernel,
        grid=(n // bn,),
        in_specs=[pl.BlockSpec((rows, d), lambda j: (0, 0)),
                  pl.BlockSpec((d, bn), lambda j: (0, j)),
                  pl.BlockSpec((1, bn), lambda j: (0, j))],
        out_specs=pl.BlockSpec((rows, bn), lambda j: (0, j)),
        out_shape=jax.ShapeDtypeStruct((rows, n), F32),
        compiler_params=_cparams(("arbitrary",)),
        name="ada",
    )(cc, w_ada, b_ada.reshape(1, n))


def _inproj_kernel(x_ref, ctx_ref, mod_ref, g1_ref, win_ref, gq_ref, wuq_ref, wuqs_ref, gkv_ref, wuk_ref,
                   wuv_ref, gkn_ref, tqc_ref, tqs_ref, tkc_ref, tks_ref,
                   z0_ref, gz_ref, q_ref, k_ref, v_ref, sg_ref, *, d, n_lat_tiles):
    i = pl.program_id(1)
    is_ctx = i == n_lat_tiles
    xt = jnp.where(is_ctx, ctx_ref[0], x_ref[0])
    shift = mod_ref[0, :, 0:d]
    scale = mod_ref[0, :, d:2 * d]
    inv = lax.rsqrt(jnp.mean(xt * xt, axis=-1, keepdims=True) + EPS)
    xn = ((xt * inv * g1_ref[...]) * (1.0 + scale) + shift).astype(BF16)

    def seg(lo, hi):
        return jnp.dot(xn, win_ref[:, lo:hi], preferred_element_type=F32)

    o_q = 2 * d
    o_kv = o_q + Q_LORA
    o_kr = o_kv + KV_LORA
    o_g = o_kr + 2 * LANE
    zq = seg(o_q, o_kv)
    zkv = seg(o_kv, o_kr)
    kr_a = seg(o_kr, o_kr + LANE)
    kr_s = seg(o_kr + LANE, o_g)

    lane = lax.broadcasted_iota(jnp.int32, (1, LANE), 1)
    is_nope = lane < QK_NOPE
    is_rope = jnp.logical_and(lane >= QK_NOPE, lane < QK_NOPE + QK_ROPE)

    zqn = (zq * lax.rsqrt(jnp.mean(zq * zq, axis=-1, keepdims=True) + EPS) * gq_ref[...]).astype(BF16)
    qa = jnp.dot(zqn, wuq_ref[...], preferred_element_type=F32)
    qs = jnp.dot(zqn, wuqs_ref[...], preferred_element_type=F32)
    tqc = tqc_ref[...]
    tqs = tqs_ref[...]
    for h in range(MLA_HEADS):
        a = qa[:, h * HEAD_PAD:(h + 1) * HEAD_PAD]
        s = qs[:, h * HEAD_PAD:(h + 1) * HEAD_PAD]
        sq = a * a
        ssn = jnp.sum(jnp.where(is_nope, sq, 0.0), axis=-1, keepdims=True)
        ssr = jnp.sum(jnp.where(is_rope, sq, 0.0), axis=-1, keepdims=True)
        invh = jnp.where(is_nope, lax.rsqrt(ssn * (1.0 / QK_NOPE) + EPS), lax.rsqrt(ssr * (1.0 / QK_ROPE) + EPS))
        q_ref[0, :, h * HEAD_PAD:(h + 1) * HEAD_PAD] = (invh * (a * tqc + s * tqs)).astype(BF16)

    ckv = (zkv * lax.rsqrt(jnp.mean(zkv * zkv, axis=-1, keepdims=True) + EPS) * gkv_ref[...]).astype(BF16)
    lane_h = lax.broadcasted_iota(jnp.int32, (1, MLA_HEADS * HEAD_PAD), 1) % HEAD_PAD
    vp = jnp.dot(ckv, wuv_ref[...], preferred_element_type=F32) + jnp.where(lane_h == V_HEAD, 1.0, 0.0)
    v_ref[0] = vp.T.astype(BF16)
    ka = jnp.dot(ckv, wuk_ref[...], preferred_element_type=F32)
    inv_r = lax.rsqrt(jnp.sum(kr_a * kr_a, axis=-1, keepdims=True) * (1.0 / QK_ROPE) + EPS)
    kr = inv_r * (kr_a * tkc_ref[...] + kr_s * tks_ref[...])
    gkn = gkn_ref[...]
    for h in range(MLA_HEADS):
        a = ka[:, h * HEAD_PAD:(h + 1) * HEAD_PAD]
        invh = lax.rsqrt(jnp.sum(a * a, axis=-1, keepdims=True) * (1.0 / QK_NOPE) + EPS)
        k_ref[0, :, h * HEAD_PAD:(h + 1) * HEAD_PAD] = (a * invh * gkn + kr).astype(BF16)

    z0_ref[0] = seg(0, d)
    gz_ref[0] = jax.nn.gelu(seg(d, o_q)).astype(BF16)
    sg_ref[0] = jax.nn.sigmoid(seg(o_g, o_g + 2 * d)).astype(BF16)


def _inproj(x, ctx, mod, g1, win_p, gq, wuq_p, wuq_s, gkv, wuk_p, wuv, gkn, tqc, tqs, tkc, tks, tile):
    b, s, d = x.shape
    lc = ctx.shape[1]
    assert lc == tile and s % tile == 0
    nl = s // tile
    ltot = s + lc
    n_in = win_p.shape[1]
    hp = MLA_HEADS * HEAD_PAD
    const = lambda bb, i: (0, 0)
    row = lambda bb, i: (bb, i, 0)
    tab = lambda bb, i: (i, 0)
    out_shapes = (
        jax.ShapeDtypeStruct((b, ltot, d), F32),
        jax.ShapeDtypeStruct((b, ltot, d), BF16),
        jax.ShapeDtypeStruct((b, ltot, hp), BF16),
        jax.ShapeDtypeStruct((b, ltot, hp), BF16),
        jax.ShapeDtypeStruct((b, hp, ltot), BF16),
        jax.ShapeDtypeStruct((b, ltot, 2 * d), BF16),
    )
    return pl.pallas_call(
        functools.partial(_inproj_kernel, d=d, n_lat_tiles=nl),
        grid=(b, nl + 1),
        in_specs=[
            pl.BlockSpec((1, tile, d), lambda bb, i: (bb, jnp.minimum(i, nl - 1), 0)),
            pl.BlockSpec((1, tile, d), lambda bb, i: (bb, 0, 0)),
            pl.BlockSpec((1, 1, N_MOD * d), lambda bb, i: (2 * bb + i // nl, 0, 0)),
            pl.BlockSpec((1, d), const),
            pl.BlockSpec((d, n_in), const),
            pl.BlockSpec((1, Q_LORA), const),
            pl.BlockSpec((Q_LORA, hp), const),
            pl.BlockSpec((Q_LORA, hp), const),
            pl.BlockSpec((1, KV_LORA), const),
            pl.BlockSpec((KV_LORA, hp), const),
            pl.BlockSpec((KV_LORA, hp), const),
            pl.BlockSpec((1, LANE), const),
            pl.BlockSpec((tile, LANE), tab),
            pl.BlockSpec((tile, LANE), tab),
            pl.BlockSpec((tile, LANE), tab),
            pl.BlockSpec((tile, LANE), tab),
        ],
        out_specs=[
            pl.BlockSpec((1, tile, d), row),
            pl.BlockSpec((1, tile, d), row),
            pl.BlockSpec((1, tile, hp), row),
            pl.BlockSpec((1, tile, hp), row),
            pl.BlockSpec((1, hp, tile), lambda bb, i: (bb, 0, i)),
            pl.BlockSpec((1, tile, 2 * d), row),
        ],
        out_shape=out_shapes,
        compiler_params=_cparams(("arbitrary", "arbitrary")),
        name="inproj",
    )(x, ctx, mod, g1, win_p, gq, wuq_p, wuq_s, gkv, wuk_p, wuv, gkn, tqc, tqs, tkc, tks)


def _sigmoid(x):
    return 0.5 * jnp.tanh(0.5 * x) + 0.5


def _softplus(x):
    e = jnp.exp(-jnp.abs(x))
    u = 1.0 + e
    tiny = u == 1.0
    log1p_e = jnp.where(tiny, e, jnp.log(u) * (e / jnp.where(tiny, 1.0, u - 1.0)))
    return jnp.maximum(x, 0.0) + log1p_e


def _rnn_kernel(*refs, d, tile, n_lat, reverse):
    if reverse:
        (xr_ref, wa_ref, wx_ref, ba_ref, bx_ref, lam_ref, hf_ref, gz_ref, o_ref, a_s, b_s, hb_s, h_s) = refs
    else:
        (z_ref, zp_ref, zn_ref, cw_ref, cb_ref, wa_ref, wx_ref, ba_ref, bx_ref, lam_ref,
         o_ref, xr_ref, a_s, b_s, h_s) = refs
    j = pl.program_id(1)
    nb = a_s.shape[0]
    if not reverse:
        c = jnp.where(j == 0, n_lat, j - 1)
        prev_zero = jnp.logical_or(c == 0, c == n_lat)
        next_zero = c >= n_lat - 1
        n_ext = tile + 2 * SUBLANE
        taps = []
        for bi in range(nb):
            ext = jnp.concatenate([jnp.where(prev_zero, 0.0, zp_ref[bi]), z_ref[bi],
                                   jnp.where(next_zero, 0.0, zn_ref[bi])], axis=0)
            taps.append([ext if (CONV_PAD_L - k) % n_ext == 0 else pltpu.roll(ext, (CONV_PAD_L - k) % n_ext, 0)
                         for k in range(CONV_W)])
    pair = 2 * (d // RG_BLOCKS)
    neg_c_softplus = -RG_C * _softplus(-lam_ref[...])

    hs = [jnp.where(j == 0, 0.0, h_s[bi]) for bi in range(nb)]
    subs = range(tile // RNN_SUB)
    for sb in (reversed(subs) if reverse else subs):
        r0 = sb * RNN_SUB
        for bi in range(nb):
            dst = hb_s.at[bi] if reverse else o_ref.at[bi]
            if reverse:
                xr = xr_ref[bi, r0:r0 + RNN_SUB, :]
            else:
                xr = cb_ref[...]
                for k in range(CONV_W):
                    xr = xr + taps[bi][k][SUBLANE + r0:SUBLANE + r0 + RNN_SUB] * cw_ref[k:k + 1, :]
                xr_ref[bi, r0:r0 + RNN_SUB, :] = xr
            xb = xr.astype(BF16)
            ya = jnp.concatenate([jnp.dot(xb[:, m * pair:(m + 1) * pair], wa_ref[m], preferred_element_type=F32)
                                  for m in range(d // pair)], axis=-1)
            yx = jnp.concatenate([jnp.dot(xb[:, m * pair:(m + 1) * pair], wx_ref[m], preferred_element_type=F32)
                                  for m in range(d // pair)], axis=-1)
            r = _sigmoid(ya + ba_ref[...])
            gi = _sigmoid(yx + bx_ref[...])
            log_a = neg_c_softplus * r
            a_s[bi, r0:r0 + RNN_SUB, :] = jnp.exp(log_a)
            th = jnp.tanh(log_a)
            v = -2.0 * th / (1.0 - th)
            b_s[bi, r0:r0 + RNN_SUB, :] = (v * lax.rsqrt(jnp.maximum(v, 1e-30))) * (gi * xr)
            rows = range(r0, r0 + RNN_SUB)
            h = hs[bi]
            for t in (reversed(rows) if reverse else rows):
                h = a_s[bi, t:t + 1, :] * h + b_s[bi, t:t + 1, :]
                dst[t:t + 1, :] = h
            hs[bi] = h
    for bi in range(nb):
        h_s[bi] = hs[bi]
    if reverse:
        o_ref[...] = ((hf_ref[...] + hb_s[...]) * gz_ref[...]).astype(BF16)


def _rnn(z, wa_p, wx_p, ba, bx, lam, tile, n_lat, reverse, conv=None, hf=None, gz=None):
    b, ltot, d = z.shape
    nb = RNN_SAMPLES if b % RNN_SAMPLES == 0 else 1
    n_halo = ltot // SUBLANE
    per = tile // SUBLANE
    if reverse:
        cidx = lambda j: jnp.where(j == 0, n_lat, n_lat - j)
        oidx = lambda j: jnp.where(j == 0, n_lat - 1, n_lat - j)
    else:
        cidx = lambda j: jnp.where(j == 0, n_lat, j - 1)
        oidx = lambda j: jnp.maximum(j - 1, 0)
    const2 = lambda bb, j: (0, 0)
    const3 = lambda bb, j: (0, 0, 0)
    chunk = pl.BlockSpec((nb, tile, d), lambda bb, j: (bb, cidx(j), 0))
    out_chunk = pl.BlockSpec((nb, tile, d), lambda bb, j: (bb, oidx(j), 0))
    gate_specs = [pl.BlockSpec(wa_p.shape, const3), pl.BlockSpec(wx_p.shape, const3),
                  pl.BlockSpec((1, d), const2), pl.BlockSpec((1, d), const2), pl.BlockSpec((1, d), const2)]
    gate_args = [wa_p, wx_p, ba, bx, lam]
    scratch = [pltpu.VMEM((nb, tile, d), F32), pltpu.VMEM((nb, tile, d), F32)]
    if reverse:
        in_specs = [chunk] + gate_specs + [out_chunk, chunk]
        args = [z] + gate_args + [hf, gz]
        out_specs = out_chunk
        out_shape = jax.ShapeDtypeStruct((b, n_lat * tile, d), BF16)
        scratch += [pltpu.VMEM((nb, tile, d), F32)]
    else:
        in_specs = [chunk,
                    pl.BlockSpec((nb, SUBLANE, d), lambda bb, j: (bb, jnp.maximum(cidx(j) * per - 1, 0), 0)),
                    pl.BlockSpec((nb, SUBLANE, d), lambda bb, j: (bb, jnp.minimum((cidx(j) + 1) * per, n_halo - 1), 0)),
                    pl.BlockSpec((CONV_W, d), const2), pl.BlockSpec((1, d), const2)] + gate_specs
        args = [z, z, z, conv[0], conv[1]] + gate_args
        out_specs = [out_chunk, chunk]
        out_shape = (jax.ShapeDtypeStruct((b, n_lat * tile, d), F32), jax.ShapeDtypeStruct((b, ltot, d), F32))
    scratch += [pltpu.VMEM((nb, 1, d), F32)]
    return pl.pallas_call(
        functools.partial(_rnn_kernel, d=d, tile=tile, n_lat=n_lat, reverse=reverse),
        grid=(b // nb, n_lat + 1),
        in_specs=in_specs,
        out_specs=out_specs,
        out_shape=out_shape,
        scratch_shapes=scratch,
        compiler_params=_cparams(("arbitrary", "arbitrary")),
        name="rnn_bwd" if reverse else "rnn_fwd",
    )(*args)


def _attn_kernel(q_ref, k_ref, vt_ref, *rest, n_cast):
    w_refs = rest[:n_cast]
    o_ref = rest[n_cast]
    wb_refs = rest[n_cast + 1:2 * n_cast + 1]
    s_even, s_odd, m_even, m_odd = rest[2 * n_cast + 1:]
    g = pl.program_id(0)

    @pl.when(g == 0)
    def _():
        s_odd[...] = jnp.zeros_like(s_odd)
        m_odd[...] = jnp.zeros_like(m_odd)

    n_keys = k_ref.shape[1]
    tq = q_ref.shape[1]

    def step(s_new, m_new, s_prev, m_prev):
        heads = range(2)
        lanes = [slice(hh * HEAD_PAD, (hh + 1) * HEAD_PAD) for hh in heads]
        qs = [q_ref[0, :, lanes[hh]] for hh in heads]
        m_run = [None, None]
        acc = [jnp.zeros((HEAD_PAD, tq), F32) for _ in heads]
        for c in range(n_keys // MXU_COLS):
            rows = slice(c * MXU_COLS, (c + 1) * MXU_COLS)
            for hh in heads:
                st = lax.dot_general(k_ref[0, rows, lanes[hh]], qs[hh], (((1,), (1,)), ((), ())),
                                     preferred_element_type=F32)
                s_new[hh, rows, :] = st
                cm = jnp.max(st, axis=0, keepdims=True)
                m_run[hh] = cm if c == 0 else jnp.maximum(m_run[hh], cm)
                pt = jnp.exp2(s_prev[hh, rows, :] - m_prev[hh]).astype(BF16)
                acc[hh] = acc[hh] + jnp.dot(vt_ref[0, lanes[hh], rows], pt, preferred_element_type=F32)
        for hh in heads:
            m_new[hh] = m_run[hh]
        outs = [acc[hh][0:V_HEAD] / acc[hh][V_HEAD:V_HEAD + 1] for hh in heads]
        o_ref[0] = jnp.concatenate(outs, axis=0).T.astype(BF16)
        for w_ref, wb_ref in zip(w_refs, wb_refs):
            wb_ref[...] = w_ref[...].astype(BF16)

    @pl.when(g % 2 == 0)
    def _():
        step(s_even, m_even, s_odd, m_odd)

    @pl.when(g % 2 == 1)
    def _():
        step(s_odd, m_odd, s_even, m_even)


def _attention(q, k, vt, s, tq, to_bf16):
    b, ltot, _ = q.shape
    pairs = MLA_HEADS // 2
    nq = s // tq
    n_items = b * pairs * nq
    flat = [w.reshape(-1, w.shape[-1]) for w in to_bf16]
    slabs = [w.shape[0] // n_items for w in flat]
    assert all(w.shape[0] == sl * n_items and sl % (2 * SUBLANE) == 0 for w, sl in zip(flat, slabs))
    slab_specs = [pl.BlockSpec((sl, w.shape[1]), lambda g: (jnp.minimum(g, n_items - 1), 0))
                  for w, sl in zip(flat, slabs)]

    def item(g):
        return g // (pairs * nq), (g // nq) % pairs, g % nq

    def cur(g):
        return item(jnp.minimum(g, n_items - 1))

    def prev(g):
        return item(jnp.maximum(g - 1, 0))

    scores = pltpu.VMEM((2, ltot, tq), F32)
    colmax = pltpu.VMEM((2, 1, tq), F32)
    out = pl.pallas_call(
        functools.partial(_attn_kernel, n_cast=len(flat)),
        grid=(n_items + 1,),
        in_specs=[pl.BlockSpec((1, tq, 2 * HEAD_PAD), lambda g: (cur(g)[0], cur(g)[2], cur(g)[1])),
                  pl.BlockSpec((1, ltot, 2 * HEAD_PAD), lambda g: (cur(g)[0], 0, cur(g)[1])),
                  pl.BlockSpec((1, 2 * HEAD_PAD, ltot), lambda g: (prev(g)[0], prev(g)[1], 0))] + slab_specs,
        out_specs=[pl.BlockSpec((1, tq, 2 * V_HEAD), lambda g: (prev(g)[0], prev(g)[2], prev(g)[1]))] + slab_specs,
        out_shape=[jax.ShapeDtypeStruct((b, s, MLA_HEADS * V_HEAD), BF16)]
        + [jax.ShapeDtypeStruct(w.shape, BF16) for w in flat],
        scratch_shapes=[scores, scores, colmax, colmax],
        compiler_params=_cparams(("arbitrary",)),
        name="attn",
    )(q, k, vt, *flat)
    return out[0], [wb.reshape(w.shape) for wb, w in zip(out[1:], to_bf16)]


def _merge_kernel(rnn_ref, attn_ref, sg_ref, x_ref, mod_ref, modp_ref, g2_ref, wr_ref, wm_ref, wo_ref, wrt_ref,
                  xn_ref, h2_ref, aff_ref, xn_even, xn_odd, *, d):
    g = pl.program_id(0)

    @pl.when(g == 0)
    def _():
        xn_odd[...] = jnp.zeros_like(xn_odd)

    def step(xn_new, xn_prev):
        pr = jnp.dot(rnn_ref[0], wr_ref[...], preferred_element_type=F32)
        pa = jnp.dot(attn_ref[0], wm_ref[...], preferred_element_type=F32)
        merged = (sg_ref[0, :, 0:d] * pr + sg_ref[0, :, d:2 * d] * pa).astype(BF16)

        xp = xn_prev[...]
        inv = lax.rsqrt(jnp.mean(xp * xp, axis=-1, keepdims=True) + EPS)
        h2 = (xp * inv * g2_ref[...]) * (1.0 + modp_ref[0, :, 4 * d:5 * d]) + modp_ref[0, :, 3 * d:4 * d]
        n_tok = h2.shape[0]
        for t in range(d // LANE):
            h2_ref[0, pl.ds(t, n_tok, stride=d // LANE), :] = h2[:, t * LANE:(t + 1) * LANE]
        logits = jnp.dot(h2.astype(BF16), wrt_ref[...], preferred_element_type=F32)
        lane = lax.broadcasted_iota(jnp.int32, (1, LANE), 1)
        logits = jnp.where(lane < N_EXPERTS, logits, -jnp.inf)
        e = jnp.exp(logits - jnp.max(logits, axis=-1, keepdims=True))
        aff_ref[0] = (e / jnp.sum(e, axis=-1, keepdims=True)).T[0:N_EXPERTS]

        xn = x_ref[0] + mod_ref[0, :, 2 * d:3 * d] * jnp.dot(merged, wo_ref[...], preferred_element_type=F32)
        xn_ref[0] = xn
        xn_new[...] = xn

    @pl.when(g % 2 == 0)
    def _():
        step(xn_even, xn_odd)

    @pl.when(g % 2 == 1)
    def _():
        step(xn_odd, xn_even)


def _merge(rnn, attn, sg, x, mod, g2, wr, wm, wo, wrt, tile):
    b, s, d = x.shape
    nt = s // tile
    n_items = b * nt
    const = lambda g: (0, 0)
    cur = lambda g: jnp.minimum(g, n_items - 1)
    prev = lambda g: jnp.maximum(g - 1, 0)
    row = lambda g: (cur(g) // nt, cur(g) % nt, 0)
    row_prev = lambda g: (prev(g) // nt, prev(g) % nt, 0)
    return pl.pallas_call(
        functools.partial(_merge_kernel, d=d),
        grid=(n_items + 1,),
        in_specs=[pl.BlockSpec((1, tile, d), row),
                  pl.BlockSpec((1, tile, attn.shape[2]), row),
                  pl.BlockSpec((1, tile, 2 * d), row),
                  pl.BlockSpec((1, tile, d), row),
                  pl.BlockSpec((1, 1, N_MOD * d), lambda g: (2 * (cur(g) // nt), 0, 0)),
                  pl.BlockSpec((1, 1, N_MOD * d), lambda g: (2 * (prev(g) // nt), 0, 0)),
                  pl.BlockSpec((1, d), const),
                  pl.BlockSpec(wr.shape, const),
                  pl.BlockSpec(wm.shape, const),
                  pl.BlockSpec(wo.shape, const),
                  pl.BlockSpec(wrt.shape, const)],
        out_specs=[pl.BlockSpec((1, tile, d), row),
                   pl.BlockSpec((1, tile * (d // LANE), LANE), row_prev),
                   pl.BlockSpec((1, N_EXPERTS, tile), lambda g: (prev(g) // nt, 0, prev(g) % nt))],
        out_shape=(jax.ShapeDtypeStruct((b, s, d), F32),
                   jax.ShapeDtypeStruct((b, s * (d // LANE), LANE), F32),
                   jax.ShapeDtypeStruct((b, N_EXPERTS, s), F32)),
        scratch_shapes=[pltpu.VMEM((tile, d), F32), pltpu.VMEM((tile, d), F32)],
        compiler_params=_cparams(("arbitrary",)),
        name="merge",
    )(rnn, attn, sg, x, mod, mod, g2, wr, wm, wo, wrt)


def _excl_cumsum_lanes(x01, tri):
    rows, length = x01.shape
    offset = jnp.zeros((rows, 1), F32)
    pieces = []
    for blk in range(length // LANE):
        xb = x01[:, blk * LANE:(blk + 1) * LANE]
        inc = jnp.dot(xb.astype(BF16), tri, preferred_element_type=F32)
        pieces.append(inc - xb + offset)
        offset = offset + jnp.sum(xb, axis=-1, keepdims=True)
    return jnp.concatenate(pieces, axis=-1)


def _topk_kernel(aff_ref, idx_ref, gate_ref, pos_s, sel_s, *, cap):
    a = aff_ref[0]
    n_e, length = a.shape

    def as_float(bits):
        return lax.bitcast_convert_type(bits, F32)

    def bisect(i, cand):
        t = cand | (jnp.int32(1) << (30 - i))
        cnt = jnp.sum((a >= as_float(t)).astype(jnp.int32), axis=-1, keepdims=True)
        return jnp.where(cnt >= cap, t, cand)

    thr_bits = lax.fori_loop(0, 31, bisect, jnp.zeros((n_e, 1), jnp.int32))
    thr = as_float(thr_bits)
    above = as_float(thr_bits + 1)
    gt = (a >= above).astype(F32)
    eq = jnp.logical_and(a >= thr, a < above).astype(F32)
    need = cap - jnp.sum(gt, axis=-1, keepdims=True)
    r_i = lax.broadcasted_iota(jnp.int32, (LANE, LANE), 0)
    c_i = lax.broadcasted_iota(jnp.int32, (LANE, LANE), 1)
    tri = (r_i <= c_i).astype(BF16)
    tie_rank = _excl_cumsum_lanes(eq, tri)
    sel = gt + eq * (tie_rank < need).astype(F32)
    pos_s[...] = _excl_cumsum_lanes(sel, tri)
    sel_s[...] = sel

    n_hi = idx_ref.shape[2]
    tok = lax.broadcasted_iota(jnp.int32, (1, length), 1)
    tok_hi = (tok >> 6).astype(F32)
    tok_lo = (tok & 63).astype(F32)
    a_iota = lax.broadcasted_iota(jnp.int32, (n_hi, 1), 0)
    c_iota = lax.broadcasted_iota(jnp.int32, (SLOT_LO, 1), 0)

    def compact(e, carry):
        pos_row = pos_s[pl.ds(e, 1), :].astype(jnp.int32)
        sel_row = sel_s[pl.ds(e, 1), :] > 0.5
        hot_a = jnp.where(jnp.logical_and((pos_row // SLOT_LO) == a_iota, sel_row), 1.0, 0.0)
        hot_c = jnp.where((pos_row % SLOT_LO) == c_iota, 1.0, 0.0).astype(BF16)
        g = aff_ref[0, pl.ds(e, 1), :]
        g1 = g.astype(BF16).astype(F32)
        g2 = (g - g1).astype(BF16).astype(F32)
        g3 = g - g1 - g2
        lhs = jnp.concatenate([hot_a * v for v in (tok_hi, tok_lo, g1, g2, g3)], axis=0).astype(BF16)
        res = lax.dot_general(lhs, hot_c, (((1,), (1,)), ((), ())), preferred_element_type=F32)
        part = [res[v * n_hi:(v + 1) * n_hi] for v in range(5)]
        idx_ref[0, e] = (part[0] * 64.0 + part[1]).astype(jnp.int32)
        gate_ref[0, e] = part[2] + part[3] + part[4]
        return carry

    lax.fori_loop(0, n_e, compact, 0)


def _topk(aff_t, cap):
    b, n_e, length = aff_t.shape
    assert length <= 64 * 256, "token ids travel through the compaction matmul as two bf16-exact digits (t >> 6, t & 63)"
    n_hi = max(cap // SLOT_LO, SUBLANE)
    blk = lambda bb: (bb, 0, 0)
    blk4 = lambda bb: (bb, 0, 0, 0)
    idx, gate = pl.pallas_call(
        functools.partial(_topk_kernel, cap=cap),
        grid=(b,),
        in_specs=[pl.BlockSpec((1, n_e, length), blk)],
        out_specs=[pl.BlockSpec((1, n_e, n_hi, SLOT_LO), blk4), pl.BlockSpec((1, n_e, n_hi, SLOT_LO), blk4)],
        out_shape=(jax.ShapeDtypeStruct((b, n_e, n_hi, SLOT_LO), jnp.int32),
                   jax.ShapeDtypeStruct((b, n_e, n_hi, SLOT_LO), F32)),
        scratch_shapes=[pltpu.VMEM((n_e, length), F32), pltpu.VMEM((n_e, length), F32)],
        compiler_params=_cparams(("arbitrary",)),
        name="topk",
    )(aff_t)
    flat = lambda a: a.reshape(b * n_e, 1, n_hi * SLOT_LO)[:, :, :cap]
    return flat(idx), flat(gate)


def _gather_kernel(idx_ref, h_ref, o_ref, rows_s, *, cap, d):
    tiles = d // LANE

    def body(j, carry):
        src = pl.multiple_of(idx_ref[0, 0, j] * tiles, tiles)
        dst = pl.multiple_of(j * tiles, tiles)
        rows_s[pl.ds(dst, tiles), :] = h_ref[0, pl.ds(src, tiles), :]
        return carry

    lax.fori_loop(0, cap, body, 0, unroll=8)
    o_ref[0, 0] = jnp.concatenate([rows_s[pl.ds(t, cap, stride=tiles), :] for t in range(tiles)],
                                  axis=-1).astype(BF16)


def _gather(idx, h2t, cap, d):
    b, rows, lanes = h2t.shape
    assert d // lanes == SUBLANE and lanes == LANE
    return pl.pallas_call(
        functools.partial(_gather_kernel, cap=cap, d=d),
        grid=(b, N_EXPERTS),
        in_specs=[pl.BlockSpec((1, 1, cap), lambda bb, e: (bb * N_EXPERTS + e, 0, 0), memory_space=pltpu.SMEM),
                  pl.BlockSpec((1, rows, lanes), lambda bb, e: (bb, 0, 0))],
        out_specs=pl.BlockSpec((1, 1, cap, d), lambda bb, e: (bb, e, 0, 0)),
        out_shape=jax.ShapeDtypeStruct((b, N_EXPERTS, cap, d), BF16),
        scratch_shapes=[pltpu.VMEM((cap * (d // LANE), LANE), F32)],
        compiler_params=_cparams(("arbitrary", "arbitrary")),
        name="gather",
    )(idx, h2t)


def _moe_kernel(idx_ref, gate_ref, x_ref, wg_ref, wu_ref, wd_ref, mod_ref, res_ref, o_ref, y_even, y_odd, *, d, cap):
    g = pl.program_id(0)
    e_prev = jnp.maximum(g - 1, 0) % N_EXPERTS
    piece = res_ref.shape[1]

    @pl.when(g == 0)
    def _():
        y_odd[...] = jnp.zeros_like(y_odd)

    @pl.when(e_prev == 0)
    def _():
        o_ref[...] = jnp.zeros_like(o_ref)

    def step(y_new, y_prev):
        xb = x_ref[0, 0]
        gg = jnp.dot(xb, wg_ref[0], preferred_element_type=F32)
        uu = jnp.dot(xb, wu_ref[0], preferred_element_type=F32)
        hid = (gg * jax.nn.sigmoid(gg) * uu).astype(BF16)
        y_new[...] = jnp.dot(hid, wd_ref[0], preferred_element_type=F32) * mod_ref[0, :, 5 * d:6 * d]

        rows = pl.ds(pl.multiple_of(e_prev * piece, piece), piece)
        o_ref[0, rows, :] = o_ref[0, rows, :] + res_ref[0]
        for j in range(cap):
            r = idx_ref[0, 0, j]
            o_ref[0, pl.ds(r, 1), :] = o_ref[0, pl.ds(r, 1), :] + gate_ref[0, 0, j] * y_prev[j:j + 1, :]

    @pl.when(g % 2 == 0)
    def _():
        step(y_even, y_odd)

    @pl.when(g % 2 == 1)
    def _():
        step(y_odd, y_even)


def _moe(idx, gate, xg, w_gate, w_up, w_down, mod, x_new):
    b, n_e, cap, d = xg.shape
    s = x_new.shape[1]
    f = w_gate.shape[2]
    n_items = b * n_e
    cur = lambda g: jnp.minimum(g, n_items - 1)
    prev = lambda g: jnp.maximum(g - 1, 0)
    smem = lambda: pl.BlockSpec((1, 1, cap), lambda g: (prev(g), 0, 0), memory_space=pltpu.SMEM)
    wspec = lambda shape: pl.BlockSpec(shape, lambda g: (cur(g) % n_e, 0, 0))
    return pl.pallas_call(
        functools.partial(_moe_kernel, d=d, cap=cap),
        grid=(n_items + 1,),
        in_specs=[smem(), smem(),
                  pl.BlockSpec((1, 1, cap, d), lambda g: (cur(g) // n_e, cur(g) % n_e, 0, 0)),
                  wspec((1, d, f)), wspec((1, d, f)), wspec((1, f, d)),
                  pl.BlockSpec((1, 1, N_MOD * d), lambda g: (2 * (cur(g) // n_e), 0, 0)),
                  pl.BlockSpec((1, s // n_e, d), lambda g: (prev(g) // n_e, prev(g) % n_e, 0))],
        out_specs=pl.BlockSpec((1, s, d), lambda g: (prev(g) // n_e, 0, 0), pipeline_mode=pl.Buffered(1)),
        out_shape=jax.ShapeDtypeStruct((b, s, d), F32),
        scratch_shapes=[pltpu.VMEM((cap, d), F32), pltpu.VMEM((cap, d), F32)],
        compiler_params=_cparams(("arbitrary",)),
        name="moe",
    )(idx, gate, xg, w_gate, w_up, w_down, mod, x_new)


def _pad_lanes(a, width):
    return jnp.pad(a, [(0, 0)] * (a.ndim - 1) + [(0, width - a.shape[-1])])


def _rope_tables(s, lc, g_q_nope, g_q_rope, g_k_nope, g_k_rope):
    ltot = s + lc
    n_pairs = QK_ROPE // 4
    lane = jnp.arange(LANE)
    in_rope = jnp.logical_and(lane >= QK_NOPE, lane < QK_NOPE + QK_ROPE)
    angle_id = jnp.where(in_rope, (lane - QK_NOPE) % HALF_ROPE, 0)
    inv_freq = ROPE_BASE ** (-jnp.arange(n_pairs, dtype=F32) / n_pairs)
    freq = jnp.where(in_rope, inv_freq[angle_id % n_pairs], 0.0)
    t = jnp.arange(ltot)
    pos = jnp.where((angle_id >= n_pairs)[None, :], (t % GRID_W)[:, None], (t // GRID_W)[:, None]).astype(F32)
    ang = jnp.where((t < s)[:, None], pos * freq[None, :], 0.0)
    cos, sin = jnp.cos(ang), jnp.sin(ang)
    z_nope = jnp.zeros((QK_NOPE,), F32)
    z_tail = jnp.zeros((LANE - QK_NOPE - QK_ROPE,), F32)

    def tables(g_nope, g_rope, scale):
        g_cos = jnp.concatenate([g_nope, g_rope, z_tail]) * scale
        g_sin = jnp.concatenate([z_nope, -g_rope[HALF_ROPE:], g_rope[:HALF_ROPE], z_tail]) * scale
        return cos * g_cos[None, :], sin * g_sin[None, :]

    tqc, tqs = tables(g_q_nope, g_q_rope, ATTN_SCALE * LOG2_E)
    tkc, tks = tables(z_nope, g_k_rope, 1.0)
    gkn = _pad_lanes(g_k_nope[None, :], LANE)
    return tqc, tqs, tkc, tks, gkn


def _swap_halves(a):
    return jnp.concatenate([a[..., HALF_ROPE:], a[..., :HALF_ROPE]], axis=-1)


def _layout_weights(w_in, w_uq, w_uk, w_uv, d):
    o_q = 2 * d
    o_kv = o_q + Q_LORA
    o_kr = o_kv + KV_LORA
    o_g = o_kr + QK_ROPE
    kr = w_in[:, o_kr:o_g]
    zl = jnp.zeros((w_in.shape[0], QK_NOPE), F32)
    kr_a = _pad_lanes(jnp.concatenate([zl, kr], axis=-1), LANE)
    kr_s = _pad_lanes(jnp.concatenate([zl, _swap_halves(kr)], axis=-1), LANE)
    win_p = jnp.concatenate([w_in[:, :o_kr], kr_a, kr_s, w_in[:, o_g:]], axis=-1).astype(BF16)

    uq = w_uq.reshape(Q_LORA, MLA_HEADS, QK_NOPE + QK_ROPE)
    nope, rope = uq[..., :QK_NOPE], uq[..., QK_NOPE:]
    wuq_p = _pad_lanes(jnp.concatenate([nope, rope], axis=-1), HEAD_PAD).reshape(Q_LORA, -1).astype(BF16)
    wuq_s = _pad_lanes(jnp.concatenate([jnp.zeros_like(nope), _swap_halves(rope)], axis=-1),
                       HEAD_PAD).reshape(Q_LORA, -1).astype(BF16)
    wuk_p = _pad_lanes(w_uk.reshape(KV_LORA, MLA_HEADS, QK_NOPE), HEAD_PAD).reshape(KV_LORA, -1).astype(BF16)
    wuv_p = _pad_lanes(w_uv.reshape(KV_LORA, MLA_HEADS, V_HEAD), HEAD_PAD).reshape(KV_LORA, -1).astype(BF16)
    return win_p, wuq_p, wuq_s, wuk_p, wuv_p


def _pair_blocks(w):
    *lead, nb, n, _ = w.shape
    w = w.reshape(*lead, nb // 2, 2, n, n)
    paired = jnp.einsum('...irc,ij->...irjc', w, jnp.eye(2, dtype=w.dtype))
    return paired.reshape(*lead, nb // 2, 2 * n, 2 * n).astype(BF16)


def kernel(x, c, ctx, c_ctx, w_ada, b_ada, g_norm1, g_norm2, w_in, conv_w, conv_b, rg_wa, rg_ba, rg_wx, rg_bx,
           rg_lambda, w_rnn_out, g_q_lora, w_uq, g_kv_lora, w_uk, w_uv, g_q_nope, g_q_rope, g_k_nope, g_k_rope,
           w_mla_out, w_o, w_router, w_e_gate, w_e_up, w_e_down):
    assert w_ada.shape[0] == 1, "single-layer kernel"
    b, s, d = x.shape
    lc = ctx.shape[1]
    tile = lc
    n_lat = s // tile
    cap = CAPACITY_FACTOR * s // N_EXPERTS

    pad_rows = (-(b + 1)) % SUBLANE
    cc = jnp.concatenate([c, c_ctx[None, :], jnp.zeros((pad_rows, d), F32)], axis=0)
    mod_all = _ada(cc, w_ada[0], b_ada[0])
    mod = jnp.stack([mod_all[:b], jnp.broadcast_to(mod_all[b], (b, N_MOD * d))], axis=1).reshape(2 * b, 1, N_MOD * d)

    win_p, wuq_p, wuq_s, wuk_p, wuv_p = _layout_weights(w_in[0], w_uq[0], w_uk[0], w_uv[0], d)
    tqc, tqs, tkc, tks, gkn = _rope_tables(s, lc, g_q_nope[0], g_q_rope[0], g_k_nope[0], g_k_rope[0])
    z0, gz, q, k, v, sg = _inproj(x, ctx, mod, g_norm1, win_p, g_q_lora, wuq_p, wuq_s, g_kv_lora, wuk_p,
                                  wuv_p, gkn, tqc, tqs, tkc, tks, tile)

    gate_w = _pair_blocks(jnp.stack([rg_wa[0], rg_wx[0]]))
    h_f, xr = _rnn(z0, gate_w[0, 0], gate_w[1, 0], rg_ba[0, 0:1], rg_bx[0, 0:1],
                   rg_lambda[0, 0:1], tile, n_lat, False, conv=(conv_w[0], conv_b))
    rnn = _rnn(xr, gate_w[0, 1], gate_w[1, 1], rg_ba[0, 1:2], rg_bx[0, 1:2],
               rg_lambda[0, 1:2], tile, n_lat, True, hf=h_f, gz=gz)

    attn, (wg_b, wu_b, wd_b) = _attention(q, k, v, s, 2 * MXU_COLS, (w_e_gate[0], w_e_up[0], w_e_down[0]))

    x_new, h2, aff = _merge(rnn, attn, sg, x, mod, g_norm2, w_rnn_out[0].astype(BF16), w_mla_out[0].astype(BF16),
                            w_o[0].astype(BF16), _pad_lanes(w_router[0], LANE).astype(BF16), 2 * tile)

    idx, gate = _topk(aff, cap)
    xg = _gather(idx, h2, cap, d)
    return _moe(idx, gate, xg, wg_b, wu_b, wd_b, mod, x_new)
```

```python
import functools

import jax
import jax.numpy as jnp
from jax import lax
from jax.experimental import pallas as pl
from jax.experimental.pallas import tpu as pltpu

EPS = 1e-6
N_MOD = 6
GRID_W = 64
RG_BLOCKS = 8
CONV_W = 4
CONV_PAD_L = 2
RG_C = 8.0
MLA_HEADS = 8
QK_NOPE = 64
QK_ROPE = 32
V_HEAD = 64
Q_LORA = 256
KV_LORA = 128
ROPE_BASE = 10000.0
ATTN_SCALE = (QK_NOPE + QK_ROPE) ** -0.5
LOG2_E = 1.4426950408889634
N_EXPERTS = 16
CAPACITY_FACTOR = 2

LANE = 128
SUBLANE = 8
MXU_COLS = 256
HEAD_PAD = LANE
HALF_ROPE = QK_ROPE // 2
RNN_SUB = 64
RNN_SAMPLES = 4
TOPK_SAMPLES = 4
SLOT_LO = 32
VMEM_LIMIT = 56 * 1024 * 1024

BF16 = jnp.bfloat16
F32 = jnp.float32


def _cparams(sem):
    return pltpu.CompilerParams(dimension_semantics=sem, vmem_limit_bytes=VMEM_LIMIT)


def _ada_kernel(c_ref, w_ref, b_ref, o_ref):
    cc = c_ref[...]
    s = cc * jax.nn.sigmoid(cc)
    o_ref[...] = jnp.dot(s, w_ref[...], preferred_element_type=F32) + b_ref[...]


def _ada(cc, w_ada, b_ada):
    rows, d = cc.shape
    n = w_ada.shape[1]
    bn = d
    return pl.pallas_call(
        _ada_kernel,
        grid=(n // bn,),
        in_specs=[pl.BlockSpec((rows, d), lambda j: (0, 0)),
                  pl.BlockSpec((d, bn), lambda j: (0, j)),
                  pl.BlockSpec((1, bn), lambda j: (0, j))],
        out_specs=pl.BlockSpec((rows, bn), lambda j: (0, j)),
        out_shape=jax.ShapeDtypeStruct((rows, n), F32),
        compiler_params=_cparams(("arbitrary",)),
        name="ada",
    )(cc, w_ada, b_ada.reshape(1, n))


def _inproj_kernel(x_ref, ctx_ref, mod_ref, g1_ref, win_ref, gq_ref, wuq_ref, wuqs_ref, gkv_ref, wuk_ref,
                   wuv_ref, gkn_ref, tqc_ref, tqs_ref, tkc_ref, tks_ref,
                   z0_ref, gz_ref, q_ref, k_ref, v_ref, sg_ref, *, d, n_lat_tiles):
    i = pl.program_id(1)
    is_ctx = i == n_lat_tiles
    xt = jnp.where(is_ctx, ctx_ref[0], x_ref[0])
    shift = mod_ref[0, :, 0:d]
    scale = mod_ref[0, :, d:2 * d]
    inv = lax.rsqrt(jnp.mean(xt * xt, axis=-1, keepdims=True) + EPS)
    xn = ((xt * inv * g1_ref[...]) * (1.0 + scale) + shift).astype(BF16)

    def seg(lo, hi):
        return jnp.dot(xn, win_ref[:, lo:hi], preferred_element_type=F32)

    o_q = 2 * d
    o_kv = o_q + Q_LORA
    o_kr = o_kv + KV_LORA
    o_g = o_kr + 2 * LANE
    zq = seg(o_q, o_kv)
    zkv = seg(o_kv, o_kr)
    kr_a = seg(o_kr, o_kr + LANE)
    kr_s = seg(o_kr + LANE, o_g)

    lane = lax.broadcasted_iota(jnp.int32, (1, LANE), 1)
    is_nope = lane < QK_NOPE
    is_rope = jnp.logical_and(lane >= QK_NOPE, lane < QK_NOPE + QK_ROPE)

    zqn = (zq * lax.rsqrt(jnp.mean(zq * zq, axis=-1, keepdims=True) + EPS) * gq_ref[...]).astype(BF16)
    qa = jnp.dot(zqn, wuq_ref[...], preferred_element_type=F32)
    qs = jnp.dot(zqn, wuqs_ref[...], preferred_element_type=F32)
    tqc = tqc_ref[...]
    tqs = tqs_ref[...]
    for h in range(MLA_HEADS):
        a = qa[:, h * HEAD_PAD:(h + 1) * HEAD_PAD]
        s = qs[:, h * HEAD_PAD:(h + 1) * HEAD_PAD]
        sq = a * a
        ssn = jnp.sum(jnp.where(is_nope, sq, 0.0), axis=-1, keepdims=True)
        ssr = jnp.sum(jnp.where(is_rope, sq, 0.0), axis=-1, keepdims=True)
        invh = jnp.where(is_nope, lax.rsqrt(ssn * (1.0 / QK_NOPE) + EPS), lax.rsqrt(ssr * (1.0 / QK_ROPE) + EPS))
        q_ref[0, :, h * HEAD_PAD:(h + 1) * HEAD_PAD] = (invh * (a * tqc + s * tqs)).astype(BF16)

    ckv = (zkv * lax.rsqrt(jnp.mean(zkv * zkv, axis=-1, keepdims=True) + EPS) * gkv_ref[...]).astype(BF16)
    lane_h = lax.broadcasted_iota(jnp.int32, (1, MLA_HEADS * HEAD_PAD), 1) % HEAD_PAD
    vp = jnp.dot(ckv, wuv_ref[...], preferred_element_type=F32) + jnp.where(lane_h == V_HEAD, 1.0, 0.0)
    v_ref[0] = vp.T.astype(BF16)
    ka = jnp.dot(ckv, wuk_ref[...], preferred_element_type=F32)
    inv_r = lax.rsqrt(jnp.sum(kr_a * kr_a, axis=-1, keepdims=True) * (1.0 / QK_ROPE) + EPS)
    kr = inv_r * (kr_a * tkc_ref[...] + kr_s * tks_ref[...])
    gkn = gkn_ref[...]
    for h in range(MLA_HEADS):
        a = ka[:, h * HEAD_PAD:(h + 1) * HEAD_PAD]
        invh = lax.rsqrt(jnp.sum(a * a, axis=-1, keepdims=True) * (1.0 / QK_NOPE) + EPS)
        k_ref[0, :, h * HEAD_PAD:(h + 1) * HEAD_PAD] = (a * invh * gkn + kr).astype(BF16)

    z0_ref[0] = seg(0, d)
    gz_ref[0] = jax.nn.gelu(seg(d, o_q)).astype(BF16)
    sg_ref[0] = jax.nn.sigmoid(seg(o_g, o_g + 2 * d)).astype(BF16)


def _inproj(x, ctx, mod, g1, win_p, gq, wuq_p, wuq_s, gkv, wuk_p, wuv, gkn, tqc, tqs, tkc, tks, tile):
    b, s, d = x.shape
    lc = ctx.shape[1]
    assert lc == tile and s % tile == 0
    nl = s // tile
    ltot = s + lc
    n_in = win_p.shape[1]
    hp = MLA_HEADS * HEAD_PAD
    const = lambda bb, i: (0, 0)
    row = lambda bb, i: (bb, i, 0)
    tab = lambda bb, i: (i, 0)
    out_shapes = (
        jax.ShapeDtypeStruct((b, ltot, d), F32),
        jax.ShapeDtypeStruct((b, ltot, d), BF16),
        jax.ShapeDtypeStruct((b, ltot, hp), BF16),
        jax.ShapeDtypeStruct((b, ltot, hp), BF16),
        jax.ShapeDtypeStruct((b, hp, ltot), BF16),
        jax.ShapeDtypeStruct((b, ltot, 2 * d), BF16),
    )
    return pl.pallas_call(
        functools.partial(_inproj_kernel, d=d, n_lat_tiles=nl),
        grid=(b, nl + 1),
        in_specs=[
            pl.BlockSpec((1, tile, d), lambda bb, i: (bb, jnp.minimum(i, nl - 1), 0)),
            pl.BlockSpec((1, tile, d), lambda bb, i: (bb, 0, 0)),
            pl.BlockSpec((1, 1, N_MOD * d), lambda bb, i: (2 * bb + i // nl, 0, 0)),
            pl.BlockSpec((1, d), const),
            pl.BlockSpec((d, n_in), const),
            pl.BlockSpec((1, Q_LORA), const),
            pl.BlockSpec((Q_LORA, hp), const),
            pl.BlockSpec((Q_LORA, hp), const),
            pl.BlockSpec((1, KV_LORA), const),
            pl.BlockSpec((KV_LORA, hp), const),
            pl.BlockSpec((KV_LORA, hp), const),
            pl.BlockSpec((1, LANE), const),
            pl.BlockSpec((tile, LANE), tab),
            pl.BlockSpec((tile, LANE), tab),
            pl.BlockSpec((tile, LANE), tab),
            pl.BlockSpec((tile, LANE), tab),
        ],
        out_specs=[
            pl.BlockSpec((1, tile, d), row),
            pl.BlockSpec((1, tile, d), row),
            pl.BlockSpec((1, tile, hp), row),
            pl.BlockSpec((1, tile, hp), row),
            pl.BlockSpec((1, hp, tile), lambda bb, i: (bb, 0, i)),
            pl.BlockSpec((1, tile, 2 * d), row),
        ],
        out_shape=out_shapes,
        compiler_params=_cparams(("arbitrary", "arbitrary")),
        name="inproj",
    )(x, ctx, mod, g1, win_p, gq, wuq_p, wuq_s, gkv, wuk_p, wuv, gkn, tqc, tqs, tkc, tks)


def _sigmoid(x):
    return 0.5 * jnp.tanh(0.5 * x) + 0.5


def _softplus(x):
    e = jnp.exp(-jnp.abs(x))
    u = 1.0 + e
    tiny = u == 1.0
    log1p_e = jnp.where(tiny, e, jnp.log(u) * (e / jnp.where(tiny, 1.0, u - 1.0)))
    return jnp.maximum(x, 0.0) + log1p_e


def _rnn_kernel(*refs, d, tile, n_lat, reverse):
    if reverse:
        (xr_ref, wa_ref, wx_ref, ba_ref, bx_ref, lam_ref, hf_ref, gz_ref, o_ref, a_s, b_s, hb_s, h_s) = refs
    else:
        (z_ref, zp_ref, zn_ref, cw_ref, cb_ref, wa_ref, wx_ref, ba_ref, bx_ref, lam_ref,
         o_ref, xr_ref, a_s, b_s, h_s) = refs
    j = pl.program_id(1)
    nb = a_s.shape[0]
    if not reverse:
        c = jnp.where(j == 0, n_lat, j - 1)
        prev_zero = jnp.logical_or(c == 0, c == n_lat)
        next_zero = c >= n_lat - 1
        n_ext = tile + 2 * SUBLANE
        taps = []
        for bi in range(nb):
            ext = jnp.concatenate([jnp.where(prev_zero, 0.0, zp_ref[bi]), z_ref[bi],
                                   jnp.where(next_zero, 0.0, zn_ref[bi])], axis=0)
            taps.append([ext if (CONV_PAD_L - k) % n_ext == 0 else pltpu.roll(ext, (CONV_PAD_L - k) % n_ext, 0)
                         for k in range(CONV_W)])
    pair = 2 * (d // RG_BLOCKS)
    neg_c_softplus = -RG_C * _softplus(-lam_ref[...])

    hs = [jnp.where(j == 0, 0.0, h_s[bi]) for bi in range(nb)]
    subs = range(tile // RNN_SUB)
    for sb in (reversed(subs) if reverse else subs):
        r0 = sb * RNN_SUB
        for bi in range(nb):
            dst = hb_s.at[bi] if reverse else o_ref.at[bi]
            if reverse:
                xr = xr_ref[bi, r0:r0 + RNN_SUB, :]
            else:
                xr = cb_ref[...]
                for k in range(CONV_W):
                    xr = xr + taps[bi][k][SUBLANE + r0:SUBLANE + r0 + RNN_SUB] * cw_ref[k:k + 1, :]
                xr_ref[bi, r0:r0 + RNN_SUB, :] = xr
            xb = xr.astype(BF16)
            ya = jnp.concatenate([jnp.dot(xb[:, m * pair:(m + 1) * pair], wa_ref[m], preferred_element_type=F32)
                                  for m in range(d // pair)], axis=-1)
            yx = jnp.concatenate([jnp.dot(xb[:, m * pair:(m + 1) * pair], wx_ref[m], preferred_element_type=F32)
                                  for m in range(d // pair)], axis=-1)
            r = _sigmoid(ya + ba_ref[...])
            gi = _sigmoid(yx + bx_ref[...])
            log_a = neg_c_softplus * r
            a_s[bi, r0:r0 + RNN_SUB, :] = jnp.exp(log_a)
            th = jnp.tanh(log_a)
            v = -2.0 * th / (1.0 - th)
            b_s[bi, r0:r0 + RNN_SUB, :] = (v * lax.rsqrt(jnp.maximum(v, 1e-30))) * (gi * xr)
            rows = range(r0, r0 + RNN_SUB)
            h = hs[bi]
            for t in (reversed(rows) if reverse else rows):
                h = a_s[bi, t:t + 1, :] * h + b_s[bi, t:t + 1, :]
                dst[t:t + 1, :] = h
            hs[bi] = h
    for bi in range(nb):
        h_s[bi] = hs[bi]
    if reverse:
        o_ref[...] = ((hf_ref[...] + hb_s[...]) * gz_ref[...]).astype(BF16)


def _rnn(z, wa_p, wx_p, ba, bx, lam, tile, n_lat, reverse, conv=None, hf=None, gz=None):
    b, ltot, d = z.shape
    nb = RNN_SAMPLES if b % RNN_SAMPLES == 0 else 1
    n_halo = ltot // SUBLANE
    per = tile // SUBLANE
    if reverse:
        cidx = lambda j: jnp.where(j == 0, n_lat, n_lat - j)
        oidx = lambda j: jnp.where(j == 0, n_lat - 1, n_lat - j)
    else:
        cidx = lambda j: jnp.where(j == 0, n_lat, j - 1)
        oidx = lambda j: jnp.maximum(j - 1, 0)
    const2 = lambda bb, j: (0, 0)
    const3 = lambda bb, j: (0, 0, 0)
    chunk = pl.BlockSpec((nb, tile, d), lambda bb, j: (bb, cidx(j), 0))
    out_chunk = pl.BlockSpec((nb, tile, d), lambda bb, j: (bb, oidx(j), 0))
    gate_specs = [pl.BlockSpec(wa_p.shape, const3), pl.BlockSpec(wx_p.shape, const3),
                  pl.BlockSpec((1, d), const2), pl.BlockSpec((1, d), const2), pl.BlockSpec((1, d), const2)]
    gate_args = [wa_p, wx_p, ba, bx, lam]
    scratch = [pltpu.VMEM((nb, tile, d), F32), pltpu.VMEM((nb, tile, d), F32)]
    if reverse:
        in_specs = [chunk] + gate_specs + [out_chunk, chunk]
        args = [z] + gate_args + [hf, gz]
        out_specs = out_chunk
        out_shape = jax.ShapeDtypeStruct((b, n_lat * tile, d), BF16)
        scratch += [pltpu.VMEM((nb, tile, d), F32)]
    else:
        in_specs = [chunk,
                    pl.BlockSpec((nb, SUBLANE, d), lambda bb, j: (bb, jnp.maximum(cidx(j) * per - 1, 0), 0)),
                    pl.BlockSpec((nb, SUBLANE, d), lambda bb, j: (bb, jnp.minimum((cidx(j) + 1) * per, n_halo - 1), 0)),
                    pl.BlockSpec((CONV_W, d), const2), pl.BlockSpec((1, d), const2)] + gate_specs
        args = [z, z, z, conv[0], conv[1]] + gate_args
        out_specs = [out_chunk, chunk]
        out_shape = (jax.ShapeDtypeStruct((b, n_lat * tile, d), F32), jax.ShapeDtypeStruct((b, ltot, d), F32))
    scratch += [pltpu.VMEM((nb, 1, d), F32)]
    return pl.pallas_call(
        functools.partial(_rnn_kernel, d=d, tile=tile, n_lat=n_lat, reverse=reverse),
        grid=(b // nb, n_lat + 1),
        in_specs=in_specs,
        out_specs=out_specs,
        out_shape=out_shape,
        scratch_shapes=scratch,
        compiler_params=_cparams(("arbitrary", "arbitrary")),
        name="rnn_bwd" if reverse else "rnn_fwd",
    )(*args)


def _attn_kernel(q_ref, k_ref, vt_ref, *rest, n_cast):
    w_refs = rest[:n_cast]
    o_ref = rest[n_cast]
    wb_refs = rest[n_cast + 1:2 * n_cast + 1]
    s_even, s_odd, m_even, m_odd = rest[2 * n_cast + 1:]
    g = pl.program_id(0)

    @pl.when(g == 0)
    def _():
        s_odd[...] = jnp.zeros_like(s_odd)
        m_odd[...] = jnp.zeros_like(m_odd)

    n_keys = k_ref.shape[1]
    tq = q_ref.shape[1]

    def step(s_new, m_new, s_prev, m_prev):
        heads = range(2)
        lanes = [slice(hh * HEAD_PAD, (hh + 1) * HEAD_PAD) for hh in heads]
        qs = [q_ref[0, :, lanes[hh]] for hh in heads]
        m_run = [None, None]
        acc = [jnp.zeros((HEAD_PAD, tq), F32) for _ in heads]
        for c in range(n_keys // MXU_COLS):
            rows = slice(c * MXU_COLS, (c + 1) * MXU_COLS)
            for hh in heads:
                st = lax.dot_general(k_ref[0, rows, lanes[hh]], qs[hh], (((1,), (1,)), ((), ())),
                                     preferred_element_type=F32)
                s_new[hh, rows, :] = st
                cm = jnp.max(st, axis=0, keepdims=True)
                m_run[hh] = cm if c == 0 else jnp.maximum(m_run[hh], cm)
                pt = jnp.exp2(s_prev[hh, rows, :] - m_prev[hh]).astype(BF16)
                acc[hh] = acc[hh] + jnp.dot(vt_ref[0, lanes[hh], rows], pt, preferred_element_type=F32)
        for hh in heads:
            m_new[hh] = m_run[hh]
        outs = [acc[hh][0:V_HEAD] / acc[hh][V_HEAD:V_HEAD + 1] for hh in heads]
        o_ref[0] = jnp.concatenate(outs, axis=0).T.astype(BF16)
        for w_ref, wb_ref in zip(w_refs, wb_refs):
            wb_ref[...] = w_ref[...].astype(BF16)

    @pl.when(g % 2 == 0)
    def _():
        step(s_even, m_even, s_odd, m_odd)

    @pl.when(g % 2 == 1)
    def _():
        step(s_odd, m_odd, s_even, m_even)


def _attention(q, k, vt, s, tq, to_bf16):
    b, ltot, _ = q.shape
    pairs = MLA_HEADS // 2
    nq = s // tq
    n_items = b * pairs * nq
    flat = [w.reshape(-1, w.shape[-1]) for w in to_bf16]
    slabs = [w.shape[0] // n_items for w in flat]
    assert all(w.shape[0] == sl * n_items and sl % (2 * SUBLANE) == 0 for w, sl in zip(flat, slabs))
    slab_specs = [pl.BlockSpec((sl, w.shape[1]), lambda g: (jnp.minimum(g, n_items - 1), 0))
                  for w, sl in zip(flat, slabs)]

    def item(g):
        return g // (pairs * nq), (g // nq) % pairs, g % nq

    def cur(g):
        return item(jnp.minimum(g, n_items - 1))

    def prev(g):
        return item(jnp.maximum(g - 1, 0))

    scores = pltpu.VMEM((2, ltot, tq), F32)
    colmax = pltpu.VMEM((2, 1, tq), F32)
    out = pl.pallas_call(
        functools.partial(_attn_kernel, n_cast=len(flat)),
        grid=(n_items + 1,),
        in_specs=[pl.BlockSpec((1, tq, 2 * HEAD_PAD), lambda g: (cur(g)[0], cur(g)[2], cur(g)[1])),
                  pl.BlockSpec((1, ltot, 2 * HEAD_PAD), lambda g: (cur(g)[0], 0, cur(g)[1])),
                  pl.BlockSpec((1, 2 * HEAD_PAD, ltot), lambda g: (prev(g)[0], prev(g)[1], 0))] + slab_specs,
        out_specs=[pl.BlockSpec((1, tq, 2 * V_HEAD), lambda g: (prev(g)[0], prev(g)[2], prev(g)[1]))] + slab_specs,
        out_shape=[jax.ShapeDtypeStruct((b, s, MLA_HEADS * V_HEAD), BF16)]
        + [jax.ShapeDtypeStruct(w.shape, BF16) for w in flat],
        scratch_shapes=[scores, scores, colmax, colmax],
        compiler_params=_cparams(("arbitrary",)),
        name="attn",
    )(q, k, vt, *flat)
    return out[0], [wb.reshape(w.shape) for wb, w in zip(out[1:], to_bf16)]


def _merge_kernel(rnn_ref, attn_ref, sg_ref, x_ref, mod_ref, modp_ref, g2_ref, wr_ref, wm_ref, wo_ref, wrt_ref,
                  xn_ref, h2_ref, aff_ref, xn_even, xn_odd, *, d):
    g = pl.program_id(0)

    @pl.when(g == 0)
    def _():
        xn_odd[...] = jnp.zeros_like(xn_odd)

    def step(xn_new, xn_prev):
        pr = jnp.dot(rnn_ref[0], wr_ref[...], preferred_element_type=F32)
        pa = jnp.dot(attn_ref[0], wm_ref[...], preferred_element_type=F32)
        merged = (sg_ref[0, :, 0:d] * pr + sg_ref[0, :, d:2 * d] * pa).astype(BF16)

        xp = xn_prev[...]
        inv = lax.rsqrt(jnp.mean(xp * xp, axis=-1, keepdims=True) + EPS)
        h2 = (xp * inv * g2_ref[...]) * (1.0 + modp_ref[0, :, 4 * d:5 * d]) + modp_ref[0, :, 3 * d:4 * d]
        n_tok = h2.shape[0]
        for t in range(d // LANE):
            h2_ref[0, pl.ds(t, n_tok, stride=d // LANE), :] = h2[:, t * LANE:(t + 1) * LANE]
        logits = jnp.dot(h2.astype(BF16), wrt_ref[...], preferred_element_type=F32)
        lane = lax.broadcasted_iota(jnp.int32, (1, LANE), 1)
        logits = jnp.where(lane < N_EXPERTS, logits, -jnp.inf)
        e = jnp.exp(logits - jnp.max(logits, axis=-1, keepdims=True))
        aff_ref[0] = (e / jnp.sum(e, axis=-1, keepdims=True)).T[0:N_EXPERTS]

        xn = x_ref[0] + mod_ref[0, :, 2 * d:3 * d] * jnp.dot(merged, wo_ref[...], preferred_element_type=F32)
        xn_ref[0] = xn
        xn_new[...] = xn

    @pl.when(g % 2 == 0)
    def _():
        step(xn_even, xn_odd)

    @pl.when(g % 2 == 1)
    def _():
        step(xn_odd, xn_even)


def _merge(rnn, attn, sg, x, mod, g2, wr, wm, wo, wrt, tile):
    b, s, d = x.shape
    nt = s // tile
    n_items = b * nt
    const = lambda g: (0, 0)
    cur = lambda g: jnp.minimum(g, n_items - 1)
    prev = lambda g: jnp.maximum(g - 1, 0)
    row = lambda g: (cur(g) // nt, cur(g) % nt, 0)
    row_prev = lambda g: (prev(g) // nt, prev(g) % nt, 0)
    return pl.pallas_call(
        functools.partial(_merge_kernel, d=d),
        grid=(n_items + 1,),
        in_specs=[pl.BlockSpec((1, tile, d), row),
                  pl.BlockSpec((1, tile, attn.shape[2]), row),
                  pl.BlockSpec((1, tile, 2 * d), row),
                  pl.BlockSpec((1, tile, d), row),
                  pl.BlockSpec((1, 1, N_MOD * d), lambda g: (2 * (cur(g) // nt), 0, 0)),
                  pl.BlockSpec((1, 1, N_MOD * d), lambda g: (2 * (prev(g) // nt), 0, 0)),
                  pl.BlockSpec((1, d), const),
                  pl.BlockSpec(wr.shape, const),
                  pl.BlockSpec(wm.shape, const),
                  pl.BlockSpec(wo.shape, const),
                  pl.BlockSpec(wrt.shape, const)],
        out_specs=[pl.BlockSpec((1, tile, d), row),
                   pl.BlockSpec((1, tile * (d // LANE), LANE), row_prev),
                   pl.BlockSpec((1, N_EXPERTS, tile), lambda g: (prev(g) // nt, 0, prev(g) % nt))],
        out_shape=(jax.ShapeDtypeStruct((b, s, d), F32),
                   jax.ShapeDtypeStruct((b, s * (d // LANE), LANE), F32),
                   jax.ShapeDtypeStruct((b, N_EXPERTS, s), F32)),
        scratch_shapes=[pltpu.VMEM((tile, d), F32), pltpu.VMEM((tile, d), F32)],
        compiler_params=_cparams(("arbitrary",)),
        name="merge",
    )(rnn, attn, sg, x, mod, mod, g2, wr, wm, wo, wrt)


def _excl_cumsum_lanes(x01, tri):
    rows, length = x01.shape
    offset = jnp.zeros((rows, 1), F32)
    pieces = []
    for blk in range(length // LANE):
        xb = x01[:, blk * LANE:(blk + 1) * LANE]
        inc = jnp.dot(xb.astype(BF16), tri, preferred_element_type=F32)
        pieces.append(inc - xb + offset)
        offset = offset + jnp.sum(xb, axis=-1, keepdims=True)
    return jnp.concatenate(pieces, axis=-1)


def _topk_kernel(aff_ref, idx_ref, gate_ref, pos_s, sel_s, *, cap):
    a = aff_ref[0]
    n_e, length = a.shape

    def as_float(bits):
        return lax.bitcast_convert_type(bits, F32)

    def bisect(i, cand):
        t = cand | (jnp.int32(1) << (30 - i))
        cnt = jnp.sum((a >= as_float(t)).astype(jnp.int32), axis=-1, keepdims=True)
        return jnp.where(cnt >= cap, t, cand)

    thr_bits = lax.fori_loop(0, 31, bisect, jnp.zeros((n_e, 1), jnp.int32))
    thr = as_float(thr_bits)
    above = as_float(thr_bits + 1)
    gt = (a >= above).astype(F32)
    eq = jnp.logical_and(a >= thr, a < above).astype(F32)
    need = cap - jnp.sum(gt, axis=-1, keepdims=True)
    r_i = lax.broadcasted_iota(jnp.int32, (LANE, LANE), 0)
    c_i = lax.broadcasted_iota(jnp.int32, (LANE, LANE), 1)
    tri = (r_i <= c_i).astype(BF16)
    tie_rank = _excl_cumsum_lanes(eq, tri)
    sel = gt + eq * (tie_rank < need).astype(F32)
    pos_s[...] = _excl_cumsum_lanes(sel, tri)
    sel_s[...] = sel

    n_hi = idx_ref.shape[2]
    tok = lax.broadcasted_iota(jnp.int32, (1, length), 1)
    tok_hi = (tok >> 6).astype(F32)
    tok_lo = (tok & 63).astype(F32)
    a_iota = lax.broadcasted_iota(jnp.int32, (n_hi, 1), 0)
    c_iota = lax.broadcasted_iota(jnp.int32, (SLOT_LO, 1), 0)

    def compact(e, carry):
        pos_row = pos_s[pl.ds(e, 1), :].astype(jnp.int32)
        sel_row = sel_s[pl.ds(e, 1), :] > 0.5
        hot_a = jnp.where(jnp.logical_and((pos_row // SLOT_LO) == a_iota, sel_row), 1.0, 0.0)
        hot_c = jnp.where((pos_row % SLOT_LO) == c_iota, 1.0, 0.0).astype(BF16)
        g = aff_ref[0, pl.ds(e, 1), :]
        g1 = g.astype(BF16).astype(F32)
        g2 = (g - g1).astype(BF16).astype(F32)
        g3 = g - g1 - g2
        lhs = jnp.concatenate([hot_a * v for v in (tok_hi, tok_lo, g1, g2, g3)], axis=0).astype(BF16)
        res = lax.dot_general(lhs, hot_c, (((1,), (1,)), ((), ())), preferred_element_type=F32)
        part = [res[v * n_hi:(v + 1) * n_hi] for v in range(5)]
        idx_ref[0, e] = (part[0] * 64.0 + part[1]).astype(jnp.int32)
        gate_ref[0, e] = part[2] + part[3] + part[4]
        return carry

    lax.fori_loop(0, n_e, compact, 0)


def _topk(aff_t, cap):
    b, n_e, length = aff_t.shape
    assert length <= 64 * 256, "token ids travel through the compaction matmul as two bf16-exact digits (t >> 6, t & 63)"
    group = TOPK_SAMPLES if b % TOPK_SAMPLES == 0 else 1
    aff_t = aff_t.reshape(b // group, group * n_e, length)
    b, n_e = b // group, group * n_e
    n_hi = max(cap // SLOT_LO, SUBLANE)
    blk = lambda bb: (bb, 0, 0)
    blk4 = lambda bb: (bb, 0, 0, 0)
    idx, gate = pl.pallas_call(
        functools.partial(_topk_kernel, cap=cap),
        grid=(b,),
        in_specs=[pl.BlockSpec((1, n_e, length), blk)],
        out_specs=[pl.BlockSpec((1, n_e, n_hi, SLOT_LO), blk4), pl.BlockSpec((1, n_e, n_hi, SLOT_LO), blk4)],
        out_shape=(jax.ShapeDtypeStruct((b, n_e, n_hi, SLOT_LO), jnp.int32),
                   jax.ShapeDtypeStruct((b, n_e, n_hi, SLOT_LO), F32)),
        scratch_shapes=[pltpu.VMEM((n_e, length), F32), pltpu.VMEM((n_e, length), F32)],
        compiler_params=_cparams(("arbitrary",)),
        name="topk",
    )(aff_t)
    flat = lambda a: a.reshape(b * n_e, 1, n_hi * SLOT_LO)[:, :, :cap]
    return flat(idx), flat(gate)


def _gather_kernel(idx_ref, h_ref, o_ref, rows_s, *, cap, d):
    tiles = d // LANE

    def body(j, carry):
        src = pl.multiple_of(idx_ref[0, 0, j] * tiles, tiles)
        dst = pl.multiple_of(j * tiles, tiles)
        rows_s[pl.ds(dst, tiles), :] = h_ref[0, pl.ds(src, tiles), :]
        return carry

    lax.fori_loop(0, cap, body, 0, unroll=8)
    o_ref[0, 0] = jnp.concatenate([rows_s[pl.ds(t, cap, stride=tiles), :] for t in range(tiles)],
                                  axis=-1).astype(BF16)


def _gather(idx, h2t, cap, d):
    b, rows, lanes = h2t.shape
    assert d // lanes == SUBLANE and lanes == LANE
    return pl.pallas_call(
        functools.partial(_gather_kernel, cap=cap, d=d),
        grid=(b, N_EXPERTS),
        in_specs=[pl.BlockSpec((1, 1, cap), lambda bb, e: (bb * N_EXPERTS + e, 0, 0), memory_space=pltpu.SMEM),
                  pl.BlockSpec((1, rows, lanes), lambda bb, e: (bb, 0, 0))],
        out_specs=pl.BlockSpec((1, 1, cap, d), lambda bb, e: (bb, e, 0, 0)),
        out_shape=jax.ShapeDtypeStruct((b, N_EXPERTS, cap, d), BF16),
        scratch_shapes=[pltpu.VMEM((cap * (d // LANE), LANE), F32)],
        compiler_params=_cparams(("arbitrary", "arbitrary")),
        name="gather",
    )(idx, h2t)


def _moe_kernel(idx_ref, gate_ref, x_ref, wg_ref, wu_ref, wd_ref, mod_ref, res_ref, o_ref, y_even, y_odd, *, d, cap):
    g = pl.program_id(0)
    e_prev = jnp.maximum(g - 1, 0) % N_EXPERTS
    piece = res_ref.shape[1]

    @pl.when(g == 0)
    def _():
        y_odd[...] = jnp.zeros_like(y_odd)

    @pl.when(e_prev == 0)
    def _():
        o_ref[...] = jnp.zeros_like(o_ref)

    def step(y_new, y_prev):
        xb = x_ref[0, 0]
        gg = jnp.dot(xb, wg_ref[0], preferred_element_type=F32)
        uu = jnp.dot(xb, wu_ref[0], preferred_element_type=F32)
        hid = (gg * jax.nn.sigmoid(gg) * uu).astype(BF16)
        y_new[...] = jnp.dot(hid, wd_ref[0], preferred_element_type=F32) * mod_ref[0, :, 5 * d:6 * d]

        rows = pl.ds(pl.multiple_of(e_prev * piece, piece), piece)
        o_ref[0, rows, :] = o_ref[0, rows, :] + res_ref[0]
        for j in range(cap):
            r = idx_ref[0, 0, j]
            o_ref[0, pl.ds(r, 1), :] = o_ref[0, pl.ds(r, 1), :] + gate_ref[0, 0, j] * y_prev[j:j + 1, :]

    @pl.when(g % 2 == 0)
    def _():
        step(y_even, y_odd)

    @pl.when(g % 2 == 1)
    def _():
        step(y_odd, y_even)


def _moe(idx, gate, xg, w_gate, w_up, w_down, mod, x_new):
    b, n_e, cap, d = xg.shape
    s = x_new.shape[1]
    f = w_gate.shape[2]
    n_items = b * n_e
    cur = lambda g: jnp.minimum(g, n_items - 1)
    prev = lambda g: jnp.maximum(g - 1, 0)
    smem = lambda: pl.BlockSpec((1, 1, cap), lambda g: (prev(g), 0, 0), memory_space=pltpu.SMEM)
    wspec = lambda shape: pl.BlockSpec(shape, lambda g: (cur(g) % n_e, 0, 0))
    return pl.pallas_call(
        functools.partial(_moe_kernel, d=d, cap=cap),
        grid=(n_items + 1,),
        in_specs=[smem(), smem(),
                  pl.BlockSpec((1, 1, cap, d), lambda g: (cur(g) // n_e, cur(g) % n_e, 0, 0)),
                  wspec((1, d, f)), wspec((1, d, f)), wspec((1, f, d)),
                  pl.BlockSpec((1, 1, N_MOD * d), lambda g: (2 * (cur(g) // n_e), 0, 0)),
                  pl.BlockSpec((1, s // n_e, d), lambda g: (prev(g) // n_e, prev(g) % n_e, 0))],
        out_specs=pl.BlockSpec((1, s, d), lambda g: (prev(g) // n_e, 0, 0), pipeline_mode=pl.Buffered(1)),
        out_shape=jax.ShapeDtypeStruct((b, s, d), F32),
        scratch_shapes=[pltpu.VMEM((cap, d), F32), pltpu.VMEM((cap, d), F32)],
        compiler_params=_cparams(("arbitrary",)),
        name="moe",
    )(idx, gate, xg, w_gate, w_up, w_down, mod, x_new)


def _pad_lanes(a, width):
    return jnp.pad(a, [(0, 0)] * (a.ndim - 1) + [(0, width - a.shape[-1])])


def _rope_tables(s, lc, g_q_nope, g_q_rope, g_k_nope, g_k_rope):
    ltot = s + lc
    n_pairs = QK_ROPE // 4
    lane = jnp.arange(LANE)
    in_rope = jnp.logical_and(lane >= QK_NOPE, lane < QK_NOPE + QK_ROPE)
    angle_id = jnp.where(in_rope, (lane - QK_NOPE) % HALF_ROPE, 0)
    inv_freq = ROPE_BASE ** (-jnp.arange(n_pairs, dtype=F32) / n_pairs)
    freq = jnp.where(in_rope, inv_freq[angle_id % n_pairs], 0.0)
    t = jnp.arange(ltot)
    pos = jnp.where((angle_id >= n_pairs)[None, :], (t % GRID_W)[:, None], (t // GRID_W)[:, None]).astype(F32)
    ang = jnp.where((t < s)[:, None], pos * freq[None, :], 0.0)
    cos, sin = jnp.cos(ang), jnp.sin(ang)
    z_nope = jnp.zeros((QK_NOPE,), F32)
    z_tail = jnp.zeros((LANE - QK_NOPE - QK_ROPE,), F32)

    def tables(g_nope, g_rope, scale):
        g_cos = jnp.concatenate([g_nope, g_rope, z_tail]) * scale
        g_sin = jnp.concatenate([z_nope, -g_rope[HALF_ROPE:], g_rope[:HALF_ROPE], z_tail]) * scale
        return cos * g_cos[None, :], sin * g_sin[None, :]

    tqc, tqs = tables(g_q_nope, g_q_rope, ATTN_SCALE * LOG2_E)
    tkc, tks = tables(z_nope, g_k_rope, 1.0)
    gkn = _pad_lanes(g_k_nope[None, :], LANE)
    return tqc, tqs, tkc, tks, gkn


def _swap_halves(a):
    return jnp.concatenate([a[..., HALF_ROPE:], a[..., :HALF_ROPE]], axis=-1)


def _layout_weights(w_in, w_uq, w_uk, w_uv, d):
    o_q = 2 * d
    o_kv = o_q + Q_LORA
    o_kr = o_kv + KV_LORA
    o_g = o_kr + QK_ROPE
    kr = w_in[:, o_kr:o_g]
    zl = jnp.zeros((w_in.shape[0], QK_NOPE), F32)
    kr_a = _pad_lanes(jnp.concatenate([zl, kr], axis=-1), LANE)
    kr_s = _pad_lanes(jnp.concatenate([zl, _swap_halves(kr)], axis=-1), LANE)
    win_p = jnp.concatenate([w_in[:, :o_kr], kr_a, kr_s, w_in[:, o_g:]], axis=-1).astype(BF16)

    uq = w_uq.reshape(Q_LORA, MLA_HEADS, QK_NOPE + QK_ROPE)
    nope, rope = uq[..., :QK_NOPE], uq[..., QK_NOPE:]
    wuq_p = _pad_lanes(jnp.concatenate([nope, rope], axis=-1), HEAD_PAD).reshape(Q_LORA, -1).astype(BF16)
    wuq_s = _pad_lanes(jnp.concatenate([jnp.zeros_like(nope), _swap_halves(rope)], axis=-1),
                       HEAD_PAD).reshape(Q_LORA, -1).astype(BF16)
    wuk_p = _pad_lanes(w_uk.reshape(KV_LORA, MLA_HEADS, QK_NOPE), HEAD_PAD).reshape(KV_LORA, -1).astype(BF16)
    wuv_p = _pad_lanes(w_uv.reshape(KV_LORA, MLA_HEADS, V_HEAD), HEAD_PAD).reshape(KV_LORA, -1).astype(BF16)
    return win_p, wuq_p, wuq_s, wuk_p, wuv_p


def _pair_blocks(w):
    *lead, nb, n, _ = w.shape
    w = w.reshape(*lead, nb // 2, 2, n, n)
    paired = jnp.einsum('...irc,ij->...irjc', w, jnp.eye(2, dtype=w.dtype))
    return paired.reshape(*lead, nb // 2, 2 * n, 2 * n).astype(BF16)


def kernel(x, c, ctx, c_ctx, w_ada, b_ada, g_norm1, g_norm2, w_in, conv_w, conv_b, rg_wa, rg_ba, rg_wx, rg_bx,
           rg_lambda, w_rnn_out, g_q_lora, w_uq, g_kv_lora, w_uk, w_uv, g_q_nope, g_q_rope, g_k_nope, g_k_rope,
           w_mla_out, w_o, w_router, w_e_gate, w_e_up, w_e_down):
    assert w_ada.shape[0] == 1, "single-layer kernel"
    b, s, d = x.shape
    lc = ctx.shape[1]
    tile = lc
    n_lat = s // tile
    cap = CAPACITY_FACTOR * s // N_EXPERTS

    pad_rows = (-(b + 1)) % SUBLANE
    cc = jnp.concatenate([c, c_ctx[None, :], jnp.zeros((pad_rows, d), F32)], axis=0)
    mod_all = _ada(cc, w_ada[0], b_ada[0])
    mod = jnp.stack([mod_all[:b], jnp.broadcast_to(mod_all[b], (b, N_MOD * d))], axis=1).reshape(2 * b, 1, N_MOD * d)

    win_p, wuq_p, wuq_s, wuk_p, wuv_p = _layout_weights(w_in[0], w_uq[0], w_uk[0], w_uv[0], d)
    tqc, tqs, tkc, tks, gkn = _rope_tables(s, lc, g_q_nope[0], g_q_rope[0], g_k_nope[0], g_k_rope[0])
    z0, gz, q, k, v, sg = _inproj(x, ctx, mod, g_norm1, win_p, g_q_lora, wuq_p, wuq_s, g_kv_lora, wuk_p,
                                  wuv_p, gkn, tqc, tqs, tkc, tks, tile)

    gate_w = _pair_blocks(jnp.stack([rg_wa[0], rg_wx[0]]))
    h_f, xr = _rnn(z0, gate_w[0, 0], gate_w[1, 0], rg_ba[0, 0:1], rg_bx[0, 0:1],
                   rg_lambda[0, 0:1], tile, n_lat, False, conv=(conv_w[0], conv_b))
    rnn = _rnn(xr, gate_w[0, 1], gate_w[1, 1], rg_ba[0, 1:2], rg_bx[0, 1:2],
               rg_lambda[0, 1:2], tile, n_lat, True, hf=h_f, gz=gz)

    attn, (wg_b, wu_b, wd_b) = _attention(q, k, v, s, 2 * MXU_COLS, (w_e_gate[0], w_e_up[0], w_e_down[0]))

    x_new, h2, aff = _merge(rnn, attn, sg, x, mod, g_norm2, w_rnn_out[0].astype(BF16), w_mla_out[0].astype(BF16),
                            w_o[0].astype(BF16), _pad_lanes(w_router[0], LANE).astype(BF16), 2 * tile)

    idx, gate = _topk(aff, cap)
    xg = _gather(idx, h2, cap, d)
    return _moe(idx, gate, xg, wg_b, wu_b, wd_b, mod, x_new)
```

```python
import functools

import jax
import jax.numpy as jnp
from jax import lax
from jax.experimental import pallas as pl
from jax.experimental.pallas import tpu as pltpu

EPS = 1e-6
N_MOD = 6
GRID_W = 64
RG_BLOCKS = 8
CONV_W = 4
CONV_PAD_L = 2
RG_C = 8.0
MLA_HEADS = 8
QK_NOPE = 64
QK_ROPE = 32
V_HEAD = 64
Q_LORA = 256
KV_LORA = 128
ROPE_BASE = 10000.0
ATTN_SCALE = (QK_NOPE + QK_ROPE) ** -0.5
LOG2_E = 1.4426950408889634
N_EXPERTS = 16
CAPACITY_FACTOR = 2

LANE = 128
SUBLANE = 8
MXU_COLS = 256
HEAD_PAD = LANE
HALF_ROPE = QK_ROPE // 2
RNN_SUB = 64
RNN_SAMPLES = 4
TOPK_SAMPLES = 4
SLOT_LO = 32
VMEM_LIMIT = 56 * 1024 * 1024

BF16 = jnp.bfloat16
F32 = jnp.float32


def _cparams(sem):
    return pltpu.CompilerParams(dimension_semantics=sem, vmem_limit_bytes=VMEM_LIMIT)


def _ada_kernel(c_ref, w_ref, b_ref, o_ref):
    cc = c_ref[...]
    s = cc * jax.nn.sigmoid(cc)
    o_ref[...] = jnp.dot(s, w_ref[...], preferred_element_type=F32) + b_ref[...]


def _ada(cc, w_ada, b_ada):
    rows, d = cc.shape
    n = w_ada.shape[1]
    bn = d
    return pl.pallas_call(
        _ada_kernel,
        grid=(n // bn,),
        in_specs=[pl.BlockSpec((rows, d), lambda j: (0, 0)),
                  pl.BlockSpec((d, bn), lambda j: (0, j)),
                  pl.BlockSpec((1, bn), lambda j: (0, j))],
        out_specs=pl.BlockSpec((rows, bn), lambda j: (0, j)),
        out_shape=jax.ShapeDtypeStruct((rows, n), F32),
        compiler_params=_cparams(("arbitrary",)),
        name="ada",
    )(cc, w_ada, b_ada.reshape(1, n))


def _inproj_kernel(x_ref, ctx_ref, mod_ref, g1_ref, win_ref, gq_ref, wuq_ref, wuqs_ref, gkv_ref, wuk_ref,
                   wuv_ref, gkn_ref, tqc_ref, tqs_ref, tkc_ref, tks_ref,
                   z0_ref, gz_ref, q_ref, k_ref, v_ref, sg_ref, *, d, n_lat_tiles):
    i = pl.program_id(1)
    is_ctx = i == n_lat_tiles
    xt = jnp.where(is_ctx, ctx_ref[0], x_ref[0])
    shift = mod_ref[0, :, 0:d]
    scale = mod_ref[0, :, d:2 * d]
    inv = lax.rsqrt(jnp.mean(xt * xt, axis=-1, keepdims=True) + EPS)
    xn = ((xt * inv * g1_ref[...]) * (1.0 + scale) + shift).astype(BF16)

    def seg(lo, hi):
        return jnp.dot(xn, win_ref[:, lo:hi], preferred_element_type=F32)

    o_q = 2 * d
    o_kv = o_q + Q_LORA
    o_kr = o_kv + KV_LORA
    o_g = o_kr + 2 * LANE
    zq = seg(o_q, o_kv)
    zkv = seg(o_kv, o_kr)
    kr_a = seg(o_kr, o_kr + LANE)
    kr_s = seg(o_kr + LANE, o_g)

    lane = lax.broadcasted_iota(jnp.int32, (1, LANE), 1)
    is_nope = lane < QK_NOPE
    is_rope = jnp.logical_and(lane >= QK_NOPE, lane < QK_NOPE + QK_ROPE)

    zqn = (zq * lax.rsqrt(jnp.mean(zq * zq, axis=-1, keepdims=True) + EPS) * gq_ref[...]).astype(BF16)
    qa = jnp.dot(zqn, wuq_ref[...], preferred_element_type=F32)
    qs = jnp.dot(zqn, wuqs_ref[...], preferred_element_type=F32)
    tqc = tqc_ref[...]
    tqs = tqs_ref[...]
    for h in range(MLA_HEADS):
        a = qa[:, h * HEAD_PAD:(h + 1) * HEAD_PAD]
        s = qs[:, h * HEAD_PAD:(h + 1) * HEAD_PAD]
        sq = a * a
        ssn = jnp.sum(jnp.where(is_nope, sq, 0.0), axis=-1, keepdims=True)
        ssr = jnp.sum(jnp.where(is_rope, sq, 0.0), axis=-1, keepdims=True)
        invh = jnp.where(is_nope, lax.rsqrt(ssn * (1.0 / QK_NOPE) + EPS), lax.rsqrt(ssr * (1.0 / QK_ROPE) + EPS))
        q_ref[0, :, h * HEAD_PAD:(h + 1) * HEAD_PAD] = (invh * (a * tqc + s * tqs)).astype(BF16)

    ckv = (zkv * lax.rsqrt(jnp.mean(zkv * zkv, axis=-1, keepdims=True) + EPS) * gkv_ref[...]).astype(BF16)
    lane_h = lax.broadcasted_iota(jnp.int32, (1, MLA_HEADS * HEAD_PAD), 1) % HEAD_PAD
    vp = jnp.dot(ckv, wuv_ref[...], preferred_element_type=F32) + jnp.where(lane_h == V_HEAD, 1.0, 0.0)
    v_ref[0] = vp.T.astype(BF16)
    ka = jnp.dot(ckv, wuk_ref[...], preferred_element_type=F32)
    inv_r = lax.rsqrt(jnp.sum(kr_a * kr_a, axis=-1, keepdims=True) * (1.0 / QK_ROPE) + EPS)
    kr = inv_r * (kr_a * tkc_ref[...] + kr_s * tks_ref[...])
    gkn = gkn_ref[...]
    for h in range(MLA_HEADS):
        a = ka[:, h * HEAD_PAD:(h + 1) * HEAD_PAD]
        invh = lax.rsqrt(jnp.sum(a * a, axis=-1, keepdims=True) * (1.0 / QK_NOPE) + EPS)
        k_ref[0, :, h * HEAD_PAD:(h + 1) * HEAD_PAD] = (a * invh * gkn + kr).astype(BF16)

    z0_ref[0] = seg(0, d)
    gz_ref[0] = jax.nn.gelu(seg(d, o_q)).astype(BF16)
    sg_ref[0] = jax.nn.sigmoid(seg(o_g, o_g + 2 * d)).astype(BF16)


def _inproj(x, ctx, mod, g1, win_p, gq, wuq_p, wuq_s, gkv, wuk_p, wuv, gkn, tqc, tqs, tkc, tks, tile):
    b, s, d = x.shape
    lc = ctx.shape[1]
    assert lc == tile and s % tile == 0
    nl = s // tile
    ltot = s + lc
    n_in = win_p.shape[1]
    hp = MLA_HEADS * HEAD_PAD
    const = lambda bb, i: (0, 0)
    row = lambda bb, i: (bb, i, 0)
    tab = lambda bb, i: (i, 0)
    out_shapes = (
        jax.ShapeDtypeStruct((b, ltot, d), F32),
        jax.ShapeDtypeStruct((b, ltot, d), BF16),
        jax.ShapeDtypeStruct((b, ltot, hp), BF16),
        jax.ShapeDtypeStruct((b, ltot, hp), BF16),
        jax.ShapeDtypeStruct((b, hp, ltot), BF16),
        jax.ShapeDtypeStruct((b, ltot, 2 * d), BF16),
    )
    return pl.pallas_call(
        functools.partial(_inproj_kernel, d=d, n_lat_tiles=nl),
        grid=(b, nl + 1),
        in_specs=[
            pl.BlockSpec((1, tile, d), lambda bb, i: (bb, jnp.minimum(i, nl - 1), 0)),
            pl.BlockSpec((1, tile, d), lambda bb, i: (bb, 0, 0)),
            pl.BlockSpec((1, 1, N_MOD * d), lambda bb, i: (2 * bb + i // nl, 0, 0)),
            pl.BlockSpec((1, d), const),
            pl.BlockSpec((d, n_in), const),
            pl.BlockSpec((1, Q_LORA), const),
            pl.BlockSpec((Q_LORA, hp), const),
            pl.BlockSpec((Q_LORA, hp), const),
            pl.BlockSpec((1, KV_LORA), const),
            pl.BlockSpec((KV_LORA, hp), const),
            pl.BlockSpec((KV_LORA, hp), const),
            pl.BlockSpec((1, LANE), const),
            pl.BlockSpec((tile, LANE), tab),
            pl.BlockSpec((tile, LANE), tab),
            pl.BlockSpec((tile, LANE), tab),
            pl.BlockSpec((tile, LANE), tab),
        ],
        out_specs=[
            pl.BlockSpec((1, tile, d), row),
            pl.BlockSpec((1, tile, d), row),
            pl.BlockSpec((1, tile, hp), row),
            pl.BlockSpec((1, tile, hp), row),
            pl.BlockSpec((1, hp, tile), lambda bb, i: (bb, 0, i)),
            pl.BlockSpec((1, tile, 2 * d), row),
        ],
        out_shape=out_shapes,
        compiler_params=_cparams(("arbitrary", "arbitrary")),
        name="inproj",
    )(x, ctx, mod, g1, win_p, gq, wuq_p, wuq_s, gkv, wuk_p, wuv, gkn, tqc, tqs, tkc, tks)


def _sigmoid(x):
    return 0.5 * jnp.tanh(0.5 * x) + 0.5


def _softplus(x):
    e = jnp.exp(-jnp.abs(x))
    u = 1.0 + e
    tiny = u == 1.0
    log1p_e = jnp.where(tiny, e, jnp.log(u) * (e / jnp.where(tiny, 1.0, u - 1.0)))
    return jnp.maximum(x, 0.0) + log1p_e


def _rnn_kernel(*refs, d, tile, n_lat, reverse):
    if reverse:
        (xr_ref, wa_ref, wx_ref, ba_ref, bx_ref, lam_ref, hf_ref, gz_ref, o_ref, a_s, b_s, hb_s, h_s) = refs
    else:
        (z_ref, zp_ref, zn_ref, cw_ref, cb_ref, wa_ref, wx_ref, ba_ref, bx_ref, lam_ref,
         o_ref, xr_ref, a_s, b_s, h_s) = refs
    j = pl.program_id(1)
    nb = a_s.shape[0]
    if not reverse:
        c = jnp.where(j == 0, n_lat, j - 1)
        prev_zero = jnp.logical_or(c == 0, c == n_lat)
        next_zero = c >= n_lat - 1
        n_ext = tile + 2 * SUBLANE
        taps = []
        for bi in range(nb):
            ext = jnp.concatenate([jnp.where(prev_zero, 0.0, zp_ref[bi]), z_ref[bi],
                                   jnp.where(next_zero, 0.0, zn_ref[bi])], axis=0)
            taps.append([ext if (CONV_PAD_L - k) % n_ext == 0 else pltpu.roll(ext, (CONV_PAD_L - k) % n_ext, 0)
                         for k in range(CONV_W)])
    pair = 2 * (d // RG_BLOCKS)
    neg_c_softplus = -RG_C * _softplus(-lam_ref[...])

    hs = [jnp.where(j == 0, 0.0, h_s[bi]) for bi in range(nb)]
    subs = range(tile // RNN_SUB)
    for sb in (reversed(subs) if reverse else subs):
        r0 = sb * RNN_SUB
        for bi in range(nb):
            dst = hb_s.at[bi] if reverse else o_ref.at[bi]
            if reverse:
                xr = xr_ref[bi, r0:r0 + RNN_SUB, :]
            else:
                xr = cb_ref[...]
                for k in range(CONV_W):
                    xr = xr + taps[bi][k][SUBLANE + r0:SUBLANE + r0 + RNN_SUB] * cw_ref[k:k + 1, :]
                xr_ref[bi, r0:r0 + RNN_SUB, :] = xr
            xb = xr.astype(BF16)
            ya = jnp.concatenate([jnp.dot(xb[:, m * pair:(m + 1) * pair], wa_ref[m], preferred_element_type=F32)
                                  for m in range(d // pair)], axis=-1)
            yx = jnp.concatenate([jnp.dot(xb[:, m * pair:(m + 1) * pair], wx_ref[m], preferred_element_type=F32)
                                  for m in range(d // pair)], axis=-1)
            r = _sigmoid(ya + ba_ref[...])
            gi = _sigmoid(yx + bx_ref[...])
            log_a = neg_c_softplus * r
            a_s[bi, r0:r0 + RNN_SUB, :] = jnp.exp(log_a)
            th = jnp.tanh(log_a)
            v = -2.0 * th / (1.0 - th)
            b_s[bi, r0:r0 + RNN_SUB, :] = (v * lax.rsqrt(jnp.maximum(v, 1e-30))) * (gi * xr)
            rows = range(r0, r0 + RNN_SUB)
            h = hs[bi]
            for t in (reversed(rows) if reverse else rows):
                h = a_s[bi, t:t + 1, :] * h + b_s[bi, t:t + 1, :]
                dst[t:t + 1, :] = h
            hs[bi] = h
    for bi in range(nb):
        h_s[bi] = hs[bi]
    if reverse:
        o_ref[...] = ((hf_ref[...] + hb_s[...]) * gz_ref[...]).astype(BF16)


def _rnn(z, wa_p, wx_p, ba, bx, lam, tile, n_lat, reverse, conv=None, hf=None, gz=None):
    b, ltot, d = z.shape
    nb = RNN_SAMPLES if b % RNN_SAMPLES == 0 else 1
    n_halo = ltot // SUBLANE
    per = tile // SUBLANE
    if reverse:
        cidx = lambda j: jnp.where(j == 0, n_lat, n_lat - j)
        oidx = lambda j: jnp.where(j == 0, n_lat - 1, n_lat - j)
    else:
        cidx = lambda j: jnp.where(j == 0, n_lat, j - 1)
        oidx = lambda j: jnp.maximum(j - 1, 0)
    const2 = lambda bb, j: (0, 0)
    const3 = lambda bb, j: (0, 0, 0)
    chunk = pl.BlockSpec((nb, tile, d), lambda bb, j: (bb, cidx(j), 0))
    out_chunk = pl.BlockSpec((nb, tile, d), lambda bb, j: (bb, oidx(j), 0))
    gate_specs = [pl.BlockSpec(wa_p.shape, const3), pl.BlockSpec(wx_p.shape, const3),
                  pl.BlockSpec((1, d), const2), pl.BlockSpec((1, d), const2), pl.BlockSpec((1, d), const2)]
    gate_args = [wa_p, wx_p, ba, bx, lam]
    scratch = [pltpu.VMEM((nb, tile, d), F32), pltpu.VMEM((nb, tile, d), F32)]
    if reverse:
        in_specs = [chunk] + gate_specs + [out_chunk, chunk]
        args = [z] + gate_args + [hf, gz]
        out_specs = out_chunk
        out_shape = jax.ShapeDtypeStruct((b, n_lat * tile, d), BF16)
        scratch += [pltpu.VMEM((nb, tile, d), F32)]
    else:
        in_specs = [chunk,
                    pl.BlockSpec((nb, SUBLANE, d), lambda bb, j: (bb, jnp.maximum(cidx(j) * per - 1, 0), 0)),
                    pl.BlockSpec((nb, SUBLANE, d), lambda bb, j: (bb, jnp.minimum((cidx(j) + 1) * per, n_halo - 1), 0)),
                    pl.BlockSpec((CONV_W, d), const2), pl.BlockSpec((1, d), const2)] + gate_specs
        args = [z, z, z, conv[0], conv[1]] + gate_args
        out_specs = [out_chunk, chunk]
        out_shape = (jax.ShapeDtypeStruct((b, n_lat * tile, d), F32), jax.ShapeDtypeStruct((b, ltot, d), F32))
    scratch += [pltpu.VMEM((nb, 1, d), F32)]
    return pl.pallas_call(
        functools.partial(_rnn_kernel, d=d, tile=tile, n_lat=n_lat, reverse=reverse),
        grid=(b // nb, n_lat + 1),
        in_specs=in_specs,
        out_specs=out_specs,
        out_shape=out_shape,
        scratch_shapes=scratch,
        compiler_params=_cparams(("arbitrary", "arbitrary")),
        name="rnn_bwd" if reverse else "rnn_fwd",
    )(*args)


def _attn_kernel(q_ref, k_ref, vt_ref, *rest, n_cast):
    w_refs = rest[:n_cast]
    o_ref = rest[n_cast]
    wb_refs = rest[n_cast + 1:2 * n_cast + 1]
    s_even, s_odd, m_even, m_odd = rest[2 * n_cast + 1:]
    g = pl.program_id(0)

    @pl.when(g == 0)
    def _():
        s_odd[...] = jnp.zeros_like(s_odd)
        m_odd[...] = jnp.zeros_like(m_odd)

    n_keys = k_ref.shape[1]
    tq = q_ref.shape[1]

    def step(s_new, m_new, s_prev, m_prev):
        heads = range(2)
        lanes = [slice(hh * HEAD_PAD, (hh + 1) * HEAD_PAD) for hh in heads]
        qs = [q_ref[0, :, lanes[hh]] for hh in heads]
        m_run = [None, None]
        acc = [jnp.zeros((HEAD_PAD, tq), F32) for _ in heads]
        for c in range(n_keys // MXU_COLS):
            rows = slice(c * MXU_COLS, (c + 1) * MXU_COLS)
            for hh in heads:
                st = lax.dot_general(k_ref[0, rows, lanes[hh]], qs[hh], (((1,), (1,)), ((), ())),
                                     preferred_element_type=F32)
                s_new[hh, rows, :] = st
                cm = jnp.max(st, axis=0, keepdims=True)
                m_run[hh] = cm if c == 0 else jnp.maximum(m_run[hh], cm)
                pt = jnp.exp2(s_prev[hh, rows, :] - m_prev[hh]).astype(BF16)
                acc[hh] = acc[hh] + jnp.dot(vt_ref[0, lanes[hh], rows], pt, preferred_element_type=F32)
        for hh in heads:
            m_new[hh] = m_run[hh]
        outs = [acc[hh][0:V_HEAD] / acc[hh][V_HEAD:V_HEAD + 1] for hh in heads]
        o_ref[0] = jnp.concatenate(outs, axis=0).T.astype(BF16)
        for w_ref, wb_ref in zip(w_refs, wb_refs):
            wb_ref[...] = w_ref[...].astype(BF16)

    @pl.when(g % 2 == 0)
    def _():
        step(s_even, m_even, s_odd, m_odd)

    @pl.when(g % 2 == 1)
    def _():
        step(s_odd, m_odd, s_even, m_even)


def _attention(q, k, vt, s, tq, to_bf16):
    b, ltot, _ = q.shape
    pairs = MLA_HEADS // 2
    nq = s // tq
    n_items = b * pairs * nq
    flat = [w.reshape(-1, w.shape[-1]) for w in to_bf16]
    slabs = [w.shape[0] // n_items for w in flat]
    assert all(w.shape[0] == sl * n_items and sl % (2 * SUBLANE) == 0 for w, sl in zip(flat, slabs))
    slab_specs = [pl.BlockSpec((sl, w.shape[1]), lambda g: (jnp.minimum(g, n_items - 1), 0))
                  for w, sl in zip(flat, slabs)]

    def item(g):
        return g // (pairs * nq), (g // nq) % pairs, g % nq

    def cur(g):
        return item(jnp.minimum(g, n_items - 1))

    def prev(g):
        return item(jnp.maximum(g - 1, 0))

    scores = pltpu.VMEM((2, ltot, tq), F32)
    colmax = pltpu.VMEM((2, 1, tq), F32)
    out = pl.pallas_call(
        functools.partial(_attn_kernel, n_cast=len(flat)),
        grid=(n_items + 1,),
        in_specs=[pl.BlockSpec((1, tq, 2 * HEAD_PAD), lambda g: (cur(g)[0], cur(g)[2], cur(g)[1])),
                  pl.BlockSpec((1, ltot, 2 * HEAD_PAD), lambda g: (cur(g)[0], 0, cur(g)[1])),
                  pl.BlockSpec((1, 2 * HEAD_PAD, ltot), lambda g: (prev(g)[0], prev(g)[1], 0))] + slab_specs,
        out_specs=[pl.BlockSpec((1, tq, 2 * V_HEAD), lambda g: (prev(g)[0], prev(g)[2], prev(g)[1]))] + slab_specs,
        out_shape=[jax.ShapeDtypeStruct((b, s, MLA_HEADS * V_HEAD), BF16)]
        + [jax.ShapeDtypeStruct(w.shape, BF16) for w in flat],
        scratch_shapes=[scores, scores, colmax, colmax],
        compiler_params=_cparams(("arbitrary",)),
        name="attn",
    )(q, k, vt, *flat)
    return out[0], [wb.reshape(w.shape) for wb, w in zip(out[1:], to_bf16)]


def _merge_kernel(rnn_ref, attn_ref, sg_ref, x_hbm, mod_ref, modp_ref, g2_ref, wr_ref, wm_ref, wo_ref, wrt_ref,
                  xn_ref, h2_ref, aff_ref, xn_even, xn_odd, xbuf, sem, *, d, n_items, nt):
    g = pl.program_id(0)
    tile = xbuf.shape[1]

    def x_copy(s):
        it = jnp.minimum(s, n_items - 1)
        rows = pl.ds(pl.multiple_of((it % nt) * tile, tile), tile)
        return pltpu.make_async_copy(x_hbm.at[it // nt, rows, :], xbuf.at[s % 3], sem.at[s % 3])

    @pl.when(g == 0)
    def _():
        xn_odd[...] = jnp.zeros_like(xn_odd)
        x_copy(0).start()
        x_copy(1).start()

    @pl.when(g + 2 <= n_items)
    def _():
        x_copy(g + 2).start()

    x_copy(g).wait()
    x_ref = xbuf.at[g % 3]

    def step(xn_new, xn_prev):
        pr = jnp.dot(rnn_ref[0], wr_ref[...], preferred_element_type=F32)
        pa = jnp.dot(attn_ref[0], wm_ref[...], preferred_element_type=F32)
        merged = (sg_ref[0, :, 0:d] * pr + sg_ref[0, :, d:2 * d] * pa).astype(BF16)

        xp = xn_prev[...]
        inv = lax.rsqrt(jnp.mean(xp * xp, axis=-1, keepdims=True) + EPS)
        h2 = (xp * inv * g2_ref[...]) * (1.0 + modp_ref[0, :, 4 * d:5 * d]) + modp_ref[0, :, 3 * d:4 * d]
        n_tok = h2.shape[0]
        for t in range(d // LANE):
            h2_ref[0, pl.ds(t, n_tok, stride=d // LANE), :] = h2[:, t * LANE:(t + 1) * LANE]
        logits = jnp.dot(h2.astype(BF16), wrt_ref[...], preferred_element_type=F32)
        lane = lax.broadcasted_iota(jnp.int32, (1, LANE), 1)
        logits = jnp.where(lane < N_EXPERTS, logits, -jnp.inf)
        e = jnp.exp(logits - jnp.max(logits, axis=-1, keepdims=True))
        aff_ref[0] = (e / jnp.sum(e, axis=-1, keepdims=True)).T[0:N_EXPERTS]

        xn = x_ref[...] + mod_ref[0, :, 2 * d:3 * d] * jnp.dot(merged, wo_ref[...], preferred_element_type=F32)
        xn_ref[0] = xn
        xn_new[...] = xn

    @pl.when(g % 2 == 0)
    def _():
        step(xn_even, xn_odd)

    @pl.when(g % 2 == 1)
    def _():
        step(xn_odd, xn_even)


def _merge(rnn, attn, sg, x, mod, g2, wr, wm, wo, wrt, tile):
    b, s, d = x.shape
    nt = s // tile
    n_items = b * nt
    const = lambda g: (0, 0)
    cur = lambda g: jnp.minimum(g, n_items - 1)
    prev = lambda g: jnp.maximum(g - 1, 0)
    row = lambda g: (cur(g) // nt, cur(g) % nt, 0)
    row_prev = lambda g: (prev(g) // nt, prev(g) % nt, 0)
    return pl.pallas_call(
        functools.partial(_merge_kernel, d=d, n_items=n_items, nt=nt),
        grid=(n_items + 1,),
        in_specs=[pl.BlockSpec((1, tile, d), row),
                  pl.BlockSpec((1, tile, attn.shape[2]), row),
                  pl.BlockSpec((1, tile, 2 * d), row),
                  pl.BlockSpec(memory_space=pl.ANY),
                  pl.BlockSpec((1, 1, N_MOD * d), lambda g: (2 * (cur(g) // nt), 0, 0)),
                  pl.BlockSpec((1, 1, N_MOD * d), lambda g: (2 * (prev(g) // nt), 0, 0)),
                  pl.BlockSpec((1, d), const),
                  pl.BlockSpec(wr.shape, const),
                  pl.BlockSpec(wm.shape, const),
                  pl.BlockSpec(wo.shape, const),
                  pl.BlockSpec(wrt.shape, const)],
        out_specs=[pl.BlockSpec((1, tile, d), row),
                   pl.BlockSpec((1, tile * (d // LANE), LANE), row_prev),
                   pl.BlockSpec((1, N_EXPERTS, tile), lambda g: (prev(g) // nt, 0, prev(g) % nt))],
        out_shape=(jax.ShapeDtypeStruct((b, s, d), F32),
                   jax.ShapeDtypeStruct((b, s * (d // LANE), LANE), F32),
                   jax.ShapeDtypeStruct((b, N_EXPERTS, s), F32)),
        scratch_shapes=[pltpu.VMEM((tile, d), F32), pltpu.VMEM((tile, d), F32),
                        pltpu.VMEM((3, tile, d), F32), pltpu.SemaphoreType.DMA((3,))],
        compiler_params=_cparams(("arbitrary",)),
        name="merge",
    )(rnn, attn, sg, x, mod, mod, g2, wr, wm, wo, wrt)


def _excl_cumsum_lanes(x01, tri):
    rows, length = x01.shape
    offset = jnp.zeros((rows, 1), F32)
    pieces = []
    for blk in range(length // LANE):
        xb = x01[:, blk * LANE:(blk + 1) * LANE]
        inc = jnp.dot(xb.astype(BF16), tri, preferred_element_type=F32)
        pieces.append(inc - xb + offset)
        offset = offset + jnp.sum(xb, axis=-1, keepdims=True)
    return jnp.concatenate(pieces, axis=-1)


def _topk_kernel(aff_ref, idx_ref, gate_ref, pos_s, sel_s, *, cap):
    a = aff_ref[0]
    n_e, length = a.shape

    def as_float(bits):
        return lax.bitcast_convert_type(bits, F32)

    def bisect(i, cand):
        t = cand | (jnp.int32(1) << (30 - i))
        cnt = jnp.sum((a >= as_float(t)).astype(jnp.int32), axis=-1, keepdims=True)
        return jnp.where(cnt >= cap, t, cand)

    thr_bits = lax.fori_loop(0, 31, bisect, jnp.zeros((n_e, 1), jnp.int32))
    thr = as_float(thr_bits)
    above = as_float(thr_bits + 1)
    gt = (a >= above).astype(F32)
    eq = jnp.logical_and(a >= thr, a < above).astype(F32)
    need = cap - jnp.sum(gt, axis=-1, keepdims=True)
    r_i = lax.broadcasted_iota(jnp.int32, (LANE, LANE), 0)
    c_i = lax.broadcasted_iota(jnp.int32, (LANE, LANE), 1)
    tri = (r_i <= c_i).astype(BF16)
    tie_rank = _excl_cumsum_lanes(eq, tri)
    sel = gt + eq * (tie_rank < need).astype(F32)
    pos_s[...] = _excl_cumsum_lanes(sel, tri)
    sel_s[...] = sel

    n_hi = idx_ref.shape[2]
    tok = lax.broadcasted_iota(jnp.int32, (1, length), 1)
    tok_hi = (tok >> 6).astype(F32)
    tok_lo = (tok & 63).astype(F32)
    a_iota = lax.broadcasted_iota(jnp.int32, (n_hi, 1), 0)
    c_iota = lax.broadcasted_iota(jnp.int32, (SLOT_LO, 1), 0)

    def compact(e, carry):
        pos_row = pos_s[pl.ds(e, 1), :].astype(jnp.int32)
        sel_row = sel_s[pl.ds(e, 1), :] > 0.5
        hot_a = jnp.where(jnp.logical_and((pos_row // SLOT_LO) == a_iota, sel_row), 1.0, 0.0)
        hot_c = jnp.where((pos_row % SLOT_LO) == c_iota, 1.0, 0.0).astype(BF16)
        g = aff_ref[0, pl.ds(e, 1), :]
        g1 = g.astype(BF16).astype(F32)
        g2 = (g - g1).astype(BF16).astype(F32)
        g3 = g - g1 - g2
        lhs = jnp.concatenate([hot_a * v for v in (tok_hi, tok_lo, g1, g2, g3)], axis=0).astype(BF16)
        res = lax.dot_general(lhs, hot_c, (((1,), (1,)), ((), ())), preferred_element_type=F32)
        part = [res[v * n_hi:(v + 1) * n_hi] for v in range(5)]
        idx_ref[0, e] = (part[0] * 64.0 + part[1]).astype(jnp.int32)
        gate_ref[0, e] = part[2] + part[3] + part[4]
        return carry

    lax.fori_loop(0, n_e, compact, 0)


def _topk(aff_t, cap):
    b, n_e, length = aff_t.shape
    assert length <= 64 * 256, "token ids travel through the compaction matmul as two bf16-exact digits (t >> 6, t & 63)"
    group = TOPK_SAMPLES if b % TOPK_SAMPLES == 0 else 1
    aff_t = aff_t.reshape(b // group, group * n_e, length)
    b, n_e = b // group, group * n_e
    n_hi = max(cap // SLOT_LO, SUBLANE)
    blk = lambda bb: (bb, 0, 0)
    blk4 = lambda bb: (bb, 0, 0, 0)
    idx, gate = pl.pallas_call(
        functools.partial(_topk_kernel, cap=cap),
        grid=(b,),
        in_specs=[pl.BlockSpec((1, n_e, length), blk)],
        out_specs=[pl.BlockSpec((1, n_e, n_hi, SLOT_LO), blk4), pl.BlockSpec((1, n_e, n_hi, SLOT_LO), blk4)],
        out_shape=(jax.ShapeDtypeStruct((b, n_e, n_hi, SLOT_LO), jnp.int32),
                   jax.ShapeDtypeStruct((b, n_e, n_hi, SLOT_LO), F32)),
        scratch_shapes=[pltpu.VMEM((n_e, length), F32), pltpu.VMEM((n_e, length), F32)],
        compiler_params=_cparams(("arbitrary",)),
        name="topk",
    )(aff_t)
    flat = lambda a: a.reshape(b * n_e, 1, n_hi * SLOT_LO)[:, :, :cap]
    return flat(idx), flat(gate)


def _gather_kernel(idx_ref, h_ref, o_ref, rows_s, *, cap, d):
    tiles = d // LANE

    def body(j, carry):
        src = pl.multiple_of(idx_ref[0, 0, j] * tiles, tiles)
        dst = pl.multiple_of(j * tiles, tiles)
        rows_s[pl.ds(dst, tiles), :] = h_ref[0, pl.ds(src, tiles), :]
        return carry

    lax.fori_loop(0, cap, body, 0, unroll=8)
    o_ref[0, 0] = jnp.concatenate([rows_s[pl.ds(t, cap, stride=tiles), :] for t in range(tiles)],
                                  axis=-1).astype(BF16)


def _gather(idx, h2t, cap, d):
    b, rows, lanes = h2t.shape
    assert d // lanes == SUBLANE and lanes == LANE
    return pl.pallas_call(
        functools.partial(_gather_kernel, cap=cap, d=d),
        grid=(b, N_EXPERTS),
        in_specs=[pl.BlockSpec((1, 1, cap), lambda bb, e: (bb * N_EXPERTS + e, 0, 0), memory_space=pltpu.SMEM),
                  pl.BlockSpec((1, rows, lanes), lambda bb, e: (bb, 0, 0))],
        out_specs=pl.BlockSpec((1, 1, cap, d), lambda bb, e: (bb, e, 0, 0)),
        out_shape=jax.ShapeDtypeStruct((b, N_EXPERTS, cap, d), BF16),
        scratch_shapes=[pltpu.VMEM((cap * (d // LANE), LANE), F32)],
        compiler_params=_cparams(("arbitrary", "arbitrary")),
        name="gather",
    )(idx, h2t)


def _moe_kernel(idx_ref, gate_ref, x_ref, wg_ref, wu_ref, wd_ref, mod_ref, res_ref, o_ref, y_even, y_odd, *, d, cap):
    g = pl.program_id(0)
    e_prev = jnp.maximum(g - 1, 0) % N_EXPERTS
    piece = res_ref.shape[1]

    @pl.when(g == 0)
    def _():
        y_odd[...] = jnp.zeros_like(y_odd)

    @pl.when(e_prev == 0)
    def _():
        o_ref[...] = jnp.zeros_like(o_ref)

    def step(y_new, y_prev):
        xb = x_ref[0, 0]
        gg = jnp.dot(xb, wg_ref[0], preferred_element_type=F32)
        uu = jnp.dot(xb, wu_ref[0], preferred_element_type=F32)
        hid = (gg * jax.nn.sigmoid(gg) * uu).astype(BF16)
        y_new[...] = jnp.dot(hid, wd_ref[0], preferred_element_type=F32) * mod_ref[0, :, 5 * d:6 * d]

        rows = pl.ds(pl.multiple_of(e_prev * piece, piece), piece)
        o_ref[0, rows, :] = o_ref[0, rows, :] + res_ref[0]
        for j in range(cap):
            r = idx_ref[0, 0, j]
            o_ref[0, pl.ds(r, 1), :] = o_ref[0, pl.ds(r, 1), :] + gate_ref[0, 0, j] * y_prev[j:j + 1, :]

    @pl.when(g % 2 == 0)
    def _():
        step(y_even, y_odd)

    @pl.when(g % 2 == 1)
    def _():
        step(y_odd, y_even)


def _moe(idx, gate, xg, w_gate, w_up, w_down, mod, x_new):
    b, n_e, cap, d = xg.shape
    s = x_new.shape[1]
    f = w_gate.shape[2]
    n_items = b * n_e
    cur = lambda g: jnp.minimum(g, n_items - 1)
    prev = lambda g: jnp.maximum(g - 1, 0)
    smem = lambda: pl.BlockSpec((1, 1, cap), lambda g: (prev(g), 0, 0), memory_space=pltpu.SMEM)
    wspec = lambda shape: pl.BlockSpec(shape, lambda g: (cur(g) % n_e, 0, 0))
    return pl.pallas_call(
        functools.partial(_moe_kernel, d=d, cap=cap),
        grid=(n_items + 1,),
        in_specs=[smem(), smem(),
                  pl.BlockSpec((1, 1, cap, d), lambda g: (cur(g) // n_e, cur(g) % n_e, 0, 0)),
                  wspec((1, d, f)), wspec((1, d, f)), wspec((1, f, d)),
                  pl.BlockSpec((1, 1, N_MOD * d), lambda g: (2 * (cur(g) // n_e), 0, 0)),
                  pl.BlockSpec((1, s // n_e, d), lambda g: (prev(g) // n_e, prev(g) % n_e, 0))],
        out_specs=pl.BlockSpec((1, s, d), lambda g: (prev(g) // n_e, 0, 0), pipeline_mode=pl.Buffered(1)),
        out_shape=jax.ShapeDtypeStruct((b, s, d), F32),
        scratch_shapes=[pltpu.VMEM((cap, d), F32), pltpu.VMEM((cap, d), F32)],
        compiler_params=_cparams(("arbitrary",)),
        name="moe",
    )(idx, gate, xg, w_gate, w_up, w_down, mod, x_new)


def _pad_lanes(a, width):
    return jnp.pad(a, [(0, 0)] * (a.ndim - 1) + [(0, width - a.shape[-1])])


def _rope_tables(s, lc, g_q_nope, g_q_rope, g_k_nope, g_k_rope):
    ltot = s + lc
    n_pairs = QK_ROPE // 4
    lane = jnp.arange(LANE)
    in_rope = jnp.logical_and(lane >= QK_NOPE, lane < QK_NOPE + QK_ROPE)
    angle_id = jnp.where(in_rope, (lane - QK_NOPE) % HALF_ROPE, 0)
    inv_freq = ROPE_BASE ** (-jnp.arange(n_pairs, dtype=F32) / n_pairs)
    freq = jnp.where(in_rope, inv_freq[angle_id % n_pairs], 0.0)
    t = jnp.arange(ltot)
    pos = jnp.where((angle_id >= n_pairs)[None, :], (t % GRID_W)[:, None], (t // GRID_W)[:, None]).astype(F32)
    ang = jnp.where((t < s)[:, None], pos * freq[None, :], 0.0)
    cos, sin = jnp.cos(ang), jnp.sin(ang)
    z_nope = jnp.zeros((QK_NOPE,), F32)
    z_tail = jnp.zeros((LANE - QK_NOPE - QK_ROPE,), F32)

    def tables(g_nope, g_rope, scale):
        g_cos = jnp.concatenate([g_nope, g_rope, z_tail]) * scale
        g_sin = jnp.concatenate([z_nope, -g_rope[HALF_ROPE:], g_rope[:HALF_ROPE], z_tail]) * scale
        return cos * g_cos[None, :], sin * g_sin[None, :]

    tqc, tqs = tables(g_q_nope, g_q_rope, ATTN_SCALE * LOG2_E)
    tkc, tks = tables(z_nope, g_k_rope, 1.0)
    gkn = _pad_lanes(g_k_nope[None, :], LANE)
    return tqc, tqs, tkc, tks, gkn


def _swap_halves(a):
    return jnp.concatenate([a[..., HALF_ROPE:], a[..., :HALF_ROPE]], axis=-1)


def _layout_weights(w_in, w_uq, w_uk, w_uv, d):
    o_q = 2 * d
    o_kv = o_q + Q_LORA
    o_kr = o_kv + KV_LORA
    o_g = o_kr + QK_ROPE
    kr = w_in[:, o_kr:o_g]
    zl = jnp.zeros((w_in.shape[0], QK_NOPE), F32)
    kr_a = _pad_lanes(jnp.concatenate([zl, kr], axis=-1), LANE)
    kr_s = _pad_lanes(jnp.concatenate([zl, _swap_halves(kr)], axis=-1), LANE)
    win_p = jnp.concatenate([w_in[:, :o_kr], kr_a, kr_s, w_in[:, o_g:]], axis=-1).astype(BF16)

    uq = w_uq.reshape(Q_LORA, MLA_HEADS, QK_NOPE + QK_ROPE)
    nope, rope = uq[..., :QK_NOPE], uq[..., QK_NOPE:]
    wuq_p = _pad_lanes(jnp.concatenate([nope, rope], axis=-1), HEAD_PAD).reshape(Q_LORA, -1).astype(BF16)
    wuq_s = _pad_lanes(jnp.concatenate([jnp.zeros_like(nope), _swap_halves(rope)], axis=-1),
                       HEAD_PAD).reshape(Q_LORA, -1).astype(BF16)
    wuk_p = _pad_lanes(w_uk.reshape(KV_LORA, MLA_HEADS, QK_NOPE), HEAD_PAD).reshape(KV_LORA, -1).astype(BF16)
    wuv_p = _pad_lanes(w_uv.reshape(KV_LORA, MLA_HEADS, V_HEAD), HEAD_PAD).reshape(KV_LORA, -1).astype(BF16)
    return win_p, wuq_p, wuq_s, wuk_p, wuv_p


def _pair_blocks(w):
    *lead, nb, n, _ = w.shape
    w = w.reshape(*lead, nb // 2, 2, n, n)
    paired = jnp.einsum('...irc,ij->...irjc', w, jnp.eye(2, dtype=w.dtype))
    return paired.reshape(*lead, nb // 2, 2 * n, 2 * n).astype(BF16)


def kernel(x, c, ctx, c_ctx, w_ada, b_ada, g_norm1, g_norm2, w_in, conv_w, conv_b, rg_wa, rg_ba, rg_wx, rg_bx,
           rg_lambda, w_rnn_out, g_q_lora, w_uq, g_kv_lora, w_uk, w_uv, g_q_nope, g_q_rope, g_k_nope, g_k_rope,
           w_mla_out, w_o, w_router, w_e_gate, w_e_up, w_e_down):
    assert w_ada.shape[0] == 1, "single-layer kernel"
    b, s, d = x.shape
    lc = ctx.shape[1]
    tile = lc
    n_lat = s // tile
    cap = CAPACITY_FACTOR * s // N_EXPERTS

    pad_rows = (-(b + 1)) % SUBLANE
    cc = jnp.concatenate([c, c_ctx[None, :], jnp.zeros((pad_rows, d), F32)], axis=0)
    mod_all = _ada(cc, w_ada[0], b_ada[0])
    mod = jnp.stack([mod_all[:b], jnp.broadcast_to(mod_all[b], (b, N_MOD * d))], axis=1).reshape(2 * b, 1, N_MOD * d)

    win_p, wuq_p, wuq_s, wuk_p, wuv_p = _layout_weights(w_in[0], w_uq[0], w_uk[0], w_uv[0], d)
    tqc, tqs, tkc, tks, gkn = _rope_tables(s, lc, g_q_nope[0], g_q_rope[0], g_k_nope[0], g_k_rope[0])
    z0, gz, q, k, v, sg = _inproj(x, ctx, mod, g_norm1, win_p, g_q_lora, wuq_p, wuq_s, g_kv_lora, wuk_p,
                                  wuv_p, gkn, tqc, tqs, tkc, tks, tile)

    gate_w = _pair_blocks(jnp.stack([rg_wa[0], rg_wx[0]]))
    h_f, xr = _rnn(z0, gate_w[0, 0], gate_w[1, 0], rg_ba[0, 0:1], rg_bx[0, 0:1],
                   rg_lambda[0, 0:1], tile, n_lat, False, conv=(conv_w[0], conv_b))
    rnn = _rnn(xr, gate_w[0, 1], gate_w[1, 1], rg_ba[0, 1:2], rg_bx[0, 1:2],
               rg_lambda[0, 1:2], tile, n_lat, True, hf=h_f, gz=gz)

    attn, (wg_b, wu_b, wd_b) = _attention(q, k, v, s, 2 * MXU_COLS, (w_e_gate[0], w_e_up[0], w_e_down[0]))

    x_new, h2, aff = _merge(rnn, attn, sg, x, mod, g_norm2, w_rnn_out[0].astype(BF16), w_mla_out[0].astype(BF16),
                            w_o[0].astype(BF16), _pad_lanes(w_router[0], LANE).astype(BF16), 2 * tile)

    idx, gate = _topk(aff, cap)
    xg = _gather(idx, h2, cap, d)
    return _moe(idx, gate, xg, wg_b, wu_b, wd_b, mod, x_new)
```
